```python
import math
import jax
import jax.numpy as jnp
from jax import lax
import numpy as np

D_MODEL = 1024
BATCH = 16
SEQ = 2048
DEPTH = 4

GRID_W = 64
CTX_LEN = 256
N_MIXERS = 4
MIX_DIFF, MIX_GMLP, MIX_NATTEN, MIX_FOURIER = 0, 1, 2, 3
CTX_READERS = (MIX_DIFF, MIX_NATTEN)
EPS = 1e-6
NEG_INF = -1e30
ROPE_BASE = 10000.0

DA_HEADS = 8
DA_HEAD_DIM = 64
DA_V_DIM = 2 * DA_HEAD_DIM
DA_WIDTH = DA_HEADS * DA_V_DIM
DA_Q_BLOCK = 128

GM_CHUNK = 128
GM_GROUPS = 8
GM_WIDTH = 2 * D_MODEL
GM_GROUP_DIM = GM_WIDTH // GM_GROUPS

NA_HEADS = 16
NA_HEAD_DIM = D_MODEL // NA_HEADS
NA_WIDTH = NA_HEADS * NA_HEAD_DIM
NA_WIN_R = 8
NA_WIN_C = 16
NA_COL_BLOCK = 16
NA_COL_BAND = 2 * NA_COL_BLOCK

FN_GROUPS = 4

N_EXPERTS = 16
EXPERT_DIM = D_MODEL
EC_CAPACITY_FACTOR = 2

kernel_name = 'hybrid_diffusion_trunk'


def rmsnorm(x, g):
    xf = x.astype(jnp.float32)
    y = xf * lax.rsqrt(jnp.mean(xf * xf, axis=-1, keepdims=True) + EPS)
    return (y * g.astype(jnp.float32)).astype(x.dtype)


def adaln_params(cond, w, b, n_mod):
    m = jax.nn.silu(cond) @ w[:, :n_mod * D_MODEL] + b[:n_mod * D_MODEL]
    return [t[:, None, :] for t in jnp.split(m, n_mod, axis=-1)]


def modulate(h, shift, scale):
    return h * (1.0 + scale) + shift


def axial_rope_angles(n_tok, dim):
    t = jnp.arange(n_tok, dtype=jnp.int32)
    n_freq = dim // 4
    inv = ROPE_BASE ** (-jnp.arange(n_freq, dtype=jnp.float32) / n_freq)
    rows = (t // GRID_W).astype(jnp.float32)
    cols = (t % GRID_W).astype(jnp.float32)
    return rows[:, None] * inv, cols[:, None] * inv


def rope_axis(x, ang):
    x1, x2 = jnp.split(x, 2, axis=-1)
    cos = jnp.cos(ang)[:, None, :].astype(x.dtype)
    sin = jnp.sin(ang)[:, None, :].astype(x.dtype)
    return jnp.concatenate([x1 * cos - x2 * sin, x1 * sin + x2 * cos], axis=-1)


def axial_rope(x, ang_r, ang_c):
    xr, xc = jnp.split(x, 2, axis=-1)
    return jnp.concatenate([rope_axis(xr, ang_r), rope_axis(xc, ang_c)], axis=-1)


def query_blocks(fn, q, block):
    b, n = q.shape[:2]
    qb = jnp.moveaxis(q.reshape((b, n // block, block) + q.shape[2:]), 1, 0)
    out = jnp.moveaxis(lax.map(fn, qb), 0, 1)
    return out.reshape((b, n) + out.shape[3:])


def dense_attention(q, k, v):
    s = jnp.einsum('bqhd,bkhd->bhqk', q, k).astype(jnp.float32) * (q.shape[-1] ** -0.5)
    p = jax.nn.softmax(s, axis=-1).astype(v.dtype)
    return jnp.einsum('bhqk,bkhd->bqhd', p, v)


def diff_project(h, w_in, with_q):
    b, n, _ = h.shape
    k, v = jnp.split(h @ w_in[:, DA_WIDTH:], 2, axis=-1)
    k = k.reshape(b, n, DA_HEADS, 2, DA_HEAD_DIM)
    v = v.reshape(b, n, DA_HEADS, DA_V_DIM)
    q = (h @ w_in[:, :DA_WIDTH]).reshape(b, n, DA_HEADS, 2, DA_HEAD_DIM) if with_q else None
    return q, k, v


def rope_heads(t, ang_r, ang_c):
    b, n = t.shape[:2]
    return axial_rope(t.reshape(b, n, -1, DA_HEAD_DIM), ang_r, ang_c).reshape(t.shape)


def diff_attend(q, k, v, lam):
    s = jnp.einsum('bqhjd,bkhjd->bhjqk', q, k).astype(jnp.float32) * (DA_HEAD_DIM ** -0.5)
    p = jax.nn.softmax(s, axis=-1)
    a = (p[:, :, 0] - lam * p[:, :, 1]).astype(v.dtype)
    return jnp.einsum('bhqk,bkhe->bqhe', a, v)


def diff_output(o, subln_g, lam_init, w_out):
    b, n = o.shape[:2]
    o = rmsnorm(o, subln_g) * (1.0 - lam_init)
    return o.reshape(b, n, DA_WIDTH) @ w_out


def diff_attention_mixer(hx, hc, w_in, lq1, lk1, lq2, lk2, subln_g, w_out, lam_init, ang_r, ang_c, ctx_out):
    q, k, v = diff_project(hx, w_in, True)
    q = rope_heads(q, ang_r, ang_c)
    k = rope_heads(k, ang_r, ang_c)
    qc, kc, vc = diff_project(hc, w_in, ctx_out)
    lam = (jnp.exp(jnp.sum(lq1.astype(jnp.float32) * lk1.astype(jnp.float32)))
           - jnp.exp(jnp.sum(lq2.astype(jnp.float32) * lk2.astype(jnp.float32))) + lam_init)
    k_all = jnp.concatenate([kc, k], axis=1)
    v_all = jnp.concatenate([vc, v], axis=1)
    o = query_blocks(lambda qb: diff_attend(qb, k_all, v_all, lam), q, DA_Q_BLOCK)
    y = diff_output(o, subln_g, lam_init, w_out)
    yc = diff_output(diff_attend(qc, kc, vc, lam), subln_g, lam_init, w_out) if ctx_out else None
    return y, yc


def chunk_gmlp(h, w_in, v_g, w_s, b_s, w_out):
    b, n, _ = h.shape
    u, v = jnp.split(jax.nn.gelu(h @ w_in), 2, axis=-1)
    v = rmsnorm(v, v_g)
    v = v.reshape(b, n // GM_CHUNK, GM_CHUNK, GM_GROUPS, GM_GROUP_DIM)
    sv = jnp.einsum('gpq,bcqge->bcpge', w_s, v) + jnp.swapaxes(b_s, 0, 1)[:, :, None]
    return (u * sv.reshape(b, n, GM_WIDTH)) @ w_out


def natten_latent(q, k, v, k_ctx, v_ctx, rpb):
    b, n, h, d = q.shape
    rows = n // GRID_W
    wr = min(NA_WIN_R, rows)
    n_cb = GRID_W // NA_COL_BLOCK
    qg = q.reshape(b, rows, GRID_W, h, d)
    kg = k.reshape(b, rows, GRID_W, h, d)
    vg = v.reshape(b, rows, GRID_W, h, d)
    band_start = np.clip(np.arange(n_cb) * NA_COL_BLOCK - NA_COL_BLOCK // 2, 0, GRID_W - NA_COL_BAND)
    band_cols = band_start[:, None] + np.arange(NA_COL_BAND)
    q_cols = np.arange(GRID_W).reshape(n_cb, NA_COL_BLOCK)
    win_c0 = np.clip(q_cols - NA_WIN_C // 2, 0, GRID_W - NA_WIN_C)
    kc_b = band_cols[:, None, :]
    col_ok = (kc_b >= win_c0[:, :, None]) & (kc_b < win_c0[:, :, None] + NA_WIN_C)
    dc_idx = np.clip(kc_b - q_cols[:, :, None] + NA_WIN_C - 1, 0, 2 * NA_WIN_C - 2)
    scale = d ** -0.5

    def one_row(r):
        r0 = jnp.clip(r - wr // 2, 0, rows - wr)
        kb = lax.dynamic_slice_in_dim(kg, r0, wr, axis=1)[:, :, band_cols]
        vb = lax.dynamic_slice_in_dim(vg, r0, wr, axis=1)[:, :, band_cols]
        qr = lax.dynamic_index_in_dim(qg, r, axis=1, keepdims=False).reshape(b, n_cb, NA_COL_BLOCK, h, d)
        dr = r0 + jnp.arange(wr) - r
        bias = rpb[:, dr + NA_WIN_R - 1][:, :, dc_idx]
        bias = jnp.transpose(bias, (0, 2, 3, 1, 4))
        s_loc = jnp.einsum('bjqhd,brjkhd->bhjqrk', qr, kb).astype(jnp.float32) * scale + bias
        s_loc = jnp.where(col_ok[:, :, None, :], s_loc, NEG_INF)
        s_ctx = jnp.einsum('bjqhd,bkhd->bhjqk', qr, k_ctx).astype(jnp.float32) * scale
        s = jnp.concatenate([s_loc.reshape(b, h, n_cb, NA_COL_BLOCK, wr * NA_COL_BAND), s_ctx], axis=-1)
        p = jax.nn.softmax(s, axis=-1).astype(v.dtype)
        p_loc = p[..., :wr * NA_COL_BAND].reshape(b, h, n_cb, NA_COL_BLOCK, wr, NA_COL_BAND)
        p_ctx = p[..., wr * NA_COL_BAND:]
        o = (jnp.einsum('bhjqrk,brjkhd->bjqhd', p_loc, vb)
             + jnp.einsum('bhjqk,bkhd->bjqhd', p_ctx, v_ctx))
        return o.reshape(b, GRID_W, h, d)

    out = lax.map(one_row, jnp.arange(rows))
    return jnp.moveaxis(out, 0, 1).reshape(b, n, h, d)


def natten_mixer(hx, hc, w_in, rpb, w_out, ctx_out):
    b, n, _ = hx.shape
    lc = hc.shape[1]
    q, k, v = [t.reshape(b, n, NA_HEADS, NA_HEAD_DIM) for t in jnp.split(hx @ w_in, 3, axis=-1)]
    kc, vc = [t.reshape(b, lc, NA_HEADS, NA_HEAD_DIM) for t in jnp.split(hc @ w_in[:, NA_WIDTH:], 2, axis=-1)]
    y = natten_latent(q, k, v, kc, vc, rpb).reshape(b, n, NA_WIDTH) @ w_out
    yc = None
    if ctx_out:
        qc = (hc @ w_in[:, :NA_WIDTH]).reshape(b, lc, NA_HEADS, NA_HEAD_DIM)
        yc = dense_attention(qc, kc, vc).reshape(b, lc, NA_WIDTH) @ w_out
    return y, yc


def fourier_mix(h, w_out):
    b, n, d = h.shape
    hg = h.astype(jnp.float32).reshape(b, n, FN_GROUPS, d // FN_GROUPS)
    f = jnp.real(jnp.fft.fftn(hg, axes=(1, 3), norm='ortho'))
    return f.reshape(b, n, d).astype(h.dtype) @ w_out


def expert_choice_moe(h, w_router, w_gate, w_up, w_down):
    b, n, d = h.shape
    cap = EC_CAPACITY_FACTOR * n // N_EXPERTS
    aff = jax.nn.softmax((h @ w_router).astype(jnp.float32), axis=-1)
    top_aff, top_idx = lax.top_k(jnp.swapaxes(aff, 1, 2), cap)
    idx = top_idx.reshape(b, N_EXPERTS * cap)
    bidx = jnp.arange(b)[:, None]
    xs = h[bidx, idx].reshape(b, N_EXPERTS, cap, d)
    hid = jax.nn.silu(jnp.einsum('becd,edf->becf', xs, w_gate)) * jnp.einsum('becd,edf->becf', xs, w_up)
    ys = jnp.einsum('becf,efd->becd', hid, w_down) * top_aff[..., None].astype(h.dtype)
    return jnp.zeros_like(h).at[bidx, idx].add(ys.reshape(b, N_EXPERTS * cap, d))


def setup_inputs(seed: int = 0) -> dict:
    key = jax.random.key(seed)
    keys = iter(jax.random.split(key, 40))

    def nrm(shape, scale):
        return jax.random.normal(next(keys), shape, jnp.float32) * scale

    def gain(shape):
        return 1.0 + nrm(shape, 0.02)

    n_a, n_b, n_c, n_d = [len(range(kind, DEPTH, N_MIXERS)) for kind in range(N_MIXERS)]
    d = D_MODEL
    return {
        'x': nrm((BATCH, SEQ, d), 1.0),
        'c': nrm((BATCH, d), 1.0),
        'ctx': nrm((BATCH, CTX_LEN, d), 1.0),
        'c_ctx': nrm((d,), 1.0),
        'ada_w': nrm((DEPTH, d, 6 * d), 0.5 * d ** -0.5),
        'ada_b': nrm((DEPTH, 6 * d), 0.01),
        'mixer_norm_g': gain((DEPTH, d)),
        'moe_norm_g': gain((DEPTH, d)),
        'router_w': nrm((DEPTH, d, N_EXPERTS), d ** -0.5),
        'moe_w_gate': nrm((DEPTH, N_EXPERTS, d, EXPERT_DIM), d ** -0.5),
        'moe_w_up': nrm((DEPTH, N_EXPERTS, d, EXPERT_DIM), d ** -0.5),
        'moe_w_down': nrm((DEPTH, N_EXPERTS, EXPERT_DIM, d), EXPERT_DIM ** -0.5),
        'da_w_in': nrm((n_a, d, 3 * DA_WIDTH), d ** -0.5),
        'da_lambda_q1': nrm((n_a, DA_HEAD_DIM), 0.1),
        'da_lambda_k1': nrm((n_a, DA_HEAD_DIM), 0.1),
        'da_lambda_q2': nrm((n_a, DA_HEAD_DIM), 0.1),
        'da_lambda_k2': nrm((n_a, DA_HEAD_DIM), 0.1),
        'da_subln_g': gain((n_a, DA_V_DIM)),
        'da_w_out': nrm((n_a, DA_WIDTH, d), DA_WIDTH ** -0.5),
        'gm_w_in': nrm((n_b, d, 2 * GM_WIDTH), d ** -0.5),
        'gm_v_g': gain((n_b, GM_WIDTH)),
        'gm_w_s': nrm((n_b, GM_GROUPS, GM_CHUNK, GM_CHUNK), GM_CHUNK ** -0.5),
        'gm_b_s': 1.0 + nrm((n_b, GM_GROUPS, GM_CHUNK), 0.02),
        'gm_w_out': nrm((n_b, GM_WIDTH, d), GM_WIDTH ** -0.5),
        'na_w_in': nrm((n_c, d, 3 * NA_WIDTH), d ** -0.5),
        'na_rpb': nrm((n_c, NA_HEADS, 2 * NA_WIN_R - 1, 2 * NA_WIN_C - 1), 0.1),
        'na_w_out': nrm((n_c, NA_WIDTH, d), NA_WIDTH ** -0.5),
        'fn_w_out': nrm((n_d, d, d), d ** -0.5),
        'final_norm_g': gain((d,)),
    }


def reference(x, c, ctx, c_ctx, ada_w, ada_b, mixer_norm_g, moe_norm_g, router_w,
              moe_w_gate, moe_w_up, moe_w_down,
              da_w_in, da_lambda_q1, da_lambda_k1, da_lambda_q2, da_lambda_k2, da_subln_g, da_w_out,
              gm_w_in, gm_v_g, gm_w_s, gm_b_s, gm_w_out,
              na_w_in, na_rpb, na_w_out, fn_w_out, final_norm_g):
    n_tok = x.shape[1]
    ang_r, ang_c = axial_rope_angles(n_tok, DA_HEAD_DIM)
    readers = [i for i in range(DEPTH) if i % N_MIXERS in CTX_READERS]
    last_reader = max(readers) if readers else -1
    for i in range(DEPTH):
        kind, j = i % N_MIXERS, i // N_MIXERS
        need_ctx = i <= last_reader
        update_ctx = i < last_reader
        sh1, sc1, g1, sh2, sc2, g2 = adaln_params(c, ada_w[i], ada_b[i], 6)
        hx = modulate(rmsnorm(x, mixer_norm_g[i]), sh1, sc1)
        hc = None
        if need_ctx:
            cmod = adaln_params(c_ctx[None, :], ada_w[i], ada_b[i], 6 if update_ctx else 3)
            hc = modulate(rmsnorm(ctx, mixer_norm_g[i]), cmod[0], cmod[1])
        if kind == MIX_DIFF:
            lam_init = 0.8 - 0.6 * math.exp(-0.3 * i)
            y, yc = diff_attention_mixer(hx, hc, da_w_in[j], da_lambda_q1[j], da_lambda_k1[j],
                                         da_lambda_q2[j], da_lambda_k2[j], da_subln_g[j], da_w_out[j],
                                         lam_init, ang_r, ang_c, update_ctx)
        elif kind == MIX_GMLP:
            y = chunk_gmlp(hx, gm_w_in[j], gm_v_g[j], gm_w_s[j], gm_b_s[j], gm_w_out[j])
            yc = chunk_gmlp(hc, gm_w_in[j], gm_v_g[j], gm_w_s[j], gm_b_s[j], gm_w_out[j]) if update_ctx else None
        elif kind == MIX_NATTEN:
            y, yc = natten_mixer(hx, hc, na_w_in[j], na_rpb[j], na_w_out[j], update_ctx)
        else:
            y = fourier_mix(hx, fn_w_out[j])
            yc = fourier_mix(hc, fn_w_out[j]) if update_ctx else None
        x = x + g1 * y
        hx = modulate(rmsnorm(x, moe_norm_g[i]), sh2, sc2)
        x = x + g2 * expert_choice_moe(hx, router_w[i], moe_w_gate[i], moe_w_up[i], moe_w_down[i])
        if update_ctx:
            ctx = ctx + cmod[2] * yc
            hc = modulate(rmsnorm(ctx, moe_norm_g[i]), cmod[3], cmod[4])
            ctx = ctx + cmod[5] * expert_choice_moe(hc, router_w[i], moe_w_gate[i], moe_w_up[i], moe_w_down[i])
    return rmsnorm(x, final_norm_g)
```

```python
import functools
import math

import numpy as np
import jax
import jax.numpy as jnp
from jax import lax
from jax.experimental import pallas as pl
from jax.experimental.pallas import tpu as pltpu

F32 = jnp.float32
BF16 = jnp.bfloat16

D_MODEL = 1024
DEPTH = 4
GRID_W = 64
N_MIXERS = 4
MIX_DIFF, MIX_GMLP, MIX_NATTEN, MIX_FOURIER = 0, 1, 2, 3
CTX_READERS = (MIX_DIFF, MIX_NATTEN)
EPS = 1e-6
NEG_INF = -1e30
ROPE_BASE = 10000.0

DA_HEADS = 8
DA_HEAD_DIM = 64
DA_V_DIM = 2 * DA_HEAD_DIM
DA_WIDTH = DA_HEADS * DA_V_DIM

GM_CHUNK = 128
GM_GROUPS = 8
GM_WIDTH = 2 * D_MODEL
GM_GROUP_DIM = GM_WIDTH // GM_GROUPS

NA_HEADS = 16
NA_HEAD_DIM = D_MODEL // NA_HEADS
NA_WIDTH = NA_HEADS * NA_HEAD_DIM
NA_WIN_R = 8
NA_WIN_C = 16
NA_ROW_BLOCK = 4
NA_SLAB_ROWS = NA_ROW_BLOCK + NA_WIN_R

FN_GROUPS = 4

N_EXPERTS = 16
EC_CAPACITY_FACTOR = 2

LANES = 128
VMEM_LIMIT = 56 * 1024 * 1024


def _cparams(sem, vmem=None):
    return pltpu.CompilerParams(dimension_semantics=sem, vmem_limit_bytes=vmem)


def _rms(x):
    return x * lax.rsqrt(jnp.mean(x * x, axis=-1, keepdims=True) + EPS)


def _ada_kernel(c_ref, w_ref, b_ref, o_ref):
    c = c_ref[...]
    h = (c * jax.nn.sigmoid(c)).astype(BF16)
    o_ref[0] = jnp.dot(h, w_ref[0].astype(BF16), preferred_element_type=F32) + b_ref[0]


def ada_params(cond, ada_w, ada_b):
    r = cond.shape[0]
    depth, d, n_out = ada_w.shape
    tn = 1024
    return pl.pallas_call(
        _ada_kernel,
        grid=(depth, n_out // tn),
        in_specs=[pl.BlockSpec((r, d), lambda l, j: (0, 0)),
                  pl.BlockSpec((1, d, tn), lambda l, j: (l, 0, j)),
                  pl.BlockSpec((1, 1, tn), lambda l, j: (l, 0, j))],
        out_specs=pl.BlockSpec((1, r, tn), lambda l, j: (l, 0, j)),
        out_shape=jax.ShapeDtypeStruct((depth, r, n_out), F32),
        compiler_params=_cparams(("parallel", "parallel")),
        name="ada_params",
    )(cond, ada_w, ada_b.reshape(depth, 1, n_out))


def _linear_kernel(*refs, has_norm, has_res, rope_tiles, tn):
    it = iter(refs)
    x_ref = next(it)
    if has_norm:
        g_ref, sh_ref, sc_ref = next(it), next(it), next(it)
    w_ref = next(it)
    if rope_tiles:
        cos_ref, sin_ref = next(it), next(it)
    if has_res:
        res_ref, gate_ref = next(it), next(it)
    o_ref = next(it)
    hb_ref = next(it)
    j = pl.program_id(2)

    if has_norm:
        @pl.when(j == 0)
        def _():
            h = _rms(x_ref[0]) * g_ref[...]
            hb_ref[...] = (h * (1.0 + sc_ref[0]) + sh_ref[0]).astype(BF16)

        hb = hb_ref[...]
    else:
        hb = x_ref[0].astype(BF16)
    y = jnp.dot(hb, w_ref[...], preferred_element_type=F32)

    def plain():
        if has_res:
            o_ref[0] = (res_ref[0] + gate_ref[0] * y).astype(o_ref.dtype)
        else:
            o_ref[0] = y.astype(o_ref.dtype)

    if rope_tiles:
        @pl.when(j < rope_tiles)
        def _():
            cos, sin = cos_ref[...], sin_ref[...]
            for s in range(tn // LANES):
                seg = y[:, s * LANES:(s + 1) * LANES]
                rot = pltpu.roll(seg, LANES // 2, axis=1)
                o_ref[0, :, s * LANES:(s + 1) * LANES] = (seg * cos + rot * sin).astype(o_ref.dtype)

        pl.when(j >= rope_tiles)(plain)
    else:
        plain()


def linear(x, w, *, norm=None, res=None, rope=None, rope_tiles=0, out_dtype=BF16, tm=512, tn=1024):
    b, n, k = x.shape
    m = w.shape[1]
    tm = min(tm, n)
    tn = min(tn, m)
    args, specs = [x], [pl.BlockSpec((1, tm, k), lambda bi, i, j: (bi, i, 0))]
    if norm is not None:
        g, sh, sc = norm
        args += [g, sh, sc]
        specs += [pl.BlockSpec((1, k), lambda bi, i, j: (0, 0)),
                  pl.BlockSpec((1, 1, k), lambda bi, i, j: (bi, 0, 0)),
                  pl.BlockSpec((1, 1, k), lambda bi, i, j: (bi, 0, 0))]
    args.append(w)
    specs.append(pl.BlockSpec((k, tn), lambda bi, i, j: (0, j)))
    if rope is not None:
        args += list(rope)
        specs += [pl.BlockSpec((tm, LANES), lambda bi, i, j: (i, 0))] * 2
    if res is not None:
        r, gate = res
        args += [r, gate]
        specs += [pl.BlockSpec((1, tm, tn), lambda bi, i, j: (bi, i, j)),
                  pl.BlockSpec((1, 1, tn), lambda bi, i, j: (bi, 0, j))]
    kern = functools.partial(_linear_kernel, has_norm=norm is not None, has_res=res is not None,
                             rope_tiles=rope_tiles if rope is not None else 0, tn=tn)
    return pl.pallas_call(
        kern,
        grid=(b, n // tm, m // tn),
        in_specs=specs,
        out_specs=pl.BlockSpec((1, tm, tn), lambda bi, i, j: (bi, i, j)),
        out_shape=jax.ShapeDtypeStruct((b, n, m), out_dtype),
        scratch_shapes=[pltpu.VMEM((tm, k), BF16)],
        compiler_params=_cparams(("parallel", "parallel", "arbitrary")),
        name="linear",
    )(*args)


def _diff_attn_kernel(q_ref, k_ref, v_ref, lq1_ref, lk1_ref, lq2_ref, lk2_ref, g_ref, o_ref, *, lam_init):
    q = q_ref[0]
    k = k_ref[0]
    v = v_ref[0]
    lam = (jnp.exp(jnp.sum(lq1_ref[...] * lk1_ref[...], keepdims=True))
           - jnp.exp(jnp.sum(lq2_ref[...] * lk2_ref[...], keepdims=True)) + lam_init)
    lane = lax.broadcasted_iota(jnp.int32, (1, LANES), 1)
    first = (lane % DA_HEAD_DIM) < (DA_HEAD_DIM // 2)
    zero = jnp.zeros_like(q)
    dims = (((1,), (1,)), ((), ()))
    scale = DA_HEAD_DIM ** -0.5

    def probs(qj):
        s = lax.dot_general(qj, k, dims, preferred_element_type=F32) * scale
        e = jnp.exp(s - jnp.max(s, axis=-1, keepdims=True))
        return e / jnp.sum(e, axis=-1, keepdims=True)

    a = probs(jnp.where(first, q, zero)) - lam * probs(jnp.where(first, zero, q))
    o = jnp.dot(a.astype(BF16), v, preferred_element_type=F32)
    o_ref[0] = (_rms(o) * g_ref[...] * (1.0 - lam_init)).astype(o_ref.dtype)


def diff_attention(q_arr, q_col0, kv_arr, k_col0, v_col0, lams, subln_g, lam_init, tq=256):
    b, nq, _ = q_arr.shape
    nk = kv_arr.shape[1]
    tq = min(tq, nq)
    vec = pl.BlockSpec((1, DA_HEAD_DIM), lambda bi, h, i: (0, 0))
    return pl.pallas_call(
        functools.partial(_diff_attn_kernel, lam_init=lam_init),
        grid=(b, DA_HEADS, nq // tq),
        in_specs=[pl.BlockSpec((1, tq, LANES), lambda bi, h, i: (bi, i, q_col0 + h)),
                  pl.BlockSpec((1, nk, LANES), lambda bi, h, i: (bi, 0, k_col0 + h)),
                  pl.BlockSpec((1, nk, LANES), lambda bi, h, i: (bi, 0, v_col0 + h)),
                  vec, vec, vec, vec,
                  pl.BlockSpec((1, DA_V_DIM), lambda bi, h, i: (0, 0))],
        out_specs=pl.BlockSpec((1, tq, LANES), lambda bi, h, i: (bi, i, h)),
        out_shape=jax.ShapeDtypeStruct((b, nq, DA_WIDTH), BF16),
        compiler_params=_cparams(("parallel", "parallel", "arbitrary")),
        name="diff_attention",
    )(q_arr, kv_arr, kv_arr, *lams, subln_g)


def _da_perm():
    perm = np.zeros(LANES, np.int32)
    for l in range(LANES):
        part, within = divmod(l, 64)
        j, rem = divmod(within, 32)
        seg, i = divmod(rem, 16)
        perm[l] = j * 64 + seg * 32 + part * 16 + i
    return perm


def _rope_tables(n_tok):
    t = jnp.arange(n_tok, dtype=jnp.int32)
    n_freq = DA_HEAD_DIM // 4
    inv = ROPE_BASE ** (-jnp.arange(n_freq, dtype=F32) / n_freq)
    ang_r = (t // GRID_W).astype(F32)[:, None] * inv
    ang_c = (t % GRID_W).astype(F32)[:, None] * inv
    cos32 = jnp.concatenate([jnp.cos(ang_r), jnp.cos(ang_c)], axis=1)
    sin32 = jnp.concatenate([jnp.sin(ang_r), jnp.sin(ang_c)], axis=1)
    cos = jnp.tile(cos32, (1, 4))
    sin = jnp.concatenate([-sin32, -sin32, sin32, sin32], axis=1)
    return cos, sin


def _gmlp_kernel(x_ref, g_ref, sh_ref, sc_ref, win_ref, vg_ref, ws_ref, bs_ref, wout_ref, gate_ref, o_ref, *, tm):
    x = x_ref[0]
    h = _rms(x) * g_ref[...]
    h = (h * (1.0 + sc_ref[0]) + sh_ref[0]).astype(BF16)
    uv = jax.nn.gelu(jnp.dot(h, win_ref[...], preferred_element_type=F32), approximate=True)
    u = uv[:, :GM_WIDTH]
    v = (_rms(uv[:, GM_WIDTH:]) * vg_ref[...]).astype(BF16)
    rows = []
    for c in range(tm // GM_CHUNK):
        cols = []
        for gi in range(GM_GROUPS):
            vv = v[c * GM_CHUNK:(c + 1) * GM_CHUNK, gi * GM_GROUP_DIM:(gi + 1) * GM_GROUP_DIM]
            sv = jnp.dot(ws_ref[gi], vv, preferred_element_type=F32) + bs_ref[:, gi:gi + 1]
            cols.append(sv)
        rows.append(jnp.concatenate(cols, axis=1))
    sv = jnp.concatenate(rows, axis=0) if len(rows) > 1 else rows[0]
    y = jnp.dot((u * sv).astype(BF16), wout_ref[...], preferred_element_type=F32)
    o_ref[0] = x + gate_ref[0] * y


def gmlp_layer(x, norm, w_in, v_g, w_s, b_s_t, w_out, gate, tm=256):
    b, n, d = x.shape
    tm = min(tm, n)
    g, sh, sc = norm
    full = lambda shape: pl.BlockSpec(shape, lambda bi, i: (0,) * len(shape))
    per_b = pl.BlockSpec((1, 1, d), lambda bi, i: (bi, 0, 0))
    return pl.pallas_call(
        functools.partial(_gmlp_kernel, tm=tm),
        grid=(b, n // tm),
        in_specs=[pl.BlockSpec((1, tm, d), lambda bi, i: (bi, i, 0)),
                  full((1, d)), per_b, per_b,
                  full(w_in.shape), full((1, GM_WIDTH)), full(w_s.shape), full(b_s_t.shape),
                  full(w_out.shape), per_b],
        out_specs=pl.BlockSpec((1, tm, d), lambda bi, i: (bi, i, 0)),
        out_shape=jax.ShapeDtypeStruct((b, n, d), F32),
        compiler_params=_cparams(("parallel", "arbitrary"), VMEM_LIMIT),
        name="gmlp_layer",
    )(x, g, sh, sc, w_in, v_g, w_s, b_s_t, w_out, gate)


def _natten_kernel(q_ref, k_ref, v_ref, kc_ref, vc_ref, bias_ref, o_ref, *, n_blocks):
    rb = pl.program_id(2)
    tq = NA_ROW_BLOCK * GRID_W
    slab = NA_SLAB_ROWS * GRID_W
    start = jnp.clip(rb - 1, 0, n_blocks - 3) * tq
    start = pl.multiple_of(start, tq)
    var = jnp.where(rb == 0, 0, jnp.where(rb == n_blocks - 1, 2, 1))
    q = q_ref[0]
    k = k_ref[0, pl.ds(start, slab), :]
    v = v_ref[0, pl.ds(start, slab), :]
    kc = kc_ref[0]
    vc = vc_ref[0]
    lane = lax.broadcasted_iota(jnp.int32, (1, LANES), 1)
    left = lane < NA_HEAD_DIM
    zero = jnp.zeros_like(q)
    dims = (((1,), (1,)), ((), ()))
    scale = NA_HEAD_DIM ** -0.5
    outs = []
    for hh in range(2):
        qh = jnp.where(left, q, zero) if hh == 0 else jnp.where(left, zero, q)
        s_loc = lax.dot_general(qh, k, dims, preferred_element_type=F32) * scale + bias_ref[hh, var]
        s_ctx = lax.dot_general(qh, kc, dims, preferred_element_type=F32) * scale
        m = jnp.maximum(jnp.max(s_loc, axis=-1, keepdims=True), jnp.max(s_ctx, axis=-1, keepdims=True))
        e_loc = jnp.exp(s_loc - m)
        e_ctx = jnp.exp(s_ctx - m)
        denom = jnp.sum(e_loc, axis=-1, keepdims=True) + jnp.sum(e_ctx, axis=-1, keepdims=True)
        o = (jnp.dot(e_loc.astype(BF16), v, preferred_element_type=F32)
             + jnp.dot(e_ctx.astype(BF16), vc, preferred_element_type=F32))
        outs.append(o / denom)
    o_ref[0] = jnp.where(left, outs[0], outs[1]).astype(o_ref.dtype)


def _natten_bias(rpb, rows):
    n_blocks = rows // NA_ROW_BLOCK
    i = np.arange(NA_ROW_BLOCK)[:, None, None, None]
    c = np.arange(GRID_W)[None, :, None, None]
    m = np.arange(NA_SLAB_ROWS)[None, None, :, None]
    kc = np.arange(GRID_W)[None, None, None, :]
    dr_idx, dc_idx, ok = [], [], []
    win_c0 = np.clip(c - NA_WIN_C // 2, 0, GRID_W - NA_WIN_C)
    col_ok = (kc >= win_c0) & (kc < win_c0 + NA_WIN_C)
    for rb in (0, 1, n_blocks - 1):
        slab0 = np.clip(rb - 1, 0, n_blocks - 3) * NA_ROW_BLOCK
        r = rb * NA_ROW_BLOCK + i
        r0 = np.clip(r - NA_WIN_R // 2, 0, rows - NA_WIN_R)
        kr = slab0 + m
        row_ok = (kr >= r0) & (kr < r0 + NA_WIN_R)
        shape = (NA_ROW_BLOCK, GRID_W, NA_SLAB_ROWS, GRID_W)
        dr_idx.append(np.broadcast_to(np.clip(kr - r + NA_WIN_R - 1, 0, 2 * NA_WIN_R - 2), shape))
        dc_idx.append(np.broadcast_to(np.clip(kc - c + NA_WIN_C - 1, 0, 2 * NA_WIN_C - 2), shape))
        ok.append(np.broadcast_to(row_ok & col_ok, shape))
    tq, slab = NA_ROW_BLOCK * GRID_W, NA_SLAB_ROWS * GRID_W
    dr_idx = np.stack(dr_idx).reshape(3, tq, slab)
    dc_idx = np.stack(dc_idx).reshape(3, tq, slab)
    ok = np.stack(ok).reshape(3, tq, slab)
    return jnp.where(ok[None], rpb[:, dr_idx, dc_idx], NEG_INF)


def natten(qkv, kv_ctx, bias):
    b, n, _ = qkv.shape
    lc = kv_ctx.shape[1]
    tq = NA_ROW_BLOCK * GRID_W
    n_blocks = n // tq
    hp = NA_HEADS // 2
    return pl.pallas_call(
        functools.partial(_natten_kernel, n_blocks=n_blocks),
        grid=(hp, b, n_blocks),
        in_specs=[pl.BlockSpec((1, tq, LANES), lambda p, bi, r: (bi, r, p)),
                  pl.BlockSpec((1, n, LANES), lambda p, bi, r: (bi, 0, hp + p)),
                  pl.BlockSpec((1, n, LANES), lambda p, bi, r: (bi, 0, 2 * hp + p)),
                  pl.BlockSpec((1, lc, LANES), lambda p, bi, r: (bi, 0, p)),
                  pl.BlockSpec((1, lc, LANES), lambda p, bi, r: (bi, 0, hp + p)),
                  pl.BlockSpec((2, 3, tq, NA_SLAB_ROWS * GRID_W), lambda p, bi, r: (p, 0, 0, 0))],
        out_specs=pl.BlockSpec((1, tq, LANES), lambda p, bi, r: (bi, r, p)),
        out_shape=jax.ShapeDtypeStruct((b, n, NA_WIDTH), BF16),
        compiler_params=_cparams(("parallel", "parallel", "arbitrary"), VMEM_LIMIT),
        name="natten",
    )(qkv, qkv, qkv, kv_ctx, kv_ctx, bias)


def _dft_tables(n, sign=1.0):
    k = jnp.arange(n, dtype=jnp.int32)
    ang = ((k[:, None] * k[None, :]) % n).astype(F32) * (2.0 * math.pi / n)
    return jnp.cos(ang), sign * jnp.sin(ang)


def _fourier_chan_kernel(x_ref, g_ref, sh_ref, sc_ref, wc_ref, o_ref):
    h = _rms(x_ref[0]) * g_ref[...]
    h = (h * (1.0 + sc_ref[0]) + sh_ref[0]).astype(BF16)
    gd = D_MODEL // FN_GROUPS
    for gi in range(FN_GROUPS):
        z = jnp.dot(h[:, gi * gd:(gi + 1) * gd], wc_ref[...], preferred_element_type=F32).astype(BF16)
        o_ref[0, 0, :, gi * gd:(gi + 1) * gd] = z[:, :gd]
        o_ref[0, 1, :, gi * gd:(gi + 1) * gd] = z[:, gd:]


def _fourier_pos_kernel(wp_ref, z_ref, wout_ref, x_ref, gate_ref, o_ref, *, scale):
    f = jnp.dot(wp_ref[...], z_ref[0], preferred_element_type=F32) * scale
    y = jnp.dot(f.astype(BF16), wout_ref[...], preferred_element_type=F32)
    o_ref[0] = x_ref[0] + gate_ref[0] * y


def fourier_layer(x, norm, w_out, gate, tm=512):
    b, n, d = x.shape
    gd = d // FN_GROUPS
    g, sh, sc = norm
    cc, sc_tab = _dft_tables(gd)
    wc = jnp.concatenate([cc, sc_tab], axis=1).astype(BF16)
    cn, sn = _dft_tables(n, -1.0)
    wp = jnp.concatenate([cn, sn], axis=1).astype(BF16)
    per_b = pl.BlockSpec((1, 1, d), lambda bi, i: (bi, 0, 0))
    z = pl.pallas_call(
        _fourier_chan_kernel,
        grid=(b, n // tm),
        in_specs=[pl.BlockSpec((1, tm, d), lambda bi, i: (bi, i, 0)),
                  pl.BlockSpec((1, d), lambda bi, i: (0, 0)), per_b, per_b,
                  pl.BlockSpec((gd, 2 * gd), lambda bi, i: (0, 0))],
        out_specs=pl.BlockSpec((1, 2, tm, d), lambda bi, i: (bi, 0, i, 0)),
        out_shape=jax.ShapeDtypeStruct((b, 2, n, d), BF16),
        compiler_params=_cparams(("parallel", "parallel")),
        name="fourier_chan",
    )(x, g, sh, sc, wc)
    z = z.reshape(b, 2 * n, d)
    return pl.pallas_call(
        functools.partial(_fourier_pos_kernel, scale=1.0 / math.sqrt(n * gd)),
        grid=(b, n // tm),
        in_specs=[pl.BlockSpec((tm, 2 * n), lambda bi, i: (i, 0)),
                  pl.BlockSpec((1, 2 * n, d), lambda bi, i: (bi, 0, 0)),
                  pl.BlockSpec((d, d), lambda bi, i: (0, 0)),
                  pl.BlockSpec((1, tm, d), lambda bi, i: (bi, i, 0)), per_b],
        out_specs=pl.BlockSpec((1, tm, d), lambda bi, i: (bi, i, 0)),
        out_shape=jax.ShapeDtypeStruct((b, n, d), F32),
        compiler_params=_cparams(("parallel", "arbitrary"), VMEM_LIMIT),
        name="fourier_pos",
    )(wp, z, w_out, x, gate)


def _moe_norm_kernel(x_ref, g_ref, sh_ref, sc_ref, wr_ref, hx_ref, lg_ref):
    h = _rms(x_ref[0]) * g_ref[...]
    h = h * (1.0 + sc_ref[0]) + sh_ref[0]
    hx_ref[0] = h.astype(BF16)
    lg_ref[0] = lax.dot_general(wr_ref[...], h, (((1,), (1,)), ((), ())),
                                precision=lax.Precision.HIGHEST, preferred_element_type=F32)


def _excl_cumsum_lanes(mask, tri):
    e, n = mask.shape
    mf = jnp.where(mask, 1.0, 0.0)
    offset = jnp.zeros((e, 1), F32)
    parts = []
    for blk in range(n // LANES):
        part = mf[:, blk * LANES:(blk + 1) * LANES]
        parts.append(jnp.dot(part.astype(BF16), tri, preferred_element_type=F32) + offset)
        offset = offset + jnp.sum(part, axis=1, keepdims=True)
    return jnp.concatenate(parts, axis=1)


def _moe_route_kernel(lg_ref, tri_ref, aff_ref, slot_ref, *, cap):
    lg = lg_ref[0]
    e = jnp.exp(lg - jnp.max(lg, axis=0, keepdims=True))
    aff = e / jnp.sum(e, axis=0, keepdims=True)
    aff_ref[0] = aff
    bits = lax.bitcast_convert_type(aff, jnp.int32)
    n_exp = lg.shape[0]

    def step(it, thr):
        cand = thr | jnp.left_shift(jnp.int32(1), 30 - it)
        cnt = jnp.sum((bits >= cand).astype(jnp.int32), axis=1, keepdims=True)
        return jnp.where(cnt >= cap, cand, thr)

    thr = lax.fori_loop(0, 31, step, jnp.zeros((n_exp, 1), jnp.int32))
    gt = bits > thr
    eq = bits == thr
    need = (cap - jnp.sum(gt.astype(jnp.int32), axis=1, keepdims=True)).astype(F32)
    tri = tri_ref[...]
    sel = gt | (eq & (_excl_cumsum_lanes(eq, tri) < need))
    pos = _excl_cumsum_lanes(sel, tri).astype(jnp.int32)
    slot_ref[0] = jnp.where(sel, pos, -1)


def _moe_gather_kernel(slot_ref, aff_ref, hx_ref, xs_ref, ta_ref, *, cap):
    e = pl.program_id(1)
    slot = slot_ref[0, pl.ds(e, 1), :]
    aff = aff_ref[0, pl.ds(e, 1), :]
    n = slot.shape[1]
    hit = lax.broadcasted_iota(jnp.int32, (cap, n), 0) == slot
    onehot = jnp.where(hit, 1.0, 0.0).astype(BF16)
    xs_ref[0, 0] = jnp.dot(onehot, hx_ref[0], preferred_element_type=F32).astype(BF16)
    ta = jnp.sum(jnp.where(hit, aff, 0.0), axis=1, keepdims=True)
    ta_ref[0, 0] = jnp.broadcast_to(ta, (cap, LANES))


def _moe_ffn_kernel(xs_ref, ta_ref, wg_ref, wu_ref, wd_ref, ys_ref, wgb, wub, wdb):
    @pl.when(pl.program_id(1) == 0)
    def _():
        wgb[...] = wg_ref[0].astype(BF16)
        wub[...] = wu_ref[0].astype(BF16)
        wdb[...] = wd_ref[0].astype(BF16)

    bb, _, cap, d = xs_ref.shape
    xs = xs_ref[...].reshape(bb * cap, d)
    ta = ta_ref[...].reshape(bb * cap, LANES)[:, :1]
    gate = jnp.dot(xs, wgb[...], preferred_element_type=F32)
    up = jnp.dot(xs, wub[...], preferred_element_type=F32)
    hid = (gate * jax.nn.sigmoid(gate) * up).astype(BF16)
    y = jnp.dot(hid, wdb[...], preferred_element_type=F32) * ta
    ys_ref[...] = y.astype(BF16).reshape(bb, 1, cap, d)


def _moe_scatter_kernel(slot_ref, ys_ref, x_ref, gate_ref, *rest, cap, final):
    if final:
        fg_ref, o_ref = rest
    else:
        (o_ref,) = rest
    slot = slot_ref[0].astype(F32).astype(BF16)
    n_exp = slot.shape[1]
    col = lax.broadcasted_iota(jnp.int32, (n_exp, n_exp * cap), 1)
    row = lax.broadcasted_iota(jnp.int32, (n_exp, n_exp * cap), 0)
    spread = jnp.where(col // cap == row, 1.0, 0.0).astype(BF16)
    want = (lax.broadcasted_iota(jnp.int32, (1, n_exp * cap), 1) % cap).astype(F32)
    hit = jnp.where(jnp.dot(slot, spread, preferred_element_type=F32) == want, 1.0, 0.0).astype(BF16)
    ys = ys_ref[0].reshape(n_exp * cap, ys_ref.shape[-1])
    out = x_ref[0] + gate_ref[0] * jnp.dot(hit, ys, preferred_element_type=F32)
    if final:
        out = _rms(out) * fg_ref[...]
    o_ref[0] = out


def moe_layer(x, norm, w_router_t, w_gate, w_up, w_down, gate, final_g=None):
    b, n, d = x.shape
    n_exp = w_router_t.shape[0]
    cap = EC_CAPACITY_FACTOR * n // n_exp
    g, sh, sc = norm
    tm = min(512, n)
    per_b2 = pl.BlockSpec((1, 1, d), lambda bi, i: (bi, 0, 0))
    hx, logits = pl.pallas_call(
        _moe_norm_kernel,
        grid=(b, n // tm),
        in_specs=[pl.BlockSpec((1, tm, d), lambda bi, i: (bi, i, 0)),
                  pl.BlockSpec((1, d), lambda bi, i: (0, 0)), per_b2, per_b2,
                  pl.BlockSpec((n_exp, d), lambda bi, i: (0, 0))],
        out_specs=[pl.BlockSpec((1, tm, d), lambda bi, i: (bi, i, 0)),
                   pl.BlockSpec((1, n_exp, tm), lambda bi, i: (bi, 0, i))],
        out_shape=[jax.ShapeDtypeStruct((b, n, d), BF16), jax.ShapeDtypeStruct((b, n_exp, n), F32)],
        compiler_params=_cparams(("parallel", "parallel")),
        name="moe_norm_router",
    )(x, g, sh, sc, w_router_t)

    tri = (np.arange(LANES)[:, None] < np.arange(LANES)[None, :]).astype(np.float32)
    aff, slot = pl.pallas_call(
        functools.partial(_moe_route_kernel, cap=cap),
        grid=(b,),
        in_specs=[pl.BlockSpec((1, n_exp, n), lambda bi: (bi, 0, 0)),
                  pl.BlockSpec((LANES, LANES), lambda bi: (0, 0))],
        out_specs=[pl.BlockSpec((1, n_exp, n), lambda bi: (bi, 0, 0))] * 2,
        out_shape=[jax.ShapeDtypeStruct((b, n_exp, n), F32), jax.ShapeDtypeStruct((b, n_exp, n), jnp.int32)],
        compiler_params=_cparams(("parallel",)),
        name="moe_route",
    )(logits, jnp.asarray(tri, BF16))

    xs, ta = pl.pallas_call(
        functools.partial(_moe_gather_kernel, cap=cap),
        grid=(b, n_exp),
        in_specs=[pl.BlockSpec((1, n_exp, n), lambda bi, e: (bi, 0, 0)),
                  pl.BlockSpec((1, n_exp, n), lambda bi, e: (bi, 0, 0)),
                  pl.BlockSpec((1, n, d), lambda bi, e: (bi, 0, 0))],
        out_specs=[pl.BlockSpec((1, 1, cap, d), lambda bi, e: (bi, e, 0, 0)),
                   pl.BlockSpec((1, 1, cap, LANES), lambda bi, e: (bi, e, 0, 0))],
        out_shape=[jax.ShapeDtypeStruct((b, n_exp, cap, d), BF16),
                   jax.ShapeDtypeStruct((b, n_exp, cap, LANES), F32)],
        compiler_params=_cparams(("parallel", "arbitrary")),
        name="moe_gather",
    )(slot, aff, hx)

    bb = 2
    f = w_gate.shape[-1]
    ys = pl.pallas_call(
        _moe_ffn_kernel,
        grid=(n_exp, b // bb),
        in_specs=[pl.BlockSpec((bb, 1, cap, d), lambda e, bi: (bi, e, 0, 0)),
                  pl.BlockSpec((bb, 1, cap, LANES), lambda e, bi: (bi, e, 0, 0)),
                  pl.BlockSpec((1, d, f), lambda e, bi: (e, 0, 0)),
                  pl.BlockSpec((1, d, f), lambda e, bi: (e, 0, 0)),
                  pl.BlockSpec((1, f, d), lambda e, bi: (e, 0, 0))],
        out_specs=pl.BlockSpec((bb, 1, cap, d), lambda e, bi: (bi, e, 0, 0)),
        out_shape=jax.ShapeDtypeStruct((b, n_exp, cap, d), BF16),
        scratch_shapes=[pltpu.VMEM((d, f), BF16), pltpu.VMEM((d, f), BF16), pltpu.VMEM((f, d), BF16)],
        compiler_params=_cparams(("parallel", "arbitrary"), VMEM_LIMIT),
        name="moe_ffn",
    )(xs, ta, w_gate, w_up, w_down)

    slot_t = jnp.swapaxes(slot, 1, 2)
    tn = min(512, n)
    args = [slot_t, ys, x, gate]
    specs = [pl.BlockSpec((1, tn, n_exp), lambda bi, i: (bi, i, 0)),
             pl.BlockSpec((1, n_exp, cap, d), lambda bi, i: (bi, 0, 0, 0)),
             pl.BlockSpec((1, tn, d), lambda bi, i: (bi, i, 0)), per_b2]
    if final_g is not None:
        args.append(final_g)
        specs.append(pl.BlockSpec((1, d), lambda bi, i: (0, 0)))
    return pl.pallas_call(
        functools.partial(_moe_scatter_kernel, cap=cap, final=final_g is not None),
        grid=(b, n // tn),
        in_specs=specs,
        out_specs=pl.BlockSpec((1, tn, d), lambda bi, i: (bi, i, 0)),
        out_shape=jax.ShapeDtypeStruct((b, n, d), F32),
        compiler_params=_cparams(("parallel", "arbitrary"), VMEM_LIMIT),
        name="moe_scatter",
    )(*args)


def kernel(x, c, ctx, c_ctx, ada_w, ada_b, mixer_norm_g, moe_norm_g, router_w, moe_w_gate, moe_w_up, moe_w_down, da_w_in, da_lambda_q1, da_lambda_k1, da_lambda_q2, da_lambda_k2, da_subln_g, da_w_out, gm_w_in, gm_v_g, gm_w_s, gm_b_s, gm_w_out, na_w_in, na_rpb, na_w_out, fn_w_out, final_norm_g):
    b, n, d = x.shape
    lc = ctx.shape[1]
    rows = n // GRID_W
    readers = [i for i in range(DEPTH) if i % N_MIXERS in CTX_READERS]
    last_reader = max(readers) if readers else -1

    r_pad = -(-(b + 1) // 8) * 8
    cond = jnp.zeros((r_pad, d), F32).at[:b].set(c).at[b].set(c_ctx)
    mods = ada_params(cond, ada_w, ada_b)

    def lat_mod(i, k):
        return mods[i, :b, k * d:(k + 1) * d].reshape(b, 1, d)

    def ctx_mod(i, k):
        return jnp.broadcast_to(mods[i, b, k * d:(k + 1) * d].reshape(1, 1, d), (b, 1, d))

    rope = _rope_tables(n)
    perm = _da_perm()

    for i in range(DEPTH):
        kind, j = i % N_MIXERS, i // N_MIXERS
        need_ctx = i <= last_reader
        update_ctx = i < last_reader
        mg = mixer_norm_g[i].reshape(1, d)
        xnorm = (mg, lat_mod(i, 0), lat_mod(i, 1))
        cnorm = (mg, ctx_mod(i, 0), ctx_mod(i, 1)) if need_ctx else None
        g1 = lat_mod(i, 2)

        if kind == MIX_DIFF:
            lam_init = 0.8 - 0.6 * math.exp(-0.3 * i)
            cols = np.arange(2 * DA_WIDTH).reshape(2 * DA_HEADS, LANES)[:, perm].reshape(-1)
            w_in = jnp.concatenate([da_w_in[j][:, cols], da_w_in[j][:, 2 * DA_WIDTH:]], axis=1).astype(BF16)
            w_out = da_w_out[j].astype(BF16)
            lams = [t[j].reshape(1, DA_HEAD_DIM) for t in (da_lambda_q1, da_lambda_k1, da_lambda_q2, da_lambda_k2)]
            sg = da_subln_g[j].reshape(1, DA_V_DIM)
            nb = DA_WIDTH // LANES
            qkv = linear(x, w_in, norm=xnorm, rope=rope, rope_tiles=2)
            if update_ctx:
                qkv_c = linear(ctx, w_in, norm=cnorm)
                kv_c = qkv_c[:, :, DA_WIDTH:]
            else:
                kv_c = linear(ctx, w_in[:, DA_WIDTH:], norm=cnorm)
            kv_all = jnp.concatenate([kv_c, qkv[:, :, DA_WIDTH:]], axis=1)
            o = diff_attention(qkv, 0, kv_all, 0, nb, lams, sg, lam_init)
            x = linear(o, w_out, res=(x, g1), out_dtype=F32)
            if update_ctx:
                oc = diff_attention(qkv_c, 0, qkv_c, nb, 2 * nb, lams, sg, lam_init)
                ctx_mix = (oc, w_out)
        elif kind == MIX_GMLP:
            w_in = gm_w_in[j].astype(BF16)
            w_out = gm_w_out[j].astype(BF16)
            vg = gm_v_g[j].reshape(1, GM_WIDTH)
            w_s = gm_w_s[j].astype(BF16)
            b_s_t = jnp.transpose(gm_b_s[j])
            x = gmlp_layer(x, xnorm, w_in, vg, w_s, b_s_t, w_out, g1)
            if update_ctx:
                ctx = gmlp_layer(ctx, cnorm, w_in, vg, w_s, b_s_t, w_out, ctx_mod(i, 2))
                ctx_mix = None
        elif kind == MIX_NATTEN:
            w_in = na_w_in[j].astype(BF16)
            w_out = na_w_out[j].astype(BF16)
            qkv = linear(x, w_in, norm=xnorm)
            kv_c = linear(ctx, w_in[:, NA_WIDTH:], norm=cnorm)
            o = natten(qkv, kv_c, _natten_bias(na_rpb[j], rows))
            x = linear(o, w_out, res=(x, g1), out_dtype=F32)
            if update_ctx:
                raise NotImplementedError("context update after a neighbourhood layer")
        else:
            x = fourier_layer(x, xnorm, fn_w_out[j].astype(BF16), g1)
            if update_ctx:
                ctx = fourier_layer(ctx, cnorm, fn_w_out[j].astype(BF16), ctx_mod(i, 2))
                ctx_mix = None

        wr_t = jnp.transpose(router_w[i])
        moe_g = moe_norm_g[i].reshape(1, d)
        last = i == DEPTH - 1
        x = moe_layer(x, (moe_g, lat_mod(i, 3), lat_mod(i, 4)), wr_t, moe_w_gate[i], moe_w_up[i],
                      moe_w_down[i], lat_mod(i, 5), final_g=final_norm_g.reshape(1, d) if last else None)
        if update_ctx:
            if ctx_mix is not None:
                oc, w_out = ctx_mix
                ctx = linear(oc, w_out, res=(ctx, ctx_mod(i, 2)), out_dtype=F32)
            ctx = moe_layer(ctx, (moe_g, ctx_mod(i, 3), ctx_mod(i, 4)), wr_t, moe_w_gate[i], moe_w_up[i],
                            moe_w_down[i], ctx_mod(i, 5))
    return x
```

```python
import functools
import math

import numpy as np
import jax
import jax.numpy as jnp
from jax import lax
from jax.experimental import pallas as pl
from jax.experimental.pallas import tpu as pltpu

F32 = jnp.float32
BF16 = jnp.bfloat16

D_MODEL = 1024
DEPTH = 4
GRID_W = 64
N_MIXERS = 4
MIX_DIFF, MIX_GMLP, MIX_NATTEN, MIX_FOURIER = 0, 1, 2, 3
CTX_READERS = (MIX_DIFF, MIX_NATTEN)
EPS = 1e-6
NEG_INF = -1e30
ROPE_BASE = 10000.0

DA_HEADS = 8
DA_HEAD_DIM = 64
DA_V_DIM = 2 * DA_HEAD_DIM
DA_WIDTH = DA_HEADS * DA_V_DIM

GM_CHUNK = 128
GM_GROUPS = 8
GM_WIDTH = 2 * D_MODEL
GM_GROUP_DIM = GM_WIDTH // GM_GROUPS

NA_HEADS = 16
NA_HEAD_DIM = D_MODEL // NA_HEADS
NA_WIDTH = NA_HEADS * NA_HEAD_DIM
NA_WIN_R = 8
NA_WIN_C = 16
NA_ROW_BLOCK = 4
NA_SLAB_ROWS = NA_ROW_BLOCK + NA_WIN_R

FN_GROUPS = 4

N_EXPERTS = 16
EC_CAPACITY_FACTOR = 2

LANES = 128
VMEM_LIMIT = 56 * 1024 * 1024


def _cparams(sem, vmem=None):
    return pltpu.CompilerParams(dimension_semantics=sem, vmem_limit_bytes=vmem)


def _rms(x):
    return x * lax.rsqrt(jnp.mean(x * x, axis=-1, keepdims=True) + EPS)


def _ada_kernel(c_ref, w_ref, b_ref, o_ref):
    c = c_ref[...]
    h = (c * jax.nn.sigmoid(c)).astype(BF16)
    o_ref[0] = jnp.dot(h, w_ref[0].astype(BF16), preferred_element_type=F32) + b_ref[0]


def ada_params(cond, ada_w, ada_b):
    r = cond.shape[0]
    depth, d, n_out = ada_w.shape
    tn = 1024
    return pl.pallas_call(
        _ada_kernel,
        grid=(depth, n_out // tn),
        in_specs=[pl.BlockSpec((r, d), lambda l, j: (0, 0)),
                  pl.BlockSpec((1, d, tn), lambda l, j: (l, 0, j)),
                  pl.BlockSpec((1, 1, tn), lambda l, j: (l, 0, j))],
        out_specs=pl.BlockSpec((1, r, tn), lambda l, j: (l, 0, j)),
        out_shape=jax.ShapeDtypeStruct((depth, r, n_out), F32),
        compiler_params=_cparams(("parallel", "parallel")),
        name="ada_params",
    )(cond, ada_w, ada_b.reshape(depth, 1, n_out))


def _linear_kernel(*refs, has_norm, has_res, rope_tiles, tn):
    it = iter(refs)
    x_ref = next(it)
    if has_norm:
        g_ref, sh_ref, sc_ref = next(it), next(it), next(it)
    w_ref = next(it)
    if rope_tiles:
        cos_ref, sin_ref = next(it), next(it)
    if has_res:
        res_ref, gate_ref = next(it), next(it)
    o_ref = next(it)

    if has_norm:
        h = _rms(x_ref[0]) * g_ref[...]
        hb = (h * (1.0 + sc_ref[0]) + sh_ref[0]).astype(BF16)
    else:
        hb = x_ref[0].astype(BF16)
    for j in range(w_ref.shape[1] // tn):
        cols = slice(j * tn, (j + 1) * tn)
        y = jnp.dot(hb, w_ref[:, cols], preferred_element_type=F32)
        if j < rope_tiles:
            cos, sin = cos_ref[...], sin_ref[...]
            for s in range(tn // LANES):
                seg = y[:, s * LANES:(s + 1) * LANES]
                rot = pltpu.roll(seg, LANES // 2, axis=1)
                lanes = slice(j * tn + s * LANES, j * tn + (s + 1) * LANES)
                o_ref[0, :, lanes] = (seg * cos + rot * sin).astype(o_ref.dtype)
        elif has_res:
            o_ref[0, :, cols] = (res_ref[0, :, cols] + gate_ref[0, :, cols] * y).astype(o_ref.dtype)
        else:
            o_ref[0, :, cols] = y.astype(o_ref.dtype)


def linear(x, w, *, norm=None, res=None, rope=None, rope_tiles=0, out_dtype=BF16, tm=512, tn=1024):
    b, n, k = x.shape
    m = w.shape[1]
    tm = min(tm, n)
    tn = min(tn, m)
    args, specs = [x], [pl.BlockSpec((1, tm, k), lambda bi, i: (bi, i, 0))]
    if norm is not None:
        g, sh, sc = norm
        args += [g, sh, sc]
        specs += [pl.BlockSpec((1, k), lambda bi, i: (0, 0)),
                  pl.BlockSpec((1, 1, k), lambda bi, i: (bi, 0, 0)),
                  pl.BlockSpec((1, 1, k), lambda bi, i: (bi, 0, 0))]
    args.append(w)
    specs.append(pl.BlockSpec((k, m), lambda bi, i: (0, 0)))
    if rope is not None:
        args += list(rope)
        specs += [pl.BlockSpec((tm, LANES), lambda bi, i: (i, 0))] * 2
    if res is not None:
        r, gate = res
        args += [r, gate]
        specs += [pl.BlockSpec((1, tm, m), lambda bi, i: (bi, i, 0)),
                  pl.BlockSpec((1, 1, m), lambda bi, i: (bi, 0, 0))]
    kern = functools.partial(_linear_kernel, has_norm=norm is not None, has_res=res is not None,
                             rope_tiles=rope_tiles if rope is not None else 0, tn=tn)
    return pl.pallas_call(
        kern,
        grid=(b, n // tm),
        in_specs=specs,
        out_specs=pl.BlockSpec((1, tm, m), lambda bi, i: (bi, i, 0)),
        out_shape=jax.ShapeDtypeStruct((b, n, m), out_dtype),
        compiler_params=_cparams(("parallel", "parallel"), VMEM_LIMIT),
        name="linear",
    )(*args)


DA_VT_ROWS = DA_V_DIM + 16
ATT_TQ = 256
ATT_KC = 768


def _interleave(stage_a, stage_b):
    for i in range(max(len(stage_a), len(stage_b))):
        if i < len(stage_a):
            stage_a[i]()
        if i < len(stage_b):
            stage_b[i]()


def _diff_attn_kernel(*refs, n_src, lam_init):
    q_ref = refs[0]
    k_refs = refs[1:1 + n_src]
    v_refs = refs[1 + n_src:1 + 2 * n_src]
    lq1_ref, lk1_ref, lq2_ref, lk2_ref, g_ref, o_ref, k_all, vt_all, s_scr, e_scr = refs[1 + 2 * n_src:]

    row = 0
    for k_ref, v_ref in zip(k_refs, v_refs):
        n = k_ref.shape[1]
        k_all[row:row + n, :] = k_ref[0]
        vt_all[:DA_V_DIM, row:row + n] = jnp.transpose(v_ref[0].astype(F32)).astype(BF16)
        row += n
    vt_all[DA_V_DIM:, :] = jnp.ones((DA_VT_ROWS - DA_V_DIM, vt_all.shape[1]), BF16)

    lam = (jnp.exp(jnp.sum(lq1_ref[...] * lk1_ref[...], keepdims=True))
           - jnp.exp(jnp.sum(lq2_ref[...] * lk2_ref[...], keepdims=True)) + lam_init)
    lane = lax.broadcasted_iota(jnp.int32, (1, LANES), 1)
    first = (lane % DA_HEAD_DIM) < (DA_HEAD_DIM // 2)
    dims = (((1,), (1,)), ((), ()))
    c = DA_HEAD_DIM ** -0.5 * math.log2(math.e)
    nk = k_all.shape[0]
    tq = min(ATT_TQ, q_ref.shape[1])
    n_tiles = q_ref.shape[1] // tq
    chunks = [slice(r, min(r + ATT_KC, nk)) for r in range(0, nk, ATT_KC)]
    state = [dict(m=None) for _ in range(n_tiles)]

    def scores(t):
        q = q_ref[0, t * tq:(t + 1) * tq, :]
        zero = jnp.zeros_like(q)
        q2 = jnp.concatenate([jnp.where(first, q, zero), jnp.where(first, zero, q)], axis=0)

        def chunk(rows):
            s = lax.dot_general(k_all[rows, :], q2, dims, preferred_element_type=F32) * c
            s_scr[t % 2, rows, :] = s
            m = jnp.max(s, axis=0, keepdims=True)
            state[t]["m"] = m if state[t]["m"] is None else jnp.maximum(state[t]["m"], m)

        return [functools.partial(chunk, rows) for rows in chunks]

    def exps(t):
        def chunk(rows):
            e_scr[t % 2, rows, :] = jnp.exp2(s_scr[t % 2, rows, :] - state[t]["m"]).astype(BF16)

        return [functools.partial(chunk, rows) for rows in chunks]

    def values(t):
        ot = jnp.dot(vt_all[...], e_scr[t % 2], preferred_element_type=F32)
        ot = ot[:DA_V_DIM] / ot[DA_V_DIM:DA_V_DIM + 1]
        ot = ot[:, :tq] - lam * ot[:, tq:]
        ot = ot * lax.rsqrt(jnp.mean(ot * ot, axis=0, keepdims=True) + EPS) * g_ref[...] * (1.0 - lam_init)
        o_ref[0, t * tq:(t + 1) * tq, :] = jnp.transpose(ot).astype(o_ref.dtype)

    for t in range(n_tiles + 2):
        _interleave(scores(t) if t < n_tiles else [],
                    (exps(t - 1) if 0 <= t - 1 < n_tiles else []))
        if 0 <= t - 2 < n_tiles:
            values(t - 2)


def diff_attention(q_arr, srcs, lams, subln_g, lam_init):
    b, nq, _ = q_arr.shape
    nk = sum(a.shape[1] for a, _, _ in srcs)
    tq = min(ATT_TQ, nq)
    vec = pl.BlockSpec((1, DA_HEAD_DIM), lambda bi, h: (0, 0))
    k_specs = [pl.BlockSpec((1, a.shape[1], LANES), lambda bi, h, c0=kc: (bi, 0, c0 + h)) for a, kc, _ in srcs]
    v_specs = [pl.BlockSpec((1, a.shape[1], LANES), lambda bi, h, c0=vc: (bi, 0, c0 + h)) for a, _, vc in srcs]
    arrs = [a for a, _, _ in srcs]
    return pl.pallas_call(
        functools.partial(_diff_attn_kernel, n_src=len(srcs), lam_init=lam_init),
        grid=(b, DA_HEADS),
        in_specs=[pl.BlockSpec((1, nq, LANES), lambda bi, h: (bi, 0, h))] + k_specs + v_specs
                 + [vec, vec, vec, vec, pl.BlockSpec((DA_V_DIM, 1), lambda bi, h: (0, 0))],
        out_specs=pl.BlockSpec((1, nq, LANES), lambda bi, h: (bi, 0, h)),
        out_shape=jax.ShapeDtypeStruct((b, nq, DA_WIDTH), BF16),
        scratch_shapes=[pltpu.VMEM((nk, LANES), BF16), pltpu.VMEM((DA_VT_ROWS, nk), BF16),
                        pltpu.VMEM((2, nk, 2 * tq), F32), pltpu.VMEM((2, nk, 2 * tq), BF16)],
        compiler_params=_cparams(("parallel", "parallel"), VMEM_LIMIT),
        name="diff_attention",
    )(q_arr, *arrs, *arrs, *lams, subln_g)


def _da_perm():
    perm = np.zeros(LANES, np.int32)
    for l in range(LANES):
        part, within = divmod(l, 64)
        j, rem = divmod(within, 32)
        seg, i = divmod(rem, 16)
        perm[l] = j * 64 + seg * 32 + part * 16 + i
    return perm


def _rope_tables(n_tok):
    t = jnp.arange(n_tok, dtype=jnp.int32)
    n_freq = DA_HEAD_DIM // 4
    inv = ROPE_BASE ** (-jnp.arange(n_freq, dtype=F32) / n_freq)
    ang_r = (t // GRID_W).astype(F32)[:, None] * inv
    ang_c = (t % GRID_W).astype(F32)[:, None] * inv
    cos32 = jnp.concatenate([jnp.cos(ang_r), jnp.cos(ang_c)], axis=1)
    sin32 = jnp.concatenate([jnp.sin(ang_r), jnp.sin(ang_c)], axis=1)
    cos = jnp.tile(cos32, (1, 4))
    sin = jnp.concatenate([-sin32, -sin32, sin32, sin32], axis=1)
    return cos, sin


def _gmlp_kernel(x_ref, g_ref, sh_ref, sc_ref, win_ref, vg_ref, ws_ref, bs_ref, wout_ref, gate_ref, o_ref, *, tm):
    x = x_ref[0]
    h = _rms(x) * g_ref[...]
    h = (h * (1.0 + sc_ref[0]) + sh_ref[0]).astype(BF16)
    uv = jax.nn.gelu(jnp.dot(h, win_ref[...], preferred_element_type=F32), approximate=True)
    u = uv[:, :GM_WIDTH]
    v = (_rms(uv[:, GM_WIDTH:]) * vg_ref[...]).astype(BF16)
    rows = []
    for c in range(tm // GM_CHUNK):
        cols = []
        for gi in range(GM_GROUPS):
            vv = v[c * GM_CHUNK:(c + 1) * GM_CHUNK, gi * GM_GROUP_DIM:(gi + 1) * GM_GROUP_DIM]
            sv = jnp.dot(ws_ref[gi], vv, preferred_element_type=F32) + bs_ref[:, gi:gi + 1]
            cols.append(sv)
        rows.append(jnp.concatenate(cols, axis=1))
    sv = jnp.concatenate(rows, axis=0) if len(rows) > 1 else rows[0]
    y = jnp.dot((u * sv).astype(BF16), wout_ref[...], preferred_element_type=F32)
    o_ref[0] = x + gate_ref[0] * y


def gmlp_layer(x, norm, w_in, v_g, w_s, b_s_t, w_out, gate, tm=256):
    b, n, d = x.shape
    tm = min(tm, n)
    g, sh, sc = norm
    full = lambda shape: pl.BlockSpec(shape, lambda bi, i: (0,) * len(shape))
    per_b = pl.BlockSpec((1, 1, d), lambda bi, i: (bi, 0, 0))
    return pl.pallas_call(
        functools.partial(_gmlp_kernel, tm=tm),
        grid=(b, n // tm),
        in_specs=[pl.BlockSpec((1, tm, d), lambda bi, i: (bi, i, 0)),
                  full((1, d)), per_b, per_b,
                  full(w_in.shape), full((1, GM_WIDTH)), full(w_s.shape), full(b_s_t.shape),
                  full(w_out.shape), per_b],
        out_specs=pl.BlockSpec((1, tm, d), lambda bi, i: (bi, i, 0)),
        out_shape=jax.ShapeDtypeStruct((b, n, d), F32),
        compiler_params=_cparams(("parallel", "arbitrary"), VMEM_LIMIT),
        name="gmlp_layer",
    )(x, g, sh, sc, w_in, v_g, w_s, b_s_t, w_out, gate)


NA_VT_ROWS = LANES + 16


def _natten_kernel(q_ref, k_ref, v_ref, kc_ref, vc_ref, bias_ref, o_ref, vt_all, vct, s_scr, e_scr, *, n_blocks):
    tq = NA_ROW_BLOCK * GRID_W
    slab_blocks = NA_SLAB_ROWS // NA_ROW_BLOCK
    slab = slab_blocks * tq
    lc = kc_ref.shape[1]
    for blk in range(n_blocks):
        vt_all[:LANES, blk * tq:(blk + 1) * tq] = jnp.transpose(
            v_ref[0, blk * tq:(blk + 1) * tq, :].astype(F32)).astype(BF16)
    vt_all[LANES:, :] = jnp.ones((NA_VT_ROWS - LANES, vt_all.shape[1]), BF16)
    vct[:LANES, :] = jnp.transpose(vc_ref[0].astype(F32)).astype(BF16)
    vct[LANES:, :] = jnp.ones((NA_VT_ROWS - LANES, lc), BF16)

    lane = lax.broadcasted_iota(jnp.int32, (1, LANES), 1)
    left = lane < NA_HEAD_DIM
    dims = (((1,), (1,)), ((), ()))
    c = NA_HEAD_DIM ** -0.5 * math.log2(math.e)
    state = [dict(m=None) for _ in range(n_blocks)]
    key0 = [min(max(t - 1, 0), n_blocks - slab_blocks) * tq for t in range(n_blocks)]
    kind = [0 if t == 0 else 2 if t == n_blocks - 1 else 1 for t in range(n_blocks)]

    def scores(t):
        q = q_ref[0, t * tq:(t + 1) * tq, :]
        zero = jnp.zeros_like(q)
        q2 = jnp.concatenate([jnp.where(left, q, zero), jnp.where(left, zero, q)], axis=0)

        def local():
            k = k_ref[0, key0[t]:key0[t] + slab, :]
            s = lax.dot_general(k, q2, dims, preferred_element_type=F32) * c + bias_ref[0, kind[t]]
            s_scr[t % 2, :slab, :] = s
            state[t]["m"] = jnp.max(s, axis=0, keepdims=True)

        def context():
            s = lax.dot_general(kc_ref[0], q2, dims, preferred_element_type=F32) * c
            s_scr[t % 2, slab:, :] = s
            state[t]["m"] = jnp.maximum(state[t]["m"], jnp.max(s, axis=0, keepdims=True))

        return [local, context]

    def exps(t):
        def chunk(rows):
            e_scr[t % 2, rows, :] = jnp.exp2(s_scr[t % 2, rows, :] - state[t]["m"]).astype(BF16)

        return [functools.partial(chunk, slice(0, slab)), functools.partial(chunk, slice(slab, slab + lc))]

    def values(t):
        vt = jnp.concatenate([vt_all[:, key0[t]:key0[t] + slab], vct[...]], axis=1)
        ot = jnp.dot(vt, e_scr[t % 2], preferred_element_type=F32)
        ot = ot[:LANES] / ot[LANES:LANES + 1]
        pair = jnp.concatenate([ot[:NA_HEAD_DIM, :tq], ot[NA_HEAD_DIM:, tq:]], axis=0)
        o_ref[0, t * tq:(t + 1) * tq, :] = jnp.transpose(pair).astype(o_ref.dtype)

    for t in range(n_blocks + 2):
        _interleave(scores(t) if t < n_blocks else [],
                    (exps(t - 1) if 0 <= t - 1 < n_blocks else []))
        if 0 <= t - 2 < n_blocks:
            values(t - 2)


def _natten_bias(rpb, rows):
    n_heads = rpb.shape[0]
    n_blocks = rows // NA_ROW_BLOCK
    n_dr, n_dc = 2 * NA_WIN_R - 1, 2 * NA_WIN_C - 1
    span = 2 * GRID_W - 1
    lo = GRID_W - NA_WIN_C
    u = jnp.pad(rpb, ((0, 0), (0, 0), (lo, span - n_dc - lo)))
    skew = jnp.tile(u, (1, 1, GRID_W + 1))[:, :, :GRID_W * (span + 1)].reshape(n_heads, n_dr, GRID_W, span + 1)
    toep = skew[:, :, ::-1, :GRID_W]
    pad = NA_ROW_BLOCK
    toep = jnp.pad(toep, ((0, 0), (pad, pad), (0, 0), (0, 0)))
    i = np.arange(NA_ROW_BLOCK)[:, None, None, None]
    c = np.arange(GRID_W)[None, :, None, None]
    m = np.arange(NA_SLAB_ROWS)[None, None, :, None]
    kc = np.arange(GRID_W)[None, None, None, :]
    win_c0 = np.clip(c - NA_WIN_C // 2, 0, GRID_W - NA_WIN_C)
    col_ok = (kc >= win_c0) & (kc < win_c0 + NA_WIN_C)
    tiles, ok = [], []
    for rb in (0, 1, n_blocks - 1):
        slab0 = int(np.clip(rb - 1, 0, n_blocks - 3)) * NA_ROW_BLOCK
        r = rb * NA_ROW_BLOCK + i
        r0 = np.clip(r - NA_WIN_R // 2, 0, rows - NA_WIN_R)
        kr = slab0 + m
        ok.append(np.broadcast_to((kr >= r0) & (kr < r0 + NA_WIN_R) & col_ok,
                                  (NA_ROW_BLOCK, GRID_W, NA_SLAB_ROWS, GRID_W)))
        for ii in range(NA_ROW_BLOCK):
            off = slab0 - (rb * NA_ROW_BLOCK + ii) + NA_WIN_R - 1 + pad
            tiles.append(toep[:, off:off + NA_SLAB_ROWS])
    tq, slab = NA_ROW_BLOCK * GRID_W, NA_SLAB_ROWS * GRID_W
    bias = jnp.stack(tiles, axis=1).reshape(n_heads // 2, 2, 3, NA_ROW_BLOCK, NA_SLAB_ROWS, GRID_W, GRID_W)
    bias = jnp.transpose(bias, (0, 2, 4, 6, 1, 3, 5)).reshape(n_heads // 2, 3, slab, 2 * tq)
    ok = np.stack(ok).transpose(0, 3, 4, 1, 2).reshape(3, slab, tq)
    ok = np.concatenate([ok, ok], axis=2)
    return jnp.where(ok[None], bias * math.log2(math.e), NEG_INF)


def natten(qkv, kv_ctx, bias):
    b, n, _ = qkv.shape
    lc = kv_ctx.shape[1]
    tq = NA_ROW_BLOCK * GRID_W
    n_blocks = n // tq
    hp = NA_HEADS // 2
    return pl.pallas_call(
        functools.partial(_natten_kernel, n_blocks=n_blocks),
        grid=(hp, b),
        in_specs=[pl.BlockSpec((1, n, LANES), lambda p, bi: (bi, 0, p)),
                  pl.BlockSpec((1, n, LANES), lambda p, bi: (bi, 0, hp + p)),
                  pl.BlockSpec((1, n, LANES), lambda p, bi: (bi, 0, 2 * hp + p)),
                  pl.BlockSpec((1, lc, LANES), lambda p, bi: (bi, 0, p)),
                  pl.BlockSpec((1, lc, LANES), lambda p, bi: (bi, 0, hp + p)),
                  pl.BlockSpec((1, 3, NA_SLAB_ROWS * GRID_W, 2 * tq), lambda p, bi: (p, 0, 0, 0))],
        out_specs=pl.BlockSpec((1, n, LANES), lambda p, bi: (bi, 0, p)),
        out_shape=jax.ShapeDtypeStruct((b, n, NA_WIDTH), BF16),
        scratch_shapes=[pltpu.VMEM((NA_VT_ROWS, n), BF16), pltpu.VMEM((NA_VT_ROWS, lc), BF16),
                        pltpu.VMEM((2, NA_SLAB_ROWS * GRID_W + lc, 2 * tq), F32),
                        pltpu.VMEM((2, NA_SLAB_ROWS * GRID_W + lc, 2 * tq), BF16)],
        compiler_params=_cparams(("parallel", "parallel"), VMEM_LIMIT),
        name="natten",
    )(qkv, qkv, qkv, kv_ctx, kv_ctx, bias)


def _dft_tables(n, sign=1.0):
    k = jnp.arange(n, dtype=jnp.int32)
    ang = ((k[:, None] * k[None, :]) % n).astype(F32) * (2.0 * math.pi / n)
    return jnp.cos(ang), sign * jnp.sin(ang)


def _fourier_chan_kernel(x_ref, g_ref, sh_ref, sc_ref, wc_ref, o_ref):
    h = _rms(x_ref[0]) * g_ref[...]
    h = (h * (1.0 + sc_ref[0]) + sh_ref[0]).astype(BF16)
    gd = D_MODEL // FN_GROUPS
    for gi in range(FN_GROUPS):
        z = jnp.dot(h[:, gi * gd:(gi + 1) * gd], wc_ref[...], preferred_element_type=F32).astype(BF16)
        o_ref[0, 0, :, gi * gd:(gi + 1) * gd] = z[:, :gd]
        o_ref[0, 1, :, gi * gd:(gi + 1) * gd] = z[:, gd:]


def _fourier_pos_kernel(wp_ref, z_ref, wout_ref, x_ref, gate_ref, o_ref, *, scale):
    f = jnp.dot(wp_ref[...], z_ref[0], preferred_element_type=F32) * scale
    y = jnp.dot(f.astype(BF16), wout_ref[...], preferred_element_type=F32)
    o_ref[0] = x_ref[0] + gate_ref[0] * y


def fourier_layer(x, norm, w_out, gate, tm=512):
    b, n, d = x.shape
    gd = d // FN_GROUPS
    g, sh, sc = norm
    cc, sc_tab = _dft_tables(gd)
    wc = jnp.concatenate([cc, sc_tab], axis=1).astype(BF16)
    cn, sn = _dft_tables(n, -1.0)
    wp = jnp.concatenate([cn, sn], axis=1).astype(BF16)
    per_b = pl.BlockSpec((1, 1, d), lambda bi, i: (bi, 0, 0))
    z = pl.pallas_call(
        _fourier_chan_kernel,
        grid=(b, n // tm),
        in_specs=[pl.BlockSpec((1, tm, d), lambda bi, i: (bi, i, 0)),
                  pl.BlockSpec((1, d), lambda bi, i: (0, 0)), per_b, per_b,
                  pl.BlockSpec((gd, 2 * gd), lambda bi, i: (0, 0))],
        out_specs=pl.BlockSpec((1, 2, tm, d), lambda bi, i: (bi, 0, i, 0)),
        out_shape=jax.ShapeDtypeStruct((b, 2, n, d), BF16),
        compiler_params=_cparams(("parallel", "parallel")),
        name="fourier_chan",
    )(x, g, sh, sc, wc)
    z = z.reshape(b, 2 * n, d)
    return pl.pallas_call(
        functools.partial(_fourier_pos_kernel, scale=1.0 / math.sqrt(n * gd)),
        grid=(b, n // tm),
        in_specs=[pl.BlockSpec((tm, 2 * n), lambda bi, i: (i, 0)),
                  pl.BlockSpec((1, 2 * n, d), lambda bi, i: (bi, 0, 0)),
                  pl.BlockSpec((d, d), lambda bi, i: (0, 0)),
                  pl.BlockSpec((1, tm, d), lambda bi, i: (bi, i, 0)), per_b],
        out_specs=pl.BlockSpec((1, tm, d), lambda bi, i: (bi, i, 0)),
        out_shape=jax.ShapeDtypeStruct((b, n, d), F32),
        compiler_params=_cparams(("parallel", "arbitrary"), VMEM_LIMIT),
        name="fourier_pos",
    )(wp, z, w_out, x, gate)


def _moe_norm_kernel(x_ref, g_ref, sh_ref, sc_ref, wr_ref, hx_ref, lg_ref):
    h = _rms(x_ref[0]) * g_ref[...]
    h = h * (1.0 + sc_ref[0]) + sh_ref[0]
    hx_ref[0] = h.astype(BF16)
    lg_ref[0] = lax.dot_general(wr_ref[...], h, (((1,), (1,)), ((), ())),
                                precision=lax.Precision.HIGHEST, preferred_element_type=F32)


def _excl_cumsum_lanes(mask, tri):
    e, n = mask.shape
    mf = jnp.where(mask, 1.0, 0.0)
    offset = jnp.zeros((e, 1), F32)
    parts = []
    for blk in range(n // LANES):
        part = mf[:, blk * LANES:(blk + 1) * LANES]
        parts.append(jnp.dot(part.astype(BF16), tri, preferred_element_type=F32) + offset)
        offset = offset + jnp.sum(part, axis=1, keepdims=True)
    return jnp.concatenate(parts, axis=1)


def _moe_route_kernel(lg_ref, tri_ref, aff_ref, slot_ref, *, cap):
    lg = lg_ref[0]
    e = jnp.exp(lg - jnp.max(lg, axis=0, keepdims=True))
    aff = e / jnp.sum(e, axis=0, keepdims=True)
    aff_ref[0] = aff
    n_exp = lg.shape[0]

    def as_float(bits):
        return lax.bitcast_convert_type(bits, F32)

    def step(it, thr):
        cand = thr | jnp.left_shift(jnp.int32(1), 30 - it)
        cnt = jnp.sum((aff >= as_float(cand)).astype(jnp.int32), axis=1, keepdims=True)
        return jnp.where(cnt >= cap, cand, thr)

    thr = lax.fori_loop(0, 31, step, jnp.zeros((n_exp, 1), jnp.int32))
    gt = aff >= as_float(thr + 1)
    eq = (aff >= as_float(thr)) & jnp.logical_not(gt)
    need = (cap - jnp.sum(gt.astype(jnp.int32), axis=1, keepdims=True)).astype(F32)
    tri = tri_ref[...]
    sel = gt | (eq & (_excl_cumsum_lanes(eq, tri) < need))
    pos = _excl_cumsum_lanes(sel, tri).astype(jnp.int32)
    slot_ref[0] = jnp.where(sel, pos, -1)


def _moe_gather_kernel(slot_ref, aff_ref, hx_ref, xs_ref, ta_ref, *, cap, ge):
    e0 = pl.multiple_of(pl.program_id(1) * ge, ge)
    n = slot_ref.shape[2]
    rows = lax.broadcasted_iota(jnp.int32, (cap, n), 0)
    hits, tas = [], []
    for k in range(ge):
        hit = rows == slot_ref[0, pl.ds(e0 + k, 1), :]
        hits.append(jnp.where(hit, 1.0, 0.0).astype(BF16))
        tas.append(jnp.sum(jnp.where(hit, aff_ref[0, pl.ds(e0 + k, 1), :], 0.0), axis=1, keepdims=True))
    onehot = jnp.concatenate(hits, axis=0) if ge > 1 else hits[0]
    xs = jnp.dot(onehot, hx_ref[0], preferred_element_type=F32).astype(BF16)
    xs_ref[0] = xs.reshape(ge, cap, xs.shape[-1])
    ta = jnp.concatenate(tas, axis=0) if ge > 1 else tas[0]
    ta_ref[0] = jnp.broadcast_to(ta, (ge * cap, LANES)).reshape(ge, cap, LANES)


def _moe_ffn_kernel(xs_ref, ta_ref, wg_ref, wu_ref, wd_ref, ys_ref, wgb, wub, wdb):
    @pl.when(pl.program_id(1) == 0)
    def _():
        wgb[...] = wg_ref[0, 0].astype(BF16)
        wub[...] = wu_ref[0, 0].astype(BF16)
        wdb[...] = wd_ref[0, 0].astype(BF16)

    bb, _, cap, d = xs_ref.shape
    xs = xs_ref[...].reshape(bb * cap, d)
    ta = ta_ref[...].reshape(bb * cap, LANES)[:, :1]
    gate = jnp.dot(xs, wgb[...], preferred_element_type=F32)
    up = jnp.dot(xs, wub[...], preferred_element_type=F32)
    hid = (gate * jax.nn.sigmoid(gate) * up).astype(BF16)
    y = jnp.dot(hid, wdb[...], preferred_element_type=F32) * ta
    ys_ref[...] = y.astype(BF16).reshape(bb, 1, cap, d)


def _moe_scatter_kernel(slot_ref, ys_ref, x_ref, gate_ref, *rest, cap, final):
    if final:
        fg_ref, o_ref = rest
    else:
        (o_ref,) = rest
    slot = slot_ref[0].astype(F32).astype(BF16)
    n_exp = slot.shape[1]
    col = lax.broadcasted_iota(jnp.int32, (n_exp, n_exp * cap), 1)
    row = lax.broadcasted_iota(jnp.int32, (n_exp, n_exp * cap), 0)
    spread = jnp.where(col // cap == row, 1.0, 0.0).astype(BF16)
    want = (lax.broadcasted_iota(jnp.int32, (1, n_exp * cap), 1) % cap).astype(F32)
    hit = jnp.where(jnp.dot(slot, spread, preferred_element_type=F32) == want, 1.0, 0.0).astype(BF16)
    ys = ys_ref[0].reshape(n_exp * cap, ys_ref.shape[-1])
    out = x_ref[0] + gate_ref[0] * jnp.dot(hit, ys, preferred_element_type=F32)
    if final:
        out = _rms(out) * fg_ref[...]
    o_ref[0] = out


def moe_layer(x, norm, w_router_t, layer, w_gate, w_up, w_down, gate, final_g=None):
    b, n, d = x.shape
    n_exp = w_router_t.shape[0]
    cap = EC_CAPACITY_FACTOR * n // n_exp
    g, sh, sc = norm
    tm = min(512, n)
    per_b2 = pl.BlockSpec((1, 1, d), lambda bi, i: (bi, 0, 0))
    hx, logits = pl.pallas_call(
        _moe_norm_kernel,
        grid=(b, n // tm),
        in_specs=[pl.BlockSpec((1, tm, d), lambda bi, i: (bi, i, 0)),
                  pl.BlockSpec((1, d), lambda bi, i: (0, 0)), per_b2, per_b2,
                  pl.BlockSpec((n_exp, d), lambda bi, i: (0, 0))],
        out_specs=[pl.BlockSpec((1, tm, d), lambda bi, i: (bi, i, 0)),
                   pl.BlockSpec((1, n_exp, tm), lambda bi, i: (bi, 0, i))],
        out_shape=[jax.ShapeDtypeStruct((b, n, d), BF16), jax.ShapeDtypeStruct((b, n_exp, n), F32)],
        compiler_params=_cparams(("parallel", "parallel")),
        name="moe_norm_router",
    )(x, g, sh, sc, w_router_t)

    tri = (np.arange(LANES)[:, None] < np.arange(LANES)[None, :]).astype(np.float32)
    aff, slot = pl.pallas_call(
        functools.partial(_moe_route_kernel, cap=cap),
        grid=(b,),
        in_specs=[pl.BlockSpec((1, n_exp, n), lambda bi: (bi, 0, 0)),
                  pl.BlockSpec((LANES, LANES), lambda bi: (0, 0))],
        out_specs=[pl.BlockSpec((1, n_exp, n), lambda bi: (bi, 0, 0))] * 2,
        out_shape=[jax.ShapeDtypeStruct((b, n_exp, n), F32), jax.ShapeDtypeStruct((b, n_exp, n), jnp.int32)],
        compiler_params=_cparams(("parallel",)),
        name="moe_route",
    )(logits, jnp.asarray(tri, BF16))

    rows_per_step = 512
    ge = max(1, min(n_exp, rows_per_step // cap))
    xs, ta = pl.pallas_call(
        functools.partial(_moe_gather_kernel, cap=cap, ge=ge),
        grid=(b, n_exp // ge),
        in_specs=[pl.BlockSpec((1, n_exp, n), lambda bi, e: (bi, 0, 0)),
                  pl.BlockSpec((1, n_exp, n), lambda bi, e: (bi, 0, 0)),
                  pl.BlockSpec((1, n, d), lambda bi, e: (bi, 0, 0))],
        out_specs=[pl.BlockSpec((1, ge, cap, d), lambda bi, e: (bi, e, 0, 0)),
                   pl.BlockSpec((1, ge, cap, LANES), lambda bi, e: (bi, e, 0, 0))],
        out_shape=[jax.ShapeDtypeStruct((b, n_exp, cap, d), BF16),
                   jax.ShapeDtypeStruct((b, n_exp, cap, LANES), F32)],
        compiler_params=_cparams(("parallel", "arbitrary"), VMEM_LIMIT),
        name="moe_gather",
    )(slot, aff, hx)

    bb = max(1, min(b, rows_per_step // cap))
    f = w_gate.shape[-1]
    ys = pl.pallas_call(
        _moe_ffn_kernel,
        grid=(n_exp, b // bb),
        in_specs=[pl.BlockSpec((bb, 1, cap, d), lambda e, bi: (bi, e, 0, 0)),
                  pl.BlockSpec((bb, 1, cap, LANES), lambda e, bi: (bi, e, 0, 0)),
                  pl.BlockSpec((1, 1, d, f), lambda e, bi: (layer, e, 0, 0)),
                  pl.BlockSpec((1, 1, d, f), lambda e, bi: (layer, e, 0, 0)),
                  pl.BlockSpec((1, 1, f, d), lambda e, bi: (layer, e, 0, 0))],
        out_specs=pl.BlockSpec((bb, 1, cap, d), lambda e, bi: (bi, e, 0, 0)),
        out_shape=jax.ShapeDtypeStruct((b, n_exp, cap, d), BF16),
        scratch_shapes=[pltpu.VMEM((d, f), BF16), pltpu.VMEM((d, f), BF16), pltpu.VMEM((f, d), BF16)],
        compiler_params=_cparams(("parallel", "arbitrary"), VMEM_LIMIT),
        name="moe_ffn",
    )(xs, ta, w_gate, w_up, w_down)

    slot_t = jnp.swapaxes(slot, 1, 2)
    tn = min(512, n)
    args = [slot_t, ys, x, gate]
    specs = [pl.BlockSpec((1, tn, n_exp), lambda bi, i: (bi, i, 0)),
             pl.BlockSpec((1, n_exp, cap, d), lambda bi, i: (bi, 0, 0, 0)),
             pl.BlockSpec((1, tn, d), lambda bi, i: (bi, i, 0)), per_b2]
    if final_g is not None:
        args.append(final_g)
        specs.append(pl.BlockSpec((1, d), lambda bi, i: (0, 0)))
    return pl.pallas_call(
        functools.partial(_moe_scatter_kernel, cap=cap, final=final_g is not None),
        grid=(b, n // tn),
        in_specs=specs,
        out_specs=pl.BlockSpec((1, tn, d), lambda bi, i: (bi, i, 0)),
        out_shape=jax.ShapeDtypeStruct((b, n, d), F32),
        compiler_params=_cparams(("parallel", "arbitrary"), VMEM_LIMIT),
        name="moe_scatter",
    )(*args)


def kernel(x, c, ctx, c_ctx, ada_w, ada_b, mixer_norm_g, moe_norm_g, router_w, moe_w_gate, moe_w_up, moe_w_down, da_w_in, da_lambda_q1, da_lambda_k1, da_lambda_q2, da_lambda_k2, da_subln_g, da_w_out, gm_w_in, gm_v_g, gm_w_s, gm_b_s, gm_w_out, na_w_in, na_rpb, na_w_out, fn_w_out, final_norm_g):
    b, n, d = x.shape
    rows = n // GRID_W
    readers = [i for i in range(DEPTH) if i % N_MIXERS in CTX_READERS]
    last_reader = max(readers) if readers else -1

    r_pad = -(-(b + 1) // 8) * 8
    cond = jnp.zeros((r_pad, d), F32).at[:b].set(c).at[b].set(c_ctx)
    mods = ada_params(cond, ada_w, ada_b)

    def lat_mod(i, k):
        return mods[i, :b, k * d:(k + 1) * d].reshape(b, 1, d)

    def ctx_mod(i, k):
        return jnp.broadcast_to(mods[i, b, k * d:(k + 1) * d].reshape(1, 1, d), (b, 1, d))

    rope = _rope_tables(n)
    perm = _da_perm()

    for i in range(DEPTH):
        kind, j = i % N_MIXERS, i // N_MIXERS
        need_ctx = i <= last_reader
        update_ctx = i < last_reader
        mg = mixer_norm_g[i].reshape(1, d)
        xnorm = (mg, lat_mod(i, 0), lat_mod(i, 1))
        cnorm = (mg, ctx_mod(i, 0), ctx_mod(i, 1)) if need_ctx else None
        g1 = lat_mod(i, 2)

        if kind == MIX_DIFF:
            lam_init = 0.8 - 0.6 * math.exp(-0.3 * i)
            cols = np.arange(2 * DA_WIDTH).reshape(2 * DA_HEADS, LANES)[:, perm].reshape(-1)
            w_in = jnp.concatenate([da_w_in[j][:, cols], da_w_in[j][:, 2 * DA_WIDTH:]], axis=1).astype(BF16)
            w_out = da_w_out[j].astype(BF16)
            lams = [t[j].reshape(1, DA_HEAD_DIM) for t in (da_lambda_q1, da_lambda_k1, da_lambda_q2, da_lambda_k2)]
            sg = da_subln_g[j].reshape(DA_V_DIM, 1)
            nb = DA_WIDTH // LANES
            qkv = linear(x, w_in, norm=xnorm, rope=rope, rope_tiles=2)
            if update_ctx:
                qkv_c = linear(ctx, w_in, norm=cnorm)
                ctx_src = (qkv_c, nb, 2 * nb)
            else:
                ctx_src = (linear(ctx, w_in[:, DA_WIDTH:], norm=cnorm), 0, nb)
            o = diff_attention(qkv, [ctx_src, (qkv, nb, 2 * nb)], lams, sg, lam_init)
            x = linear(o, w_out, res=(x, g1), out_dtype=F32)
            if update_ctx:
                oc = diff_attention(qkv_c, [ctx_src], lams, sg, lam_init)
                ctx_mix = (oc, w_out)
        elif kind == MIX_GMLP:
            w_in = gm_w_in[j].astype(BF16)
            w_out = gm_w_out[j].astype(BF16)
            vg = gm_v_g[j].reshape(1, GM_WIDTH)
            w_s = gm_w_s[j].astype(BF16)
            b_s_t = jnp.transpose(gm_b_s[j])
            x = gmlp_layer(x, xnorm, w_in, vg, w_s, b_s_t, w_out, g1)
            if update_ctx:
                ctx = gmlp_layer(ctx, cnorm, w_in, vg, w_s, b_s_t, w_out, ctx_mod(i, 2))
                ctx_mix = None
        elif kind == MIX_NATTEN:
            w_in = na_w_in[j].astype(BF16)
            w_out = na_w_out[j].astype(BF16)
            qkv = linear(x, w_in, norm=xnorm)
            kv_c = linear(ctx, w_in[:, NA_WIDTH:], norm=cnorm)
            o = natten(qkv, kv_c, _natten_bias(na_rpb[j], rows))
            x = linear(o, w_out, res=(x, g1), out_dtype=F32)
            if update_ctx:
                raise NotImplementedError("context update after a neighbourhood layer")
        else:
            x = fourier_layer(x, xnorm, fn_w_out[j].astype(BF16), g1)
            if update_ctx:
                ctx = fourier_layer(ctx, cnorm, fn_w_out[j].astype(BF16), ctx_mod(i, 2))
                ctx_mix = None

        wr_t = jnp.transpose(router_w[i])
        moe_g = moe_norm_g[i].reshape(1, d)
        last = i == DEPTH - 1
        x = moe_layer(x, (moe_g, lat_mod(i, 3), lat_mod(i, 4)), wr_t, i, moe_w_gate, moe_w_up,
                      moe_w_down, lat_mod(i, 5), final_g=final_norm_g.reshape(1, d) if last else None)
        if update_ctx:
            if ctx_mix is not None:
                oc, w_out = ctx_mix
                ctx = linear(oc, w_out, res=(ctx, ctx_mod(i, 2)), out_dtype=F32)
            ctx = moe_layer(ctx, (moe_g, ctx_mod(i, 3), ctx_mod(i, 4)), wr_t, i, moe_w_gate, moe_w_up,
                            moe_w_down, ctx_mod(i, 5))
    return x
```

```python
import functools
import math

import numpy as np
import jax
import jax.numpy as jnp
from jax import lax
from jax.experimental import pallas as pl
from jax.experimental.pallas import tpu as pltpu

F32 = jnp.float32
BF16 = jnp.bfloat16

D_MODEL = 1024
DEPTH = 4
GRID_W = 64
N_MIXERS = 4
MIX_DIFF, MIX_GMLP, MIX_NATTEN, MIX_FOURIER = 0, 1, 2, 3
CTX_READERS = (MIX_DIFF, MIX_NATTEN)
EPS = 1e-6
NEG_INF = -1e30
ROPE_BASE = 10000.0

DA_HEADS = 8
DA_HEAD_DIM = 64
DA_V_DIM = 2 * DA_HEAD_DIM
DA_WIDTH = DA_HEADS * DA_V_DIM

GM_CHUNK = 128
GM_GROUPS = 8
GM_WIDTH = 2 * D_MODEL
GM_GROUP_DIM = GM_WIDTH // GM_GROUPS

NA_HEADS = 16
NA_HEAD_DIM = D_MODEL // NA_HEADS
NA_WIDTH = NA_HEADS * NA_HEAD_DIM
NA_WIN_R = 8
NA_WIN_C = 16
NA_ROW_BLOCK = 4
NA_SLAB_ROWS = NA_ROW_BLOCK + NA_WIN_R

FN_GROUPS = 4

N_EXPERTS = 16
EC_CAPACITY_FACTOR = 2

LANES = 128
VMEM_LIMIT = 56 * 1024 * 1024


def _cparams(sem, vmem=None):
    return pltpu.CompilerParams(dimension_semantics=sem, vmem_limit_bytes=vmem)


def _rms(x):
    return x * lax.rsqrt(jnp.mean(x * x, axis=-1, keepdims=True) + EPS)


def _ada_kernel(c_ref, w_ref, b_ref, o_ref):
    c = c_ref[...]
    h = (c * jax.nn.sigmoid(c)).astype(BF16)
    o_ref[0] = jnp.dot(h, w_ref[0].astype(BF16), preferred_element_type=F32) + b_ref[0]


def ada_params(cond, ada_w, ada_b):
    r = cond.shape[0]
    depth, d, n_out = ada_w.shape
    tn = 1024
    return pl.pallas_call(
        _ada_kernel,
        grid=(depth, n_out // tn),
        in_specs=[pl.BlockSpec((r, d), lambda l, j: (0, 0)),
                  pl.BlockSpec((1, d, tn), lambda l, j: (l, 0, j)),
                  pl.BlockSpec((1, 1, tn), lambda l, j: (l, 0, j))],
        out_specs=pl.BlockSpec((1, r, tn), lambda l, j: (l, 0, j)),
        out_shape=jax.ShapeDtypeStruct((depth, r, n_out), F32),
        compiler_params=_cparams(("parallel", "parallel")),
        name="ada_params",
    )(cond, ada_w, ada_b.reshape(depth, 1, n_out))


def _moe_prologue(xn, mg_ref, sh_ref, sc_ref, wr_ref, hx_ref, lg_ref):
    h = _rms(xn) * mg_ref[...]
    h = h * (1.0 + sc_ref[0]) + sh_ref[0]
    h_hi = h.astype(BF16)
    h_lo = (h - h_hi.astype(F32)).astype(BF16)
    hx_ref[0] = h_hi
    a = jnp.dot(h_hi, wr_ref[...], preferred_element_type=F32)
    b = jnp.dot(h_lo, wr_ref[:, :LANES], preferred_element_type=F32)
    lt = a[:, :LANES] + a[:, LANES:] + b
    lg_ref[0] = jnp.transpose(lt)[:lg_ref.shape[1]]


def _router_split(w_router):
    w = jnp.pad(w_router, ((0, 0), (0, LANES - w_router.shape[1])))
    hi = w.astype(BF16)
    return jnp.concatenate([hi, (w - hi.astype(F32)).astype(BF16)], axis=1)


def _moe_prologue_specs(d, tm, n_exp):
    in_specs = [pl.BlockSpec((1, d), lambda bi, i: (0, 0)),
                pl.BlockSpec((1, 1, d), lambda bi, i: (bi, 0, 0)),
                pl.BlockSpec((1, 1, d), lambda bi, i: (bi, 0, 0)),
                pl.BlockSpec((d, 2 * LANES), lambda bi, i: (0, 0))]
    out_specs = [pl.BlockSpec((1, tm, d), lambda bi, i: (bi, i, 0)),
                 pl.BlockSpec((1, n_exp, tm), lambda bi, i: (bi, 0, i))]
    return in_specs, out_specs


def _linear_kernel(*refs, has_norm, has_res, has_moe, rope_tiles, tn):
    it = iter(refs)
    x_ref = next(it)
    if has_norm:
        g_ref, sh_ref, sc_ref = next(it), next(it), next(it)
    w_ref = next(it)
    if rope_tiles:
        cos_ref, sin_ref = next(it), next(it)
    if has_res:
        res_ref, gate_ref = next(it), next(it)
    if has_moe:
        moe_in = [next(it) for _ in range(4)]
    o_ref = next(it)

    if has_norm:
        h = _rms(x_ref[0]) * g_ref[...]
        hb = (h * (1.0 + sc_ref[0]) + sh_ref[0]).astype(BF16)
    else:
        hb = x_ref[0].astype(BF16)
    for j in range(w_ref.shape[1] // tn):
        cols = slice(j * tn, (j + 1) * tn)
        y = jnp.dot(hb, w_ref[:, cols], preferred_element_type=F32)
        if j < rope_tiles:
            cos, sin = cos_ref[...], sin_ref[...]
            for s in range(tn // LANES):
                seg = y[:, s * LANES:(s + 1) * LANES]
                rot = pltpu.roll(seg, LANES // 2, axis=1)
                lanes = slice(j * tn + s * LANES, j * tn + (s + 1) * LANES)
                o_ref[0, :, lanes] = (seg * cos + rot * sin).astype(o_ref.dtype)
        elif has_res:
            o_ref[0, :, cols] = (res_ref[0, :, cols] + gate_ref[0, :, cols] * y).astype(o_ref.dtype)
        else:
            o_ref[0, :, cols] = y.astype(o_ref.dtype)
    if has_moe:
        _moe_prologue(o_ref[0], *moe_in, next(it), next(it))


def linear(x, w, *, norm=None, res=None, rope=None, rope_tiles=0, moe=None, out_dtype=BF16, tm=512, tn=1024):
    b, n, k = x.shape
    m = w.shape[1]
    tm = min(tm, n)
    tn = min(tn, m)
    args, specs = [x], [pl.BlockSpec((1, tm, k), lambda bi, i: (bi, i, 0))]
    if norm is not None:
        g, sh, sc = norm
        args += [g, sh, sc]
        specs += [pl.BlockSpec((1, k), lambda bi, i: (0, 0)),
                  pl.BlockSpec((1, 1, k), lambda bi, i: (bi, 0, 0)),
                  pl.BlockSpec((1, 1, k), lambda bi, i: (bi, 0, 0))]
    args.append(w)
    specs.append(pl.BlockSpec((k, m), lambda bi, i: (0, 0)))
    if rope is not None:
        args += list(rope)
        specs += [pl.BlockSpec((tm, LANES), lambda bi, i: (i, 0))] * 2
    if res is not None:
        r, gate = res
        args += [r, gate]
        specs += [pl.BlockSpec((1, tm, m), lambda bi, i: (bi, i, 0)),
                  pl.BlockSpec((1, 1, m), lambda bi, i: (bi, 0, 0))]
    out_specs = [pl.BlockSpec((1, tm, m), lambda bi, i: (bi, i, 0))]
    out_shape = [jax.ShapeDtypeStruct((b, n, m), out_dtype)]
    if moe is not None:
        moe_in, moe_out = _moe_prologue_specs(m, tm, N_EXPERTS)
        args += list(moe)
        specs += moe_in
        out_specs += moe_out
        out_shape += [jax.ShapeDtypeStruct((b, n, m), BF16), jax.ShapeDtypeStruct((b, N_EXPERTS, n), F32)]
    kern = functools.partial(_linear_kernel, has_norm=norm is not None, has_res=res is not None,
                             has_moe=moe is not None, rope_tiles=rope_tiles if rope is not None else 0, tn=tn)
    out = pl.pallas_call(
        kern,
        grid=(b, n // tm),
        in_specs=specs,
        out_specs=out_specs,
        out_shape=out_shape,
        compiler_params=_cparams(("parallel", "parallel"), VMEM_LIMIT),
        name="linear",
    )(*args)
    return out if moe is not None else out[0]


DA_VT_ROWS = DA_V_DIM + 16
ATT_TQ = 256
ATT_KC = 768


def _interleave(stage_a, stage_b):
    for i in range(max(len(stage_a), len(stage_b))):
        if i < len(stage_a):
            stage_a[i]()
        if i < len(stage_b):
            stage_b[i]()


def _diff_attn_kernel(*refs, n_src, lam_init):
    q_ref = refs[0]
    k_refs = refs[1:1 + n_src]
    v_refs = refs[1 + n_src:1 + 2 * n_src]
    lq1_ref, lk1_ref, lq2_ref, lk2_ref, g_ref, o_ref, k_all, vt_all, s_scr, e_scr = refs[1 + 2 * n_src:]

    row = 0
    for k_ref, v_ref in zip(k_refs, v_refs):
        n = k_ref.shape[1]
        k_all[row:row + n, :] = k_ref[0]
        vt_all[:DA_V_DIM, row:row + n] = jnp.transpose(v_ref[0].astype(F32)).astype(BF16)
        row += n
    vt_all[DA_V_DIM:, :] = jnp.ones((DA_VT_ROWS - DA_V_DIM, vt_all.shape[1]), BF16)

    lam = (jnp.exp(jnp.sum(lq1_ref[...] * lk1_ref[...], keepdims=True))
           - jnp.exp(jnp.sum(lq2_ref[...] * lk2_ref[...], keepdims=True)) + lam_init)
    lane = lax.broadcasted_iota(jnp.int32, (1, LANES), 1)
    first = (lane % DA_HEAD_DIM) < (DA_HEAD_DIM // 2)
    dims = (((1,), (1,)), ((), ()))
    c = DA_HEAD_DIM ** -0.5 * math.log2(math.e)
    nk = k_all.shape[0]
    tq = min(ATT_TQ, q_ref.shape[1])
    n_tiles = q_ref.shape[1] // tq
    chunks = [slice(r, min(r + ATT_KC, nk)) for r in range(0, nk, ATT_KC)]
    state = [dict(m=None) for _ in range(n_tiles)]

    def scores(t):
        q = q_ref[0, t * tq:(t + 1) * tq, :]
        zero = jnp.zeros_like(q)
        q2 = jnp.concatenate([jnp.where(first, q, zero), jnp.where(first, zero, q)], axis=0)

        def chunk(rows):
            s = lax.dot_general(k_all[rows, :], q2, dims, preferred_element_type=F32) * c
            s_scr[t % 2, rows, :] = s
            m = jnp.max(s, axis=0, keepdims=True)
            state[t]["m"] = m if state[t]["m"] is None else jnp.maximum(state[t]["m"], m)

        return [functools.partial(chunk, rows) for rows in chunks]

    def exps(t):
        def chunk(rows):
            e_scr[t % 2, rows, :] = jnp.exp2(s_scr[t % 2, rows, :] - state[t]["m"]).astype(BF16)

        return [functools.partial(chunk, rows) for rows in chunks]

    def values(t):
        ot = jnp.dot(vt_all[...], e_scr[t % 2], preferred_element_type=F32)
        ot = ot[:DA_V_DIM] / ot[DA_V_DIM:DA_V_DIM + 1]
        ot = ot[:, :tq] - lam * ot[:, tq:]
        ot = ot * lax.rsqrt(jnp.mean(ot * ot, axis=0, keepdims=True) + EPS) * g_ref[...] * (1.0 - lam_init)
        o_ref[0, t * tq:(t + 1) * tq, :] = jnp.transpose(ot).astype(o_ref.dtype)

    for t in range(n_tiles + 2):
        if 0 <= t - 2 < n_tiles:
            values(t - 2)
        _interleave(scores(t) if t < n_tiles else [],
                    (exps(t - 1) if 0 <= t - 1 < n_tiles else []))


def diff_attention(q_arr, srcs, lams, subln_g, lam_init):
    b, nq, _ = q_arr.shape
    nk = sum(a.shape[1] for a, _, _ in srcs)
    tq = min(ATT_TQ, nq)
    vec = pl.BlockSpec((1, DA_HEAD_DIM), lambda bi, h: (0, 0))
    k_specs = [pl.BlockSpec((1, a.shape[1], LANES), lambda bi, h, c0=kc: (bi, 0, c0 + h)) for a, kc, _ in srcs]
    v_specs = [pl.BlockSpec((1, a.shape[1], LANES), lambda bi, h, c0=vc: (bi, 0, c0 + h)) for a, _, vc in srcs]
    arrs = [a for a, _, _ in srcs]
    return pl.pallas_call(
        functools.partial(_diff_attn_kernel, n_src=len(srcs), lam_init=lam_init),
        grid=(b, DA_HEADS),
        in_specs=[pl.BlockSpec((1, nq, LANES), lambda bi, h: (bi, 0, h))] + k_specs + v_specs
                 + [vec, vec, vec, vec, pl.BlockSpec((DA_V_DIM, 1), lambda bi, h: (0, 0))],
        out_specs=pl.BlockSpec((1, nq, LANES), lambda bi, h: (bi, 0, h)),
        out_shape=jax.ShapeDtypeStruct((b, nq, DA_WIDTH), BF16),
        scratch_shapes=[pltpu.VMEM((nk, LANES), BF16), pltpu.VMEM((DA_VT_ROWS, nk), BF16),
                        pltpu.VMEM((2, nk, 2 * tq), F32), pltpu.VMEM((2, nk, 2 * tq), BF16)],
        compiler_params=_cparams(("parallel", "parallel"), VMEM_LIMIT),
        name="diff_attention",
    )(q_arr, *arrs, *arrs, *lams, subln_g)


def _da_perm():
    perm = np.zeros(LANES, np.int32)
    for l in range(LANES):
        part, within = divmod(l, 64)
        j, rem = divmod(within, 32)
        seg, i = divmod(rem, 16)
        perm[l] = j * 64 + seg * 32 + part * 16 + i
    return perm


def _rope_tables(n_tok):
    t = jnp.arange(n_tok, dtype=jnp.int32)
    n_freq = DA_HEAD_DIM // 4
    inv = ROPE_BASE ** (-jnp.arange(n_freq, dtype=F32) / n_freq)
    ang_r = (t // GRID_W).astype(F32)[:, None] * inv
    ang_c = (t % GRID_W).astype(F32)[:, None] * inv
    cos32 = jnp.concatenate([jnp.cos(ang_r), jnp.cos(ang_c)], axis=1)
    sin32 = jnp.concatenate([jnp.sin(ang_r), jnp.sin(ang_c)], axis=1)
    cos = jnp.tile(cos32, (1, 4))
    sin = jnp.concatenate([-sin32, -sin32, sin32, sin32], axis=1)
    return cos, sin


def _gmlp_kernel(x_ref, g_ref, sh_ref, sc_ref, win_ref, vg_ref, ws_ref, bs_ref, wout_ref, gate_ref,
                 mg_ref, sh2_ref, sc2_ref, wr_ref, o_ref, hx_ref, lg_ref, *, tm):
    x = x_ref[0]
    h = _rms(x) * g_ref[...]
    h = (h * (1.0 + sc_ref[0]) + sh_ref[0]).astype(BF16)
    uv = jax.nn.gelu(jnp.dot(h, win_ref[...], preferred_element_type=F32), approximate=True)
    u = uv[:, :GM_WIDTH]
    v = (_rms(uv[:, GM_WIDTH:]) * vg_ref[...]).astype(BF16)
    rows = []
    for c in range(tm // GM_CHUNK):
        cols = []
        for gi in range(GM_GROUPS):
            vv = v[c * GM_CHUNK:(c + 1) * GM_CHUNK, gi * GM_GROUP_DIM:(gi + 1) * GM_GROUP_DIM]
            sv = jnp.dot(ws_ref[gi], vv, preferred_element_type=F32) + bs_ref[:, gi:gi + 1]
            cols.append(sv)
        rows.append(jnp.concatenate(cols, axis=1))
    sv = jnp.concatenate(rows, axis=0) if len(rows) > 1 else rows[0]
    y = jnp.dot((u * sv).astype(BF16), wout_ref[...], preferred_element_type=F32)
    xn = x + gate_ref[0] * y
    o_ref[0] = xn
    _moe_prologue(xn, mg_ref, sh2_ref, sc2_ref, wr_ref, hx_ref, lg_ref)


def gmlp_layer(x, norm, w_in, v_g, w_s, b_s_t, w_out, gate, moe, tm=256):
    b, n, d = x.shape
    tm = min(tm, n)
    g, sh, sc = norm
    moe_in, moe_out = _moe_prologue_specs(d, tm, N_EXPERTS)
    full = lambda shape: pl.BlockSpec(shape, lambda bi, i: (0,) * len(shape))
    per_b = pl.BlockSpec((1, 1, d), lambda bi, i: (bi, 0, 0))
    return pl.pallas_call(
        functools.partial(_gmlp_kernel, tm=tm),
        grid=(b, n // tm),
        in_specs=[pl.BlockSpec((1, tm, d), lambda bi, i: (bi, i, 0)),
                  full((1, d)), per_b, per_b,
                  full(w_in.shape), full((1, GM_WIDTH)), full(w_s.shape), full(b_s_t.shape),
                  full(w_out.shape), per_b] + moe_in,
        out_specs=[pl.BlockSpec((1, tm, d), lambda bi, i: (bi, i, 0))] + moe_out,
        out_shape=[jax.ShapeDtypeStruct((b, n, d), F32), jax.ShapeDtypeStruct((b, n, d), BF16),
                   jax.ShapeDtypeStruct((b, N_EXPERTS, n), F32)],
        compiler_params=_cparams(("parallel", "arbitrary"), VMEM_LIMIT),
        name="gmlp_layer",
    )(x, g, sh, sc, w_in, v_g, w_s, b_s_t, w_out, gate, *moe)


NA_VT_ROWS = LANES + 16


def _natten_kernel(q_ref, k_ref, v_ref, kc_ref, vc_ref, bias_ref, o_ref, vt_all, vct, s_scr, e_scr, *, n_blocks):
    tq = NA_ROW_BLOCK * GRID_W
    slab_blocks = NA_SLAB_ROWS // NA_ROW_BLOCK
    slab = slab_blocks * tq
    lc = kc_ref.shape[1]
    for blk in range(n_blocks):
        vt_all[:LANES, blk * tq:(blk + 1) * tq] = jnp.transpose(
            v_ref[0, blk * tq:(blk + 1) * tq, :].astype(F32)).astype(BF16)
    vt_all[LANES:, :] = jnp.ones((NA_VT_ROWS - LANES, vt_all.shape[1]), BF16)
    vct[:LANES, :] = jnp.transpose(vc_ref[0].astype(F32)).astype(BF16)
    vct[LANES:, :] = jnp.ones((NA_VT_ROWS - LANES, lc), BF16)

    lane = lax.broadcasted_iota(jnp.int32, (1, LANES), 1)
    left = lane < NA_HEAD_DIM
    dims = (((1,), (1,)), ((), ()))
    c = NA_HEAD_DIM ** -0.5 * math.log2(math.e)
    state = [dict(m=None) for _ in range(n_blocks)]
    key0 = [min(max(t - 1, 0), n_blocks - slab_blocks) * tq for t in range(n_blocks)]
    kind = [0 if t == 0 else 2 if t == n_blocks - 1 else 1 for t in range(n_blocks)]

    def scores(t):
        q = q_ref[0, t * tq:(t + 1) * tq, :]
        zero = jnp.zeros_like(q)
        q2 = jnp.concatenate([jnp.where(left, q, zero), jnp.where(left, zero, q)], axis=0)

        def local():
            k = k_ref[0, key0[t]:key0[t] + slab, :]
            s = lax.dot_general(k, q2, dims, preferred_element_type=F32) * c + bias_ref[0, kind[t]]
            s_scr[t % 2, :slab, :] = s
            state[t]["m"] = jnp.max(s, axis=0, keepdims=True)

        def context():
            s = lax.dot_general(kc_ref[0], q2, dims, preferred_element_type=F32) * c
            s_scr[t % 2, slab:, :] = s
            state[t]["m"] = jnp.maximum(state[t]["m"], jnp.max(s, axis=0, keepdims=True))

        return [local, context]

    def exps(t):
        def chunk(rows):
            e_scr[t % 2, rows, :] = jnp.exp2(s_scr[t % 2, rows, :] - state[t]["m"]).astype(BF16)

        return [functools.partial(chunk, slice(0, slab)), functools.partial(chunk, slice(slab, slab + lc))]

    def values(t):
        vt = jnp.concatenate([vt_all[:, key0[t]:key0[t] + slab], vct[...]], axis=1)
        ot = jnp.dot(vt, e_scr[t % 2], preferred_element_type=F32)
        ot = ot[:LANES] / ot[LANES:LANES + 1]
        pair = jnp.concatenate([ot[:NA_HEAD_DIM, :tq], ot[NA_HEAD_DIM:, tq:]], axis=0)
        o_ref[0, t * tq:(t + 1) * tq, :] = jnp.transpose(pair).astype(o_ref.dtype)

    for t in range(n_blocks + 2):
        if 0 <= t - 2 < n_blocks:
            values(t - 2)
        _interleave(scores(t) if t < n_blocks else [],
                    (exps(t - 1) if 0 <= t - 1 < n_blocks else []))


def _natten_bias(rpb, rows):
    n_heads = rpb.shape[0]
    n_blocks = rows // NA_ROW_BLOCK
    n_dr, n_dc = 2 * NA_WIN_R - 1, 2 * NA_WIN_C - 1
    span = 2 * GRID_W - 1
    lo = GRID_W - NA_WIN_C
    u = jnp.pad(rpb, ((0, 0), (0, 0), (lo, span - n_dc - lo)))
    skew = jnp.tile(u, (1, 1, GRID_W + 1))[:, :, :GRID_W * (span + 1)].reshape(n_heads, n_dr, GRID_W, span + 1)
    toep = skew[:, :, ::-1, :GRID_W]
    pad = NA_ROW_BLOCK
    toep = jnp.pad(toep, ((0, 0), (pad, pad), (0, 0), (0, 0)))
    i = np.arange(NA_ROW_BLOCK)[:, None, None, None]
    c = np.arange(GRID_W)[None, :, None, None]
    m = np.arange(NA_SLAB_ROWS)[None, None, :, None]
    kc = np.arange(GRID_W)[None, None, None, :]
    win_c0 = np.clip(c - NA_WIN_C // 2, 0, GRID_W - NA_WIN_C)
    col_ok = (kc >= win_c0) & (kc < win_c0 + NA_WIN_C)
    tiles, ok = [], []
    for rb in (0, 1, n_blocks - 1):
        slab0 = int(np.clip(rb - 1, 0, n_blocks - 3)) * NA_ROW_BLOCK
        r = rb * NA_ROW_BLOCK + i
        r0 = np.clip(r - NA_WIN_R // 2, 0, rows - NA_WIN_R)
        kr = slab0 + m
        ok.append(np.broadcast_to((kr >= r0) & (kr < r0 + NA_WIN_R) & col_ok,
                                  (NA_ROW_BLOCK, GRID_W, NA_SLAB_ROWS, GRID_W)))
        for ii in range(NA_ROW_BLOCK):
            off = slab0 - (rb * NA_ROW_BLOCK + ii) + NA_WIN_R - 1 + pad
            tiles.append(toep[:, off:off + NA_SLAB_ROWS])
    tq, slab = NA_ROW_BLOCK * GRID_W, NA_SLAB_ROWS * GRID_W
    bias = jnp.stack(tiles, axis=1).reshape(n_heads // 2, 2, 3, NA_ROW_BLOCK, NA_SLAB_ROWS, GRID_W, GRID_W)
    bias = jnp.transpose(bias, (0, 2, 4, 6, 1, 3, 5)).reshape(n_heads // 2, 3, slab, 2 * tq)
    ok = np.stack(ok).transpose(0, 3, 4, 1, 2).reshape(3, slab, tq)
    ok = np.concatenate([ok, ok], axis=2)
    return jnp.where(ok[None], bias * math.log2(math.e), NEG_INF)


def natten(qkv, kv_ctx, bias):
    b, n, _ = qkv.shape
    lc = kv_ctx.shape[1]
    tq = NA_ROW_BLOCK * GRID_W
    n_blocks = n // tq
    hp = NA_HEADS // 2
    return pl.pallas_call(
        functools.partial(_natten_kernel, n_blocks=n_blocks),
        grid=(hp, b),
        in_specs=[pl.BlockSpec((1, n, LANES), lambda p, bi: (bi, 0, p)),
                  pl.BlockSpec((1, n, LANES), lambda p, bi: (bi, 0, hp + p)),
                  pl.BlockSpec((1, n, LANES), lambda p, bi: (bi, 0, 2 * hp + p)),
                  pl.BlockSpec((1, lc, LANES), lambda p, bi: (bi, 0, p)),
                  pl.BlockSpec((1, lc, LANES), lambda p, bi: (bi, 0, hp + p)),
                  pl.BlockSpec((1, 3, NA_SLAB_ROWS * GRID_W, 2 * tq), lambda p, bi: (p, 0, 0, 0))],
        out_specs=pl.BlockSpec((1, n, LANES), lambda p, bi: (bi, 0, p)),
        out_shape=jax.ShapeDtypeStruct((b, n, NA_WIDTH), BF16),
        scratch_shapes=[pltpu.VMEM((NA_VT_ROWS, n), BF16), pltpu.VMEM((NA_VT_ROWS, lc), BF16),
                        pltpu.VMEM((2, NA_SLAB_ROWS * GRID_W + lc, 2 * tq), F32),
                        pltpu.VMEM((2, NA_SLAB_ROWS * GRID_W + lc, 2 * tq), BF16)],
        compiler_params=_cparams(("parallel", "parallel"), VMEM_LIMIT),
        name="natten",
    )(qkv, qkv, qkv, kv_ctx, kv_ctx, bias)


def _dft_tables(n, sign=1.0):
    k = jnp.arange(n, dtype=jnp.int32)
    ang = ((k[:, None] * k[None, :]) % n).astype(F32) * (2.0 * math.pi / n)
    return jnp.cos(ang), sign * jnp.sin(ang)


def _fourier_chan_kernel(x_ref, g_ref, sh_ref, sc_ref, wc_ref, o_ref):
    h = _rms(x_ref[0]) * g_ref[...]
    h = (h * (1.0 + sc_ref[0]) + sh_ref[0]).astype(BF16)
    gd = D_MODEL // FN_GROUPS
    for gi in range(FN_GROUPS):
        z = jnp.dot(h[:, gi * gd:(gi + 1) * gd], wc_ref[...], preferred_element_type=F32).astype(BF16)
        o_ref[0, 0, :, gi * gd:(gi + 1) * gd] = z[:, :gd]
        o_ref[0, 1, :, gi * gd:(gi + 1) * gd] = z[:, gd:]


def _fourier_pos_kernel(wp_ref, z_ref, wout_ref, x_ref, gate_ref, mg_ref, sh2_ref, sc2_ref, wr_ref,
                        o_ref, hx_ref, lg_ref, *, scale):
    f = jnp.dot(wp_ref[...], z_ref[0], preferred_element_type=F32) * scale
    y = jnp.dot(f.astype(BF16), wout_ref[...], preferred_element_type=F32)
    xn = x_ref[0] + gate_ref[0] * y
    o_ref[0] = xn
    _moe_prologue(xn, mg_ref, sh2_ref, sc2_ref, wr_ref, hx_ref, lg_ref)


def fourier_layer(x, norm, w_out, gate, moe, tm=512):
    b, n, d = x.shape
    moe_in, moe_out = _moe_prologue_specs(d, tm, N_EXPERTS)
    gd = d // FN_GROUPS
    g, sh, sc = norm
    cc, sc_tab = _dft_tables(gd)
    wc = jnp.concatenate([cc, sc_tab], axis=1).astype(BF16)
    cn, sn = _dft_tables(n, -1.0)
    wp = jnp.concatenate([cn, sn], axis=1).astype(BF16)
    per_b = pl.BlockSpec((1, 1, d), lambda bi, i: (bi, 0, 0))
    z = pl.pallas_call(
        _fourier_chan_kernel,
        grid=(b, n // tm),
        in_specs=[pl.BlockSpec((1, tm, d), lambda bi, i: (bi, i, 0)),
                  pl.BlockSpec((1, d), lambda bi, i: (0, 0)), per_b, per_b,
                  pl.BlockSpec((gd, 2 * gd), lambda bi, i: (0, 0))],
        out_specs=pl.BlockSpec((1, 2, tm, d), lambda bi, i: (bi, 0, i, 0)),
        out_shape=jax.ShapeDtypeStruct((b, 2, n, d), BF16),
        compiler_params=_cparams(("parallel", "parallel")),
        name="fourier_chan",
    )(x, g, sh, sc, wc)
    z = z.reshape(b, 2 * n, d)
    return pl.pallas_call(
        functools.partial(_fourier_pos_kernel, scale=1.0 / math.sqrt(n * gd)),
        grid=(b, n // tm),
        in_specs=[pl.BlockSpec((tm, 2 * n), lambda bi, i: (i, 0)),
                  pl.BlockSpec((1, 2 * n, d), lambda bi, i: (bi, 0, 0)),
                  pl.BlockSpec((d, d), lambda bi, i: (0, 0)),
                  pl.BlockSpec((1, tm, d), lambda bi, i: (bi, i, 0)), per_b] + moe_in,
        out_specs=[pl.BlockSpec((1, tm, d), lambda bi, i: (bi, i, 0))] + moe_out,
        out_shape=[jax.ShapeDtypeStruct((b, n, d), F32), jax.ShapeDtypeStruct((b, n, d), BF16),
                   jax.ShapeDtypeStruct((b, N_EXPERTS, n), F32)],
        compiler_params=_cparams(("parallel", "arbitrary"), VMEM_LIMIT),
        name="fourier_pos",
    )(wp, z, w_out, x, gate, *moe)


def _excl_cumsum_lanes(mask, tri):
    e, n = mask.shape
    mf = jnp.where(mask, 1.0, 0.0)
    offset = jnp.zeros((e, 1), F32)
    parts = []
    for blk in range(n // LANES):
        part = mf[:, blk * LANES:(blk + 1) * LANES]
        parts.append(jnp.dot(part.astype(BF16), tri, preferred_element_type=F32) + offset)
        offset = offset + jnp.sum(part, axis=1, keepdims=True)
    return jnp.concatenate(parts, axis=1)


def _moe_route_kernel(lg_ref, tri_ref, aff_ref, slot_ref, *, cap):
    lg = lg_ref[0]
    e = jnp.exp(lg - jnp.max(lg, axis=0, keepdims=True))
    aff = e / jnp.sum(e, axis=0, keepdims=True)
    aff_ref[0] = aff
    n_exp = lg.shape[0]

    def as_float(bits):
        return lax.bitcast_convert_type(bits, F32)

    def step(it, thr):
        cand = thr | jnp.left_shift(jnp.int32(1), 30 - it)
        cnt = jnp.sum((aff >= as_float(cand)).astype(jnp.int32), axis=1, keepdims=True)
        return jnp.where(cnt >= cap, cand, thr)

    thr = lax.fori_loop(0, 31, step, jnp.zeros((n_exp, 1), jnp.int32))
    gt = aff >= as_float(thr + 1)
    eq = (aff >= as_float(thr)) & jnp.logical_not(gt)
    need = (cap - jnp.sum(gt.astype(jnp.int32), axis=1, keepdims=True)).astype(F32)
    tri = tri_ref[...]
    sel = gt | (eq & (_excl_cumsum_lanes(eq, tri) < need))
    pos = _excl_cumsum_lanes(sel, tri).astype(jnp.int32)
    slot_ref[0] = jnp.where(sel, pos, -1)


def _moe_gather_kernel(slot_ref, aff_ref, hx_ref, xs_ref, ta_ref, *, cap, ge):
    e0 = pl.multiple_of(pl.program_id(1) * ge, ge)
    n = slot_ref.shape[2]
    rows = lax.broadcasted_iota(jnp.int32, (cap, n), 0)
    hits, tas = [], []
    for k in range(ge):
        hit = rows == slot_ref[0, pl.ds(e0 + k, 1), :]
        hits.append(jnp.where(hit, 1.0, 0.0).astype(BF16))
        tas.append(jnp.sum(jnp.where(hit, aff_ref[0, pl.ds(e0 + k, 1), :], 0.0), axis=1, keepdims=True))
    onehot = jnp.concatenate(hits, axis=0) if ge > 1 else hits[0]
    xs = jnp.dot(onehot, hx_ref[0], preferred_element_type=F32).astype(BF16)
    xs_ref[0] = xs.reshape(ge, cap, xs.shape[-1])
    ta = jnp.concatenate(tas, axis=0) if ge > 1 else tas[0]
    ta_ref[0] = jnp.broadcast_to(ta, (ge * cap, LANES)).reshape(ge, cap, LANES)


def _moe_ffn_kernel(xs_ref, ta_ref, wg_ref, wu_ref, wd_ref, ys_ref, wgb, wub, wdb):
    @pl.when(pl.program_id(1) == 0)
    def _():
        wgb[...] = wg_ref[0, 0].astype(BF16)
        wub[...] = wu_ref[0, 0].astype(BF16)
        wdb[...] = wd_ref[0, 0].astype(BF16)

    bb, _, cap, d = xs_ref.shape
    xs = xs_ref[...].reshape(bb * cap, d)
    ta = ta_ref[...].reshape(bb * cap, LANES)[:, :1]
    gate = jnp.dot(xs, wgb[...], preferred_element_type=F32)
    up = jnp.dot(xs, wub[...], preferred_element_type=F32)
    hid = (gate * jax.nn.sigmoid(gate) * up).astype(BF16)
    y = jnp.dot(hid, wdb[...], preferred_element_type=F32) * ta
    ys_ref[...] = y.astype(BF16).reshape(bb, 1, cap, d)


def _moe_scatter_kernel(slot_ref, ys_ref, x_ref, gate_ref, *rest, cap, final):
    if final:
        fg_ref, o_ref = rest
    else:
        (o_ref,) = rest
    slot = slot_ref[0]
    n_exp = slot.shape[1]
    if cap % LANES == 0:
        want = lax.broadcasted_iota(jnp.int32, (1, cap), 1)
        hit = jnp.concatenate([jnp.where(slot[:, e:e + 1] == want, 1.0, 0.0).astype(BF16) for e in range(n_exp)],
                              axis=1)
    else:
        col = lax.broadcasted_iota(jnp.int32, (n_exp, n_exp * cap), 1)
        row = lax.broadcasted_iota(jnp.int32, (n_exp, n_exp * cap), 0)
        spread = jnp.where(col // cap == row, 1.0, 0.0).astype(BF16)
        want = (lax.broadcasted_iota(jnp.int32, (1, n_exp * cap), 1) % cap).astype(F32)
        ids = jnp.dot(slot.astype(F32).astype(BF16), spread, preferred_element_type=F32)
        hit = jnp.where(ids == want, 1.0, 0.0).astype(BF16)
    ys = ys_ref[0].reshape(n_exp * cap, ys_ref.shape[-1])
    out = x_ref[0] + gate_ref[0] * jnp.dot(hit, ys, preferred_element_type=F32)
    if final:
        out = _rms(out) * fg_ref[...]
    o_ref[0] = out


def moe_layer(x, hx, logits, layer, w_gate, w_up, w_down, gate, final_g=None):
    b, n, d = x.shape
    n_exp = logits.shape[1]
    cap = EC_CAPACITY_FACTOR * n // n_exp
    per_b2 = pl.BlockSpec((1, 1, d), lambda bi, i: (bi, 0, 0))
    tri = (np.arange(LANES)[:, None] < np.arange(LANES)[None, :]).astype(np.float32)
    aff, slot = pl.pallas_call(
        functools.partial(_moe_route_kernel, cap=cap),
        grid=(b,),
        in_specs=[pl.BlockSpec((1, n_exp, n), lambda bi: (bi, 0, 0)),
                  pl.BlockSpec((LANES, LANES), lambda bi: (0, 0))],
        out_specs=[pl.BlockSpec((1, n_exp, n), lambda bi: (bi, 0, 0))] * 2,
        out_shape=[jax.ShapeDtypeStruct((b, n_exp, n), F32), jax.ShapeDtypeStruct((b, n_exp, n), jnp.int32)],
        compiler_params=_cparams(("parallel",)),
        name="moe_route",
    )(logits, jnp.asarray(tri, BF16))

    rows_per_step = 512
    ge = max(1, min(n_exp, rows_per_step // cap))
    xs, ta = pl.pallas_call(
        functools.partial(_moe_gather_kernel, cap=cap, ge=ge),
        grid=(b, n_exp // ge),
        in_specs=[pl.BlockSpec((1, n_exp, n), lambda bi, e: (bi, 0, 0)),
                  pl.BlockSpec((1, n_exp, n), lambda bi, e: (bi, 0, 0)),
                  pl.BlockSpec((1, n, d), lambda bi, e: (bi, 0, 0))],
        out_specs=[pl.BlockSpec((1, ge, cap, d), lambda bi, e: (bi, e, 0, 0)),
                   pl.BlockSpec((1, ge, cap, LANES), lambda bi, e: (bi, e, 0, 0))],
        out_shape=[jax.ShapeDtypeStruct((b, n_exp, cap, d), BF16),
                   jax.ShapeDtypeStruct((b, n_exp, cap, LANES), F32)],
        compiler_params=_cparams(("parallel", "arbitrary"), VMEM_LIMIT),
        name="moe_gather",
    )(slot, aff, hx)

    bb = max(1, min(b, rows_per_step // cap))
    f = w_gate.shape[-1]
    ys = pl.pallas_call(
        _moe_ffn_kernel,
        grid=(n_exp, b // bb),
        in_specs=[pl.BlockSpec((bb, 1, cap, d), lambda e, bi: (bi, e, 0, 0)),
                  pl.BlockSpec((bb, 1, cap, LANES), lambda e, bi: (bi, e, 0, 0)),
                  pl.BlockSpec((1, 1, d, f), lambda e, bi: (layer, e, 0, 0)),
                  pl.BlockSpec((1, 1, d, f), lambda e, bi: (layer, e, 0, 0)),
                  pl.BlockSpec((1, 1, f, d), lambda e, bi: (layer, e, 0, 0))],
        out_specs=pl.BlockSpec((bb, 1, cap, d), lambda e, bi: (bi, e, 0, 0)),
        out_shape=jax.ShapeDtypeStruct((b, n_exp, cap, d), BF16),
        scratch_shapes=[pltpu.VMEM((d, f), BF16), pltpu.VMEM((d, f), BF16), pltpu.VMEM((f, d), BF16)],
        compiler_params=_cparams(("parallel", "arbitrary"), VMEM_LIMIT),
        name="moe_ffn",
    )(xs, ta, w_gate, w_up, w_down)

    slot_t = jnp.swapaxes(slot, 1, 2)
    tn = min(512, n)
    args = [slot_t, ys, x, gate]
    specs = [pl.BlockSpec((1, tn, n_exp), lambda bi, i: (bi, i, 0)),
             pl.BlockSpec((1, n_exp, cap, d), lambda bi, i: (bi, 0, 0, 0)),
             pl.BlockSpec((1, tn, d), lambda bi, i: (bi, i, 0)), per_b2]
    if final_g is not None:
        args.append(final_g)
        specs.append(pl.BlockSpec((1, d), lambda bi, i: (0, 0)))
    return pl.pallas_call(
        functools.partial(_moe_scatter_kernel, cap=cap, final=final_g is not None),
        grid=(b, n // tn),
        in_specs=specs,
        out_specs=pl.BlockSpec((1, tn, d), lambda bi, i: (bi, i, 0)),
        out_shape=jax.ShapeDtypeStruct((b, n, d), F32),
        compiler_params=_cparams(("parallel", "arbitrary"), VMEM_LIMIT),
        name="moe_scatter",
    )(*args)


def kernel(x, c, ctx, c_ctx, ada_w, ada_b, mixer_norm_g, moe_norm_g, router_w, moe_w_gate, moe_w_up, moe_w_down, da_w_in, da_lambda_q1, da_lambda_k1, da_lambda_q2, da_lambda_k2, da_subln_g, da_w_out, gm_w_in, gm_v_g, gm_w_s, gm_b_s, gm_w_out, na_w_in, na_rpb, na_w_out, fn_w_out, final_norm_g):
    b, n, d = x.shape
    rows = n // GRID_W
    readers = [i for i in range(DEPTH) if i % N_MIXERS in CTX_READERS]
    last_reader = max(readers) if readers else -1

    r_pad = -(-(b + 1) // 8) * 8
    cond = jnp.zeros((r_pad, d), F32).at[:b].set(c).at[b].set(c_ctx)
    mods = ada_params(cond, ada_w, ada_b)

    def lat_mod(i, k):
        return mods[i, :b, k * d:(k + 1) * d].reshape(b, 1, d)

    def ctx_mod(i, k):
        return jnp.broadcast_to(mods[i, b, k * d:(k + 1) * d].reshape(1, 1, d), (b, 1, d))

    rope = _rope_tables(n)
    perm = _da_perm()

    for i in range(DEPTH):
        kind, j = i % N_MIXERS, i // N_MIXERS
        need_ctx = i <= last_reader
        update_ctx = i < last_reader
        mg = mixer_norm_g[i].reshape(1, d)
        xnorm = (mg, lat_mod(i, 0), lat_mod(i, 1))
        cnorm = (mg, ctx_mod(i, 0), ctx_mod(i, 1)) if need_ctx else None
        g1 = lat_mod(i, 2)
        wr = _router_split(router_w[i])
        moe_g = moe_norm_g[i].reshape(1, d)
        xmoe = (moe_g, lat_mod(i, 3), lat_mod(i, 4), wr)
        cmoe = (moe_g, ctx_mod(i, 3), ctx_mod(i, 4), wr) if update_ctx else None

        if kind == MIX_DIFF:
            lam_init = 0.8 - 0.6 * math.exp(-0.3 * i)
            cols = np.arange(2 * DA_WIDTH).reshape(2 * DA_HEADS, LANES)[:, perm].reshape(-1)
            w_in = jnp.concatenate([da_w_in[j][:, cols], da_w_in[j][:, 2 * DA_WIDTH:]], axis=1).astype(BF16)
            w_out = da_w_out[j].astype(BF16)
            lams = [t[j].reshape(1, DA_HEAD_DIM) for t in (da_lambda_q1, da_lambda_k1, da_lambda_q2, da_lambda_k2)]
            sg = da_subln_g[j].reshape(DA_V_DIM, 1)
            nb = DA_WIDTH // LANES
            qkv = linear(x, w_in, norm=xnorm, rope=rope, rope_tiles=2)
            if update_ctx:
                qkv_c = linear(ctx, w_in, norm=cnorm)
                ctx_src = (qkv_c, nb, 2 * nb)
            else:
                ctx_src = (linear(ctx, w_in[:, DA_WIDTH:], norm=cnorm), 0, nb)
            o = diff_attention(qkv, [ctx_src, (qkv, nb, 2 * nb)], lams, sg, lam_init)
            x, hx, logits = linear(o, w_out, res=(x, g1), moe=xmoe, out_dtype=F32)
            if update_ctx:
                oc = diff_attention(qkv_c, [ctx_src], lams, sg, lam_init)
                ctx, hc, logits_c = linear(oc, w_out, res=(ctx, ctx_mod(i, 2)), moe=cmoe, out_dtype=F32)
        elif kind == MIX_GMLP:
            w_in = gm_w_in[j].astype(BF16)
            w_out = gm_w_out[j].astype(BF16)
            vg = gm_v_g[j].reshape(1, GM_WIDTH)
            w_s = gm_w_s[j].astype(BF16)
            b_s_t = jnp.transpose(gm_b_s[j])
            x, hx, logits = gmlp_layer(x, xnorm, w_in, vg, w_s, b_s_t, w_out, g1, xmoe)
            if update_ctx:
                ctx, hc, logits_c = gmlp_layer(ctx, cnorm, w_in, vg, w_s, b_s_t, w_out, ctx_mod(i, 2), cmoe)
        elif kind == MIX_NATTEN:
            w_in = na_w_in[j].astype(BF16)
            w_out = na_w_out[j].astype(BF16)
            qkv = linear(x, w_in, norm=xnorm)
            kv_c = linear(ctx, w_in[:, NA_WIDTH:], norm=cnorm)
            o = natten(qkv, kv_c, _natten_bias(na_rpb[j], rows))
            x, hx, logits = linear(o, w_out, res=(x, g1), moe=xmoe, out_dtype=F32)
            if update_ctx:
                raise NotImplementedError("context update after a neighbourhood layer")
        else:
            x, hx, logits = fourier_layer(x, xnorm, fn_w_out[j].astype(BF16), g1, xmoe)
            if update_ctx:
                ctx, hc, logits_c = fourier_layer(ctx, cnorm, fn_w_out[j].astype(BF16), ctx_mod(i, 2), cmoe)

        last = i == DEPTH - 1
        x = moe_layer(x, hx, logits, i, moe_w_gate, moe_w_up, moe_w_down, lat_mod(i, 5),
                      final_g=final_norm_g.reshape(1, d) if last else None)
        if update_ctx:
            ctx = moe_layer(ctx, hc, logits_c, i, moe_w_gate, moe_w_up, moe_w_down, ctx_mod(i, 5))
    return x
```

```python
import functools
import math

import numpy as np
import jax
import jax.numpy as jnp
from jax import lax
from jax.experimental import pallas as pl
from jax.experimental.pallas import tpu as pltpu

F32 = jnp.float32
BF16 = jnp.bfloat16

D_MODEL = 1024
DEPTH = 4
GRID_W = 64
N_MIXERS = 4
MIX_DIFF, MIX_GMLP, MIX_NATTEN, MIX_FOURIER = 0, 1, 2, 3
CTX_READERS = (MIX_DIFF, MIX_NATTEN)
EPS = 1e-6
NEG_INF = -1e30
ROPE_BASE = 10000.0

DA_HEADS = 8
DA_HEAD_DIM = 64
DA_V_DIM = 2 * DA_HEAD_DIM
DA_WIDTH = DA_HEADS * DA_V_DIM

GM_CHUNK = 128
GM_GROUPS = 8
GM_WIDTH = 2 * D_MODEL
GM_GROUP_DIM = GM_WIDTH // GM_GROUPS

NA_HEADS = 16
NA_HEAD_DIM = D_MODEL // NA_HEADS
NA_WIDTH = NA_HEADS * NA_HEAD_DIM
NA_WIN_R = 8
NA_WIN_C = 16
NA_ROW_BLOCK = 4
NA_SLAB_ROWS = NA_ROW_BLOCK + NA_WIN_R

FN_GROUPS = 4

N_EXPERTS = 16
EC_CAPACITY_FACTOR = 2

LANES = 128
VMEM_LIMIT = 56 * 1024 * 1024


def _cparams(sem, vmem=None):
    return pltpu.CompilerParams(dimension_semantics=sem, vmem_limit_bytes=vmem)


def _rms(x):
    return x * lax.rsqrt(jnp.mean(x * x, axis=-1, keepdims=True) + EPS)


def _ada_kernel(c_ref, w_ref, b_ref, o_ref):
    c = c_ref[...]
    h = (c * jax.nn.sigmoid(c)).astype(BF16)
    o_ref[0] = jnp.dot(h, w_ref[0].astype(BF16), preferred_element_type=F32) + b_ref[0]


def ada_params(cond, ada_w, ada_b):
    r = cond.shape[0]
    depth, d, n_out = ada_w.shape
    tn = 1024
    return pl.pallas_call(
        _ada_kernel,
        grid=(depth, n_out // tn),
        in_specs=[pl.BlockSpec((r, d), lambda l, j: (0, 0)),
                  pl.BlockSpec((1, d, tn), lambda l, j: (l, 0, j)),
                  pl.BlockSpec((1, 1, tn), lambda l, j: (l, 0, j))],
        out_specs=pl.BlockSpec((1, r, tn), lambda l, j: (l, 0, j)),
        out_shape=jax.ShapeDtypeStruct((depth, r, n_out), F32),
        compiler_params=_cparams(("parallel", "parallel")),
        name="ada_params",
    )(cond, ada_w, ada_b.reshape(depth, 1, n_out))


def _moe_prologue(xn, mg_ref, sh_ref, sc_ref, wr_ref, hx_ref, lg_ref):
    h = _rms(xn) * mg_ref[...]
    h = h * (1.0 + sc_ref[0]) + sh_ref[0]
    h_hi = h.astype(BF16)
    h_lo = (h - h_hi.astype(F32)).astype(BF16)
    hx_ref[0] = h_hi
    a = jnp.dot(h_hi, wr_ref[...], preferred_element_type=F32)
    b = jnp.dot(h_lo, wr_ref[:, :LANES], preferred_element_type=F32)
    lt = a[:, :LANES] + a[:, LANES:] + b
    lg_ref[0] = jnp.transpose(lt)[:lg_ref.shape[1]]


def _router_split(w_router):
    w = jnp.pad(w_router, ((0, 0), (0, LANES - w_router.shape[1])))
    hi = w.astype(BF16)
    return jnp.concatenate([hi, (w - hi.astype(F32)).astype(BF16)], axis=1)


def _moe_prologue_specs(d, tm, n_exp):
    in_specs = [pl.BlockSpec((1, d), lambda bi, i: (0, 0)),
                pl.BlockSpec((1, 1, d), lambda bi, i: (bi, 0, 0)),
                pl.BlockSpec((1, 1, d), lambda bi, i: (bi, 0, 0)),
                pl.BlockSpec((d, 2 * LANES), lambda bi, i: (0, 0))]
    out_specs = [pl.BlockSpec((1, tm, d), lambda bi, i: (bi, i, 0)),
                 pl.BlockSpec((1, n_exp, tm), lambda bi, i: (bi, 0, i))]
    return in_specs, out_specs


def _linear_kernel(*refs, has_norm, has_res, has_moe, rope_tiles, tn):
    it = iter(refs)
    x_ref = next(it)
    if has_norm:
        g_ref, sh_ref, sc_ref = next(it), next(it), next(it)
    w_ref = next(it)
    if rope_tiles:
        cos_ref, sin_ref = next(it), next(it)
    if has_res:
        res_ref, gate_ref = next(it), next(it)
    if has_moe:
        moe_in = [next(it) for _ in range(4)]
    o_ref = next(it)

    if has_norm:
        h = _rms(x_ref[0]) * g_ref[...]
        hb = (h * (1.0 + sc_ref[0]) + sh_ref[0]).astype(BF16)
    else:
        hb = x_ref[0].astype(BF16)
    for j in range(w_ref.shape[1] // tn):
        cols = slice(j * tn, (j + 1) * tn)
        y = jnp.dot(hb, w_ref[:, cols], preferred_element_type=F32)
        if j < rope_tiles:
            cos, sin = cos_ref[...], sin_ref[...]
            for s in range(tn // LANES):
                seg = y[:, s * LANES:(s + 1) * LANES]
                rot = pltpu.roll(seg, LANES // 2, axis=1)
                lanes = slice(j * tn + s * LANES, j * tn + (s + 1) * LANES)
                o_ref[0, :, lanes] = (seg * cos + rot * sin).astype(o_ref.dtype)
        elif has_res:
            o_ref[0, :, cols] = (res_ref[0, :, cols] + gate_ref[0, :, cols] * y).astype(o_ref.dtype)
        else:
            o_ref[0, :, cols] = y.astype(o_ref.dtype)
    if has_moe:
        _moe_prologue(o_ref[0], *moe_in, next(it), next(it))


def linear(x, w, *, norm=None, res=None, rope=None, rope_tiles=0, moe=None, out_dtype=BF16, tm=512, tn=1024):
    b, n, k = x.shape
    m = w.shape[1]
    tm = min(tm, n)
    tn = min(tn, m)
    args, specs = [x], [pl.BlockSpec((1, tm, k), lambda bi, i: (bi, i, 0))]
    if norm is not None:
        g, sh, sc = norm
        args += [g, sh, sc]
        specs += [pl.BlockSpec((1, k), lambda bi, i: (0, 0)),
                  pl.BlockSpec((1, 1, k), lambda bi, i: (bi, 0, 0)),
                  pl.BlockSpec((1, 1, k), lambda bi, i: (bi, 0, 0))]
    args.append(w)
    specs.append(pl.BlockSpec((k, m), lambda bi, i: (0, 0)))
    if rope is not None:
        args += list(rope)
        specs += [pl.BlockSpec((tm, LANES), lambda bi, i: (i, 0))] * 2
    if res is not None:
        r, gate = res
        args += [r, gate]
        specs += [pl.BlockSpec((1, tm, m), lambda bi, i: (bi, i, 0)),
                  pl.BlockSpec((1, 1, m), lambda bi, i: (bi, 0, 0))]
    out_specs = [pl.BlockSpec((1, tm, m), lambda bi, i: (bi, i, 0))]
    out_shape = [jax.ShapeDtypeStruct((b, n, m), out_dtype)]
    if moe is not None:
        moe_in, moe_out = _moe_prologue_specs(m, tm, N_EXPERTS)
        args += list(moe)
        specs += moe_in
        out_specs += moe_out
        out_shape += [jax.ShapeDtypeStruct((b, n, m), BF16), jax.ShapeDtypeStruct((b, N_EXPERTS, n), F32)]
    kern = functools.partial(_linear_kernel, has_norm=norm is not None, has_res=res is not None,
                             has_moe=moe is not None, rope_tiles=rope_tiles if rope is not None else 0, tn=tn)
    out = pl.pallas_call(
        kern,
        grid=(b, n // tm),
        in_specs=specs,
        out_specs=out_specs,
        out_shape=out_shape,
        compiler_params=_cparams(("parallel", "parallel"), VMEM_LIMIT),
        name="linear",
    )(*args)
    return out if moe is not None else out[0]


DA_VT_ROWS = DA_V_DIM + 16
ATT_TQ = 256
ATT_KC = 768


def _interleave(stage_a, stage_b):
    for i in range(max(len(stage_a), len(stage_b))):
        if i < len(stage_a):
            stage_a[i]()
        if i < len(stage_b):
            stage_b[i]()


def _diff_attn_kernel(*refs, n_src, lam_init):
    q_ref = refs[0]
    k_refs = refs[1:1 + n_src]
    v_refs = refs[1 + n_src:1 + 2 * n_src]
    lq1_ref, lk1_ref, lq2_ref, lk2_ref, g_ref, o_ref, k_all, vt_all, s_scr, e_scr = refs[1 + 2 * n_src:]

    row = 0
    for k_ref, v_ref in zip(k_refs, v_refs):
        n = k_ref.shape[1]
        k_all[row:row + n, :] = k_ref[0]
        vt_all[:DA_V_DIM, row:row + n] = jnp.transpose(v_ref[0].astype(F32)).astype(BF16)
        row += n
    vt_all[DA_V_DIM:, :] = jnp.ones((DA_VT_ROWS - DA_V_DIM, vt_all.shape[1]), BF16)

    lam = (jnp.exp(jnp.sum(lq1_ref[...] * lk1_ref[...], keepdims=True))
           - jnp.exp(jnp.sum(lq2_ref[...] * lk2_ref[...], keepdims=True)) + lam_init)
    lane = lax.broadcasted_iota(jnp.int32, (1, LANES), 1)
    first = (lane % DA_HEAD_DIM) < (DA_HEAD_DIM // 2)
    dims = (((1,), (1,)), ((), ()))
    c = DA_HEAD_DIM ** -0.5 * math.log2(math.e)
    nk = k_all.shape[0]
    tq = min(ATT_TQ, q_ref.shape[1])
    n_tiles = q_ref.shape[1] // tq
    chunks = [slice(r, min(r + ATT_KC, nk)) for r in range(0, nk, ATT_KC)]
    state = [dict(m=None) for _ in range(n_tiles)]

    def scores(t):
        q = q_ref[0, t * tq:(t + 1) * tq, :]
        zero = jnp.zeros_like(q)
        q2 = jnp.concatenate([jnp.where(first, q, zero), jnp.where(first, zero, q)], axis=0)

        def chunk(rows):
            s = lax.dot_general(k_all[rows, :], q2, dims, preferred_element_type=F32) * c
            s_scr[t % 2, rows, :] = s
            m = jnp.max(s, axis=0, keepdims=True)
            state[t]["m"] = m if state[t]["m"] is None else jnp.maximum(state[t]["m"], m)

        return [functools.partial(chunk, rows) for rows in chunks]

    def exps(t):
        def chunk(rows):
            e_scr[t % 2, rows, :] = jnp.exp2(s_scr[t % 2, rows, :] - state[t]["m"]).astype(BF16)

        return [functools.partial(chunk, rows) for rows in chunks]

    def values(t):
        ot = jnp.dot(vt_all[...], e_scr[t % 2], preferred_element_type=F32)
        ot = ot[:DA_V_DIM] / ot[DA_V_DIM:DA_V_DIM + 1]
        ot = ot[:, :tq] - lam * ot[:, tq:]
        ot = ot * lax.rsqrt(jnp.mean(ot * ot, axis=0, keepdims=True) + EPS) * g_ref[...] * (1.0 - lam_init)
        o_ref[0, t * tq:(t + 1) * tq, :] = jnp.transpose(ot).astype(o_ref.dtype)

    for t in range(n_tiles + 2):
        if 0 <= t - 2 < n_tiles:
            values(t - 2)
        _interleave(scores(t) if t < n_tiles else [],
                    (exps(t - 1) if 0 <= t - 1 < n_tiles else []))


def diff_attention(q_arr, srcs, lams, subln_g, lam_init):
    b, nq, _ = q_arr.shape
    nk = sum(a.shape[1] for a, _, _ in srcs)
    tq = min(ATT_TQ, nq)
    vec = pl.BlockSpec((1, DA_HEAD_DIM), lambda bi, h: (0, 0))
    k_specs = [pl.BlockSpec((1, a.shape[1], LANES), lambda bi, h, c0=kc: (bi, 0, c0 + h)) for a, kc, _ in srcs]
    v_specs = [pl.BlockSpec((1, a.shape[1], LANES), lambda bi, h, c0=vc: (bi, 0, c0 + h)) for a, _, vc in srcs]
    arrs = [a for a, _, _ in srcs]
    return pl.pallas_call(
        functools.partial(_diff_attn_kernel, n_src=len(srcs), lam_init=lam_init),
        grid=(b, DA_HEADS),
        in_specs=[pl.BlockSpec((1, nq, LANES), lambda bi, h: (bi, 0, h))] + k_specs + v_specs
                 + [vec, vec, vec, vec, pl.BlockSpec((DA_V_DIM, 1), lambda bi, h: (0, 0))],
        out_specs=pl.BlockSpec((1, nq, LANES), lambda bi, h: (bi, 0, h)),
        out_shape=jax.ShapeDtypeStruct((b, nq, DA_WIDTH), BF16),
        scratch_shapes=[pltpu.VMEM((nk, LANES), BF16), pltpu.VMEM((DA_VT_ROWS, nk), BF16),
                        pltpu.VMEM((2, nk, 2 * tq), F32), pltpu.VMEM((2, nk, 2 * tq), BF16)],
        compiler_params=_cparams(("parallel", "parallel"), VMEM_LIMIT),
        name="diff_attention",
    )(q_arr, *arrs, *arrs, *lams, subln_g)


def _da_perm():
    perm = np.zeros(LANES, np.int32)
    for l in range(LANES):
        part, within = divmod(l, 64)
        j, rem = divmod(within, 32)
        seg, i = divmod(rem, 16)
        perm[l] = j * 64 + seg * 32 + part * 16 + i
    return perm


def _rope_tables(n_tok):
    t = jnp.arange(n_tok, dtype=jnp.int32)
    n_freq = DA_HEAD_DIM // 4
    inv = ROPE_BASE ** (-jnp.arange(n_freq, dtype=F32) / n_freq)
    ang_r = (t // GRID_W).astype(F32)[:, None] * inv
    ang_c = (t % GRID_W).astype(F32)[:, None] * inv
    cos32 = jnp.concatenate([jnp.cos(ang_r), jnp.cos(ang_c)], axis=1)
    sin32 = jnp.concatenate([jnp.sin(ang_r), jnp.sin(ang_c)], axis=1)
    cos = jnp.tile(cos32, (1, 4))
    sin = jnp.concatenate([-sin32, -sin32, sin32, sin32], axis=1)
    return cos, sin


def _gmlp_kernel(x_ref, g_ref, sh_ref, sc_ref, win_ref, vg_ref, ws_ref, bs_ref, wout_ref, gate_ref,
                 mg_ref, sh2_ref, sc2_ref, wr_ref, o_ref, hx_ref, lg_ref, *, tm):
    x = x_ref[0]
    h = _rms(x) * g_ref[...]
    h = (h * (1.0 + sc_ref[0]) + sh_ref[0]).astype(BF16)
    uv = jax.nn.gelu(jnp.dot(h, win_ref[...], preferred_element_type=F32), approximate=True)
    u = uv[:, :GM_WIDTH]
    v = (_rms(uv[:, GM_WIDTH:]) * vg_ref[...]).astype(BF16)
    rows = []
    for c in range(tm // GM_CHUNK):
        cols = []
        for gi in range(GM_GROUPS):
            vv = v[c * GM_CHUNK:(c + 1) * GM_CHUNK, gi * GM_GROUP_DIM:(gi + 1) * GM_GROUP_DIM]
            sv = jnp.dot(ws_ref[gi], vv, preferred_element_type=F32) + bs_ref[:, gi:gi + 1]
            cols.append(sv)
        rows.append(jnp.concatenate(cols, axis=1))
    sv = jnp.concatenate(rows, axis=0) if len(rows) > 1 else rows[0]
    y = jnp.dot((u * sv).astype(BF16), wout_ref[...], preferred_element_type=F32)
    xn = x + gate_ref[0] * y
    o_ref[0] = xn
    _moe_prologue(xn, mg_ref, sh2_ref, sc2_ref, wr_ref, hx_ref, lg_ref)


def gmlp_layer(x, norm, w_in, v_g, w_s, b_s_t, w_out, gate, moe, tm=512):
    b, n, d = x.shape
    tm = min(tm, n)
    g, sh, sc = norm
    moe_in, moe_out = _moe_prologue_specs(d, tm, N_EXPERTS)
    full = lambda shape: pl.BlockSpec(shape, lambda bi, i: (0,) * len(shape))
    per_b = pl.BlockSpec((1, 1, d), lambda bi, i: (bi, 0, 0))
    return pl.pallas_call(
        functools.partial(_gmlp_kernel, tm=tm),
        grid=(b, n // tm),
        in_specs=[pl.BlockSpec((1, tm, d), lambda bi, i: (bi, i, 0)),
                  full((1, d)), per_b, per_b,
                  full(w_in.shape), full((1, GM_WIDTH)), full(w_s.shape), full(b_s_t.shape),
                  full(w_out.shape), per_b] + moe_in,
        out_specs=[pl.BlockSpec((1, tm, d), lambda bi, i: (bi, i, 0))] + moe_out,
        out_shape=[jax.ShapeDtypeStruct((b, n, d), F32), jax.ShapeDtypeStruct((b, n, d), BF16),
                   jax.ShapeDtypeStruct((b, N_EXPERTS, n), F32)],
        compiler_params=_cparams(("parallel", "arbitrary"), VMEM_LIMIT),
        name="gmlp_layer",
    )(x, g, sh, sc, w_in, v_g, w_s, b_s_t, w_out, gate, *moe)


NA_VT_ROWS = LANES + 16


def _natten_kernel(q_ref, k_ref, v_ref, kc_ref, vc_ref, bias_ref, o_ref, vt_all, vct, s_scr, e_scr, *, n_blocks):
    tq = NA_ROW_BLOCK * GRID_W
    slab_blocks = NA_SLAB_ROWS // NA_ROW_BLOCK
    slab = slab_blocks * tq
    lc = kc_ref.shape[1]
    for blk in range(n_blocks):
        vt_all[:LANES, blk * tq:(blk + 1) * tq] = jnp.transpose(
            v_ref[0, blk * tq:(blk + 1) * tq, :].astype(F32)).astype(BF16)
    vt_all[LANES:, :] = jnp.ones((NA_VT_ROWS - LANES, vt_all.shape[1]), BF16)
    vct[:LANES, :] = jnp.transpose(vc_ref[0].astype(F32)).astype(BF16)
    vct[LANES:, :] = jnp.ones((NA_VT_ROWS - LANES, lc), BF16)

    lane = lax.broadcasted_iota(jnp.int32, (1, LANES), 1)
    left = lane < NA_HEAD_DIM
    dims = (((1,), (1,)), ((), ()))
    c = NA_HEAD_DIM ** -0.5 * math.log2(math.e)
    state = [dict(m=None) for _ in range(n_blocks)]
    key0 = [min(max(t - 1, 0), n_blocks - slab_blocks) * tq for t in range(n_blocks)]
    kind = [0 if t == 0 else 2 if t == n_blocks - 1 else 1 for t in range(n_blocks)]

    def scores(t):
        q = q_ref[0, t * tq:(t + 1) * tq, :]
        zero = jnp.zeros_like(q)
        q2 = jnp.concatenate([jnp.where(left, q, zero), jnp.where(left, zero, q)], axis=0)

        def local():
            k = k_ref[0, key0[t]:key0[t] + slab, :]
            s = lax.dot_general(k, q2, dims, preferred_element_type=F32) * c + bias_ref[0, kind[t]]
            s_scr[t % 2, :slab, :] = s
            state[t]["m"] = jnp.max(s, axis=0, keepdims=True)

        def context():
            s = lax.dot_general(kc_ref[0], q2, dims, preferred_element_type=F32) * c
            s_scr[t % 2, slab:, :] = s
            state[t]["m"] = jnp.maximum(state[t]["m"], jnp.max(s, axis=0, keepdims=True))

        return [local, context]

    def exps(t):
        def chunk(rows):
            e_scr[t % 2, rows, :] = jnp.exp2(s_scr[t % 2, rows, :] - state[t]["m"]).astype(BF16)

        return [functools.partial(chunk, slice(0, slab)), functools.partial(chunk, slice(slab, slab + lc))]

    def values(t):
        vt = jnp.concatenate([vt_all[:, key0[t]:key0[t] + slab], vct[...]], axis=1)
        ot = jnp.dot(vt, e_scr[t % 2], preferred_element_type=F32)
        ot = ot[:LANES] / ot[LANES:LANES + 1]
        pair = jnp.concatenate([ot[:NA_HEAD_DIM, :tq], ot[NA_HEAD_DIM:, tq:]], axis=0)
        o_ref[0, t * tq:(t + 1) * tq, :] = jnp.transpose(pair).astype(o_ref.dtype)

    for t in range(n_blocks + 2):
        if 0 <= t - 2 < n_blocks:
            values(t - 2)
        _interleave(scores(t) if t < n_blocks else [],
                    (exps(t - 1) if 0 <= t - 1 < n_blocks else []))


def _natten_bias(rpb, rows):
    n_heads = rpb.shape[0]
    n_blocks = rows // NA_ROW_BLOCK
    n_dr, n_dc = 2 * NA_WIN_R - 1, 2 * NA_WIN_C - 1
    rpb = rpb * math.log2(math.e)
    span = 2 * GRID_W - 1
    lo = GRID_W - NA_WIN_C - 1
    u = jnp.pad(rpb, ((0, 0), (0, 0), (lo, span - n_dc - lo)))
    skew = jnp.tile(u, (1, 1, GRID_W))[:, :, :GRID_W * (span - 1)].reshape(n_heads, n_dr, GRID_W, span - 1)
    toep = skew[:, :, :, GRID_W - 2:2 * GRID_W - 2]
    pad = NA_ROW_BLOCK
    toep = jnp.pad(toep, ((0, 0), (pad, pad), (0, 0), (0, 0)))
    i = np.arange(NA_ROW_BLOCK)[:, None, None, None]
    c = np.arange(GRID_W)[None, :, None, None]
    m = np.arange(NA_SLAB_ROWS)[None, None, :, None]
    kc = np.arange(GRID_W)[None, None, None, :]
    win_c0 = np.clip(c - NA_WIN_C // 2, 0, GRID_W - NA_WIN_C)
    col_ok = (kc >= win_c0) & (kc < win_c0 + NA_WIN_C)
    tiles, ok = [], []
    for rb in (0, 1, n_blocks - 1):
        slab0 = int(np.clip(rb - 1, 0, n_blocks - 3)) * NA_ROW_BLOCK
        r = rb * NA_ROW_BLOCK + i
        r0 = np.clip(r - NA_WIN_R // 2, 0, rows - NA_WIN_R)
        kr = slab0 + m
        ok.append(np.broadcast_to((kr >= r0) & (kr < r0 + NA_WIN_R) & col_ok,
                                  (NA_ROW_BLOCK, GRID_W, NA_SLAB_ROWS, GRID_W)))
        for ii in range(NA_ROW_BLOCK):
            off = slab0 - (rb * NA_ROW_BLOCK + ii) + NA_WIN_R - 1 + pad
            tiles.append(toep[:, off:off + NA_SLAB_ROWS])
    tq, slab = NA_ROW_BLOCK * GRID_W, NA_SLAB_ROWS * GRID_W
    bias = jnp.stack(tiles, axis=1).reshape(n_heads // 2, 2, 3, NA_ROW_BLOCK, NA_SLAB_ROWS, GRID_W, GRID_W)
    bias = jnp.transpose(bias, (0, 2, 4, 6, 1, 3, 5)).reshape(n_heads // 2, 3, slab, 2 * tq)
    ok = np.stack(ok).transpose(0, 3, 4, 1, 2).reshape(3, slab, tq)
    ok = np.concatenate([ok, ok], axis=2)
    return jnp.where(ok[None], bias, NEG_INF)


def natten(qkv, kv_ctx, bias):
    b, n, _ = qkv.shape
    lc = kv_ctx.shape[1]
    tq = NA_ROW_BLOCK * GRID_W
    n_blocks = n // tq
    hp = NA_HEADS // 2
    return pl.pallas_call(
        functools.partial(_natten_kernel, n_blocks=n_blocks),
        grid=(hp, b),
        in_specs=[pl.BlockSpec((1, n, LANES), lambda p, bi: (bi, 0, p)),
                  pl.BlockSpec((1, n, LANES), lambda p, bi: (bi, 0, hp + p)),
                  pl.BlockSpec((1, n, LANES), lambda p, bi: (bi, 0, 2 * hp + p)),
                  pl.BlockSpec((1, lc, LANES), lambda p, bi: (bi, 0, p)),
                  pl.BlockSpec((1, lc, LANES), lambda p, bi: (bi, 0, hp + p)),
                  pl.BlockSpec((1, 3, NA_SLAB_ROWS * GRID_W, 2 * tq), lambda p, bi: (p, 0, 0, 0))],
        out_specs=pl.BlockSpec((1, n, LANES), lambda p, bi: (bi, 0, p)),
        out_shape=jax.ShapeDtypeStruct((b, n, NA_WIDTH), BF16),
        scratch_shapes=[pltpu.VMEM((NA_VT_ROWS, n), BF16), pltpu.VMEM((NA_VT_ROWS, lc), BF16),
                        pltpu.VMEM((2, NA_SLAB_ROWS * GRID_W + lc, 2 * tq), F32),
                        pltpu.VMEM((2, NA_SLAB_ROWS * GRID_W + lc, 2 * tq), BF16)],
        compiler_params=_cparams(("parallel", "parallel"), VMEM_LIMIT),
        name="natten",
    )(qkv, qkv, qkv, kv_ctx, kv_ctx, bias)


def _dft_tables(n, sign=1.0):
    k = jnp.arange(n, dtype=jnp.int32)
    ang = ((k[:, None] * k[None, :]) % n).astype(F32) * (2.0 * math.pi / n)
    return jnp.cos(ang), sign * jnp.sin(ang)


def _fourier_chan_kernel(x_ref, g_ref, sh_ref, sc_ref, wc_ref, o_ref):
    h = _rms(x_ref[0]) * g_ref[...]
    h = (h * (1.0 + sc_ref[0]) + sh_ref[0]).astype(BF16)
    gd = D_MODEL // FN_GROUPS
    for gi in range(FN_GROUPS):
        z = jnp.dot(h[:, gi * gd:(gi + 1) * gd], wc_ref[...], preferred_element_type=F32).astype(BF16)
        o_ref[0, 0, :, gi * gd:(gi + 1) * gd] = z[:, :gd]
        o_ref[0, 1, :, gi * gd:(gi + 1) * gd] = z[:, gd:]


def _fourier_pos_kernel(wp_ref, z_ref, wout_ref, x_ref, gate_ref, mg_ref, sh2_ref, sc2_ref, wr_ref,
                        o_ref, hx_ref, lg_ref, *, scale):
    f = jnp.dot(wp_ref[...], z_ref[0], preferred_element_type=F32) * scale
    y = jnp.dot(f.astype(BF16), wout_ref[...], preferred_element_type=F32)
    xn = x_ref[0] + gate_ref[0] * y
    o_ref[0] = xn
    _moe_prologue(xn, mg_ref, sh2_ref, sc2_ref, wr_ref, hx_ref, lg_ref)


def fourier_layer(x, norm, w_out, gate, moe, tm=512):
    b, n, d = x.shape
    moe_in, moe_out = _moe_prologue_specs(d, tm, N_EXPERTS)
    gd = d // FN_GROUPS
    g, sh, sc = norm
    cc, sc_tab = _dft_tables(gd)
    wc = jnp.concatenate([cc, sc_tab], axis=1).astype(BF16)
    cn, sn = _dft_tables(n, -1.0)
    wp = jnp.concatenate([cn, sn], axis=1).astype(BF16)
    per_b = pl.BlockSpec((1, 1, d), lambda bi, i: (bi, 0, 0))
    z = pl.pallas_call(
        _fourier_chan_kernel,
        grid=(b, n // tm),
        in_specs=[pl.BlockSpec((1, tm, d), lambda bi, i: (bi, i, 0)),
                  pl.BlockSpec((1, d), lambda bi, i: (0, 0)), per_b, per_b,
                  pl.BlockSpec((gd, 2 * gd), lambda bi, i: (0, 0))],
        out_specs=pl.BlockSpec((1, 2, tm, d), lambda bi, i: (bi, 0, i, 0)),
        out_shape=jax.ShapeDtypeStruct((b, 2, n, d), BF16),
        compiler_params=_cparams(("parallel", "parallel")),
        name="fourier_chan",
    )(x, g, sh, sc, wc)
    z = z.reshape(b, 2 * n, d)
    return pl.pallas_call(
        functools.partial(_fourier_pos_kernel, scale=1.0 / math.sqrt(n * gd)),
        grid=(b, n // tm),
        in_specs=[pl.BlockSpec((tm, 2 * n), lambda bi, i: (i, 0)),
                  pl.BlockSpec((1, 2 * n, d), lambda bi, i: (bi, 0, 0)),
                  pl.BlockSpec((d, d), lambda bi, i: (0, 0)),
                  pl.BlockSpec((1, tm, d), lambda bi, i: (bi, i, 0)), per_b] + moe_in,
        out_specs=[pl.BlockSpec((1, tm, d), lambda bi, i: (bi, i, 0))] + moe_out,
        out_shape=[jax.ShapeDtypeStruct((b, n, d), F32), jax.ShapeDtypeStruct((b, n, d), BF16),
                   jax.ShapeDtypeStruct((b, N_EXPERTS, n), F32)],
        compiler_params=_cparams(("parallel", "arbitrary"), VMEM_LIMIT),
        name="fourier_pos",
    )(wp, z, w_out, x, gate, *moe)


def _excl_cumsum_lanes(mask, tri):
    e, n = mask.shape
    mf = jnp.where(mask, 1.0, 0.0)
    offset = jnp.zeros((e, 1), F32)
    parts = []
    for blk in range(n // LANES):
        part = mf[:, blk * LANES:(blk + 1) * LANES]
        parts.append(jnp.dot(part.astype(BF16), tri, preferred_element_type=F32) + offset)
        offset = offset + jnp.sum(part, axis=1, keepdims=True)
    return jnp.concatenate(parts, axis=1)


def _moe_route_kernel(lg_ref, tri_ref, aff_ref, slot_ref, *, cap):
    lg = lg_ref[...]
    e = jnp.exp(lg - jnp.max(lg, axis=1, keepdims=True))
    aff3 = e / jnp.sum(e, axis=1, keepdims=True)
    aff_ref[...] = aff3
    b, n_exp, n = lg.shape
    aff = aff3.reshape(b * n_exp, n)

    def as_float(bits):
        return lax.bitcast_convert_type(bits, F32)

    def step(it, thr):
        cand = thr | jnp.left_shift(jnp.int32(1), 30 - it)
        cnt = jnp.sum((aff >= as_float(cand)).astype(jnp.int32), axis=1, keepdims=True)
        return jnp.where(cnt >= cap, cand, thr)

    thr = lax.fori_loop(0, 31, step, jnp.zeros((b * n_exp, 1), jnp.int32))
    gt = aff >= as_float(thr + 1)
    eq = (aff >= as_float(thr)) & jnp.logical_not(gt)
    need = (cap - jnp.sum(gt.astype(jnp.int32), axis=1, keepdims=True)).astype(F32)
    tri = tri_ref[...]
    sel = gt | (eq & (_excl_cumsum_lanes(eq, tri) < need))
    pos = _excl_cumsum_lanes(sel, tri).astype(jnp.int32)
    slot_ref[...] = jnp.where(sel, pos, -1).reshape(b, n_exp, n)


def _moe_gather_kernel(slot_ref, aff_ref, hx_ref, xs_ref, ta_ref, *, cap, ge):
    e0 = pl.multiple_of(pl.program_id(1) * ge, ge)
    n = slot_ref.shape[2]
    rows = lax.broadcasted_iota(jnp.int32, (cap, n), 0)
    hits, tas = [], []
    for k in range(ge):
        hit = rows == slot_ref[0, pl.ds(e0 + k, 1), :]
        hits.append(jnp.where(hit, 1.0, 0.0).astype(BF16))
        tas.append(jnp.sum(jnp.where(hit, aff_ref[0, pl.ds(e0 + k, 1), :], 0.0), axis=1, keepdims=True))
    onehot = jnp.concatenate(hits, axis=0) if ge > 1 else hits[0]
    xs = jnp.dot(onehot, hx_ref[0], preferred_element_type=F32).astype(BF16)
    xs_ref[0] = xs.reshape(ge, cap, xs.shape[-1])
    ta = jnp.concatenate(tas, axis=0) if ge > 1 else tas[0]
    ta_ref[0] = jnp.broadcast_to(ta, (ge * cap, LANES)).reshape(ge, cap, LANES)


def _moe_ffn_kernel(xs_ref, ta_ref, wg_ref, wu_ref, wd_ref, ys_ref, wgb, wub, wdb):
    @pl.when(pl.program_id(1) == 0)
    def _():
        wgb[...] = wg_ref[0, 0].astype(BF16)
        wub[...] = wu_ref[0, 0].astype(BF16)
        wdb[...] = wd_ref[0, 0].astype(BF16)

    bb, _, cap, d = xs_ref.shape
    xs = xs_ref[...].reshape(bb * cap, d)
    ta = ta_ref[...].reshape(bb * cap, LANES)[:, :1]
    gate = jnp.dot(xs, wgb[...], preferred_element_type=F32)
    up = jnp.dot(xs, wub[...], preferred_element_type=F32)
    hid = (gate * jax.nn.sigmoid(gate) * up).astype(BF16)
    y = jnp.dot(hid, wdb[...], preferred_element_type=F32) * ta
    ys_ref[...] = y.astype(BF16).reshape(bb, 1, cap, d)


def _moe_scatter_kernel(slot_ref, ys_ref, x_ref, gate_ref, *rest, cap, final):
    if final:
        fg_ref, o_ref = rest
    else:
        (o_ref,) = rest
    slot = slot_ref[0]
    n_exp = slot.shape[1]
    if cap % LANES == 0:
        want = lax.broadcasted_iota(jnp.int32, (1, cap), 1)
        hit = jnp.concatenate([jnp.where(slot[:, e:e + 1] == want, 1.0, 0.0).astype(BF16) for e in range(n_exp)],
                              axis=1)
    else:
        col = lax.broadcasted_iota(jnp.int32, (n_exp, n_exp * cap), 1)
        row = lax.broadcasted_iota(jnp.int32, (n_exp, n_exp * cap), 0)
        spread = jnp.where(col // cap == row, 1.0, 0.0).astype(BF16)
        want = (lax.broadcasted_iota(jnp.int32, (1, n_exp * cap), 1) % cap).astype(F32)
        ids = jnp.dot(slot.astype(F32).astype(BF16), spread, preferred_element_type=F32)
        hit = jnp.where(ids == want, 1.0, 0.0).astype(BF16)
    ys = ys_ref[0].reshape(n_exp * cap, ys_ref.shape[-1])
    out = x_ref[0] + gate_ref[0] * jnp.dot(hit, ys, preferred_element_type=F32)
    if final:
        out = _rms(out) * fg_ref[...]
    o_ref[0] = out


def moe_layer(x, hx, logits, layer, w_gate, w_up, w_down, gate, final_g=None):
    b, n, d = x.shape
    n_exp = logits.shape[1]
    cap = EC_CAPACITY_FACTOR * n // n_exp
    per_b2 = pl.BlockSpec((1, 1, d), lambda bi, i: (bi, 0, 0))
    tri = (np.arange(LANES)[:, None] < np.arange(LANES)[None, :]).astype(np.float32)
    aff, slot = pl.pallas_call(
        functools.partial(_moe_route_kernel, cap=cap),
        grid=(1,),
        in_specs=[pl.BlockSpec((b, n_exp, n), lambda i: (0, 0, 0)),
                  pl.BlockSpec((LANES, LANES), lambda i: (0, 0))],
        out_specs=[pl.BlockSpec((b, n_exp, n), lambda i: (0, 0, 0))] * 2,
        out_shape=[jax.ShapeDtypeStruct((b, n_exp, n), F32), jax.ShapeDtypeStruct((b, n_exp, n), jnp.int32)],
        compiler_params=_cparams(("arbitrary",)),
        name="moe_route",
    )(logits, jnp.asarray(tri, BF16))

    rows_per_step = 512
    ge = max(1, min(n_exp, rows_per_step // cap))
    xs, ta = pl.pallas_call(
        functools.partial(_moe_gather_kernel, cap=cap, ge=ge),
        grid=(b, n_exp // ge),
        in_specs=[pl.BlockSpec((1, n_exp, n), lambda bi, e: (bi, 0, 0)),
                  pl.BlockSpec((1, n_exp, n), lambda bi, e: (bi, 0, 0)),
                  pl.BlockSpec((1, n, d), lambda bi, e: (bi, 0, 0))],
        out_specs=[pl.BlockSpec((1, ge, cap, d), lambda bi, e: (bi, e, 0, 0)),
                   pl.BlockSpec((1, ge, cap, LANES), lambda bi, e: (bi, e, 0, 0))],
        out_shape=[jax.ShapeDtypeStruct((b, n_exp, cap, d), BF16),
                   jax.ShapeDtypeStruct((b, n_exp, cap, LANES), F32)],
        compiler_params=_cparams(("parallel", "arbitrary"), VMEM_LIMIT),
        name="moe_gather",
    )(slot, aff, hx)

    bb = max(1, min(b, rows_per_step // cap))
    f = w_gate.shape[-1]
    ys = pl.pallas_call(
        _moe_ffn_kernel,
        grid=(n_exp, b // bb),
        in_specs=[pl.BlockSpec((bb, 1, cap, d), lambda e, bi: (bi, e, 0, 0)),
                  pl.BlockSpec((bb, 1, cap, LANES), lambda e, bi: (bi, e, 0, 0)),
                  pl.BlockSpec((1, 1, d, f), lambda e, bi: (layer, e, 0, 0)),
                  pl.BlockSpec((1, 1, d, f), lambda e, bi: (layer, e, 0, 0)),
                  pl.BlockSpec((1, 1, f, d), lambda e, bi: (layer, e, 0, 0))],
        out_specs=pl.BlockSpec((bb, 1, cap, d), lambda e, bi: (bi, e, 0, 0)),
        out_shape=jax.ShapeDtypeStruct((b, n_exp, cap, d), BF16),
        scratch_shapes=[pltpu.VMEM((d, f), BF16), pltpu.VMEM((d, f), BF16), pltpu.VMEM((f, d), BF16)],
        compiler_params=_cparams(("parallel", "arbitrary"), VMEM_LIMIT),
        name="moe_ffn",
    )(xs, ta, w_gate, w_up, w_down)

    slot_t = jnp.swapaxes(slot, 1, 2)
    tn = min(512, n)
    args = [slot_t, ys, x, gate]
    specs = [pl.BlockSpec((1, tn, n_exp), lambda bi, i: (bi, i, 0)),
             pl.BlockSpec((1, n_exp, cap, d), lambda bi, i: (bi, 0, 0, 0)),
             pl.BlockSpec((1, tn, d), lambda bi, i: (bi, i, 0)), per_b2]
    if final_g is not None:
        args.append(final_g)
        specs.append(pl.BlockSpec((1, d), lambda bi, i: (0, 0)))
    return pl.pallas_call(
        functools.partial(_moe_scatter_kernel, cap=cap, final=final_g is not None),
        grid=(b, n // tn),
        in_specs=specs,
        out_specs=pl.BlockSpec((1, tn, d), lambda bi, i: (bi, i, 0)),
        out_shape=jax.ShapeDtypeStruct((b, n, d), F32),
        compiler_params=_cparams(("parallel", "arbitrary"), VMEM_LIMIT),
        name="moe_scatter",
    )(*args)


def kernel(x, c, ctx, c_ctx, ada_w, ada_b, mixer_norm_g, moe_norm_g, router_w, moe_w_gate, moe_w_up, moe_w_down, da_w_in, da_lambda_q1, da_lambda_k1, da_lambda_q2, da_lambda_k2, da_subln_g, da_w_out, gm_w_in, gm_v_g, gm_w_s, gm_b_s, gm_w_out, na_w_in, na_rpb, na_w_out, fn_w_out, final_norm_g):
    b, n, d = x.shape
    rows = n // GRID_W
    readers = [i for i in range(DEPTH) if i % N_MIXERS in CTX_READERS]
    last_reader = max(readers) if readers else -1

    r_pad = -(-(b + 1) // 8) * 8
    cond = jnp.zeros((r_pad, d), F32).at[:b].set(c).at[b].set(c_ctx)
    mods = ada_params(cond, ada_w, ada_b)

    def lat_mod(i, k):
        return mods[i, :b, k * d:(k + 1) * d].reshape(b, 1, d)

    def ctx_mod(i, k):
        return jnp.broadcast_to(mods[i, b, k * d:(k + 1) * d].reshape(1, 1, d), (b, 1, d))

    rope = _rope_tables(n)
    perm = _da_perm()

    for i in range(DEPTH):
        kind, j = i % N_MIXERS, i // N_MIXERS
        need_ctx = i <= last_reader
        update_ctx = i < last_reader
        mg = mixer_norm_g[i].reshape(1, d)
        xnorm = (mg, lat_mod(i, 0), lat_mod(i, 1))
        cnorm = (mg, ctx_mod(i, 0), ctx_mod(i, 1)) if need_ctx else None
        g1 = lat_mod(i, 2)
        wr = _router_split(router_w[i])
        moe_g = moe_norm_g[i].reshape(1, d)
        xmoe = (moe_g, lat_mod(i, 3), lat_mod(i, 4), wr)
        cmoe = (moe_g, ctx_mod(i, 3), ctx_mod(i, 4), wr) if update_ctx else None

        if kind == MIX_DIFF:
            lam_init = 0.8 - 0.6 * math.exp(-0.3 * i)
            cols = np.arange(2 * DA_WIDTH).reshape(2 * DA_HEADS, LANES)[:, perm].reshape(-1)
            w_in = jnp.concatenate([da_w_in[j][:, cols], da_w_in[j][:, 2 * DA_WIDTH:]], axis=1).astype(BF16)
            w_out = da_w_out[j].astype(BF16)
            lams = [t[j].reshape(1, DA_HEAD_DIM) for t in (da_lambda_q1, da_lambda_k1, da_lambda_q2, da_lambda_k2)]
            sg = da_subln_g[j].reshape(DA_V_DIM, 1)
            nb = DA_WIDTH // LANES
            qkv = linear(x, w_in, norm=xnorm, rope=rope, rope_tiles=2)
            if update_ctx:
                qkv_c = linear(ctx, w_in, norm=cnorm)
                ctx_src = (qkv_c, nb, 2 * nb)
            else:
                ctx_src = (linear(ctx, w_in[:, DA_WIDTH:], norm=cnorm), 0, nb)
            o = diff_attention(qkv, [ctx_src, (qkv, nb, 2 * nb)], lams, sg, lam_init)
            x, hx, logits = linear(o, w_out, res=(x, g1), moe=xmoe, out_dtype=F32)
            if update_ctx:
                oc = diff_attention(qkv_c, [ctx_src], lams, sg, lam_init)
                ctx, hc, logits_c = linear(oc, w_out, res=(ctx, ctx_mod(i, 2)), moe=cmoe, out_dtype=F32)
        elif kind == MIX_GMLP:
            w_in = gm_w_in[j].astype(BF16)
            w_out = gm_w_out[j].astype(BF16)
            vg = gm_v_g[j].reshape(1, GM_WIDTH)
            w_s = gm_w_s[j].astype(BF16)
            b_s_t = jnp.transpose(gm_b_s[j])
            x, hx, logits = gmlp_layer(x, xnorm, w_in, vg, w_s, b_s_t, w_out, g1, xmoe)
            if update_ctx:
                ctx, hc, logits_c = gmlp_layer(ctx, cnorm, w_in, vg, w_s, b_s_t, w_out, ctx_mod(i, 2), cmoe)
        elif kind == MIX_NATTEN:
            w_in = na_w_in[j].astype(BF16)
            w_out = na_w_out[j].astype(BF16)
            qkv = linear(x, w_in, norm=xnorm)
            kv_c = linear(ctx, w_in[:, NA_WIDTH:], norm=cnorm)
            o = natten(qkv, kv_c, _natten_bias(na_rpb[j], rows))
            x, hx, logits = linear(o, w_out, res=(x, g1), moe=xmoe, out_dtype=F32)
            if update_ctx:
                raise NotImplementedError("context update after a neighbourhood layer")
        else:
            x, hx, logits = fourier_layer(x, xnorm, fn_w_out[j].astype(BF16), g1, xmoe)
            if update_ctx:
                ctx, hc, logits_c = fourier_layer(ctx, cnorm, fn_w_out[j].astype(BF16), ctx_mod(i, 2), cmoe)

        last = i == DEPTH - 1
        x = moe_layer(x, hx, logits, i, moe_w_gate, moe_w_up, moe_w_down, lat_mod(i, 5),
                      final_g=final_norm_g.reshape(1, d) if last else None)
        if update_ctx:
            ctx = moe_layer(ctx, hc, logits_c, i, moe_w_gate, moe_w_up, moe_w_down, ctx_mod(i, 5))
    return x
```

```python
import functools
import math

import numpy as np
import jax
import jax.numpy as jnp
from jax import lax
from jax.experimental import pallas as pl
from jax.experimental.pallas import tpu as pltpu

F32 = jnp.float32
BF16 = jnp.bfloat16

D_MODEL = 1024
DEPTH = 4
GRID_W = 64
N_MIXERS = 4
MIX_DIFF, MIX_GMLP, MIX_NATTEN, MIX_FOURIER = 0, 1, 2, 3
CTX_READERS = (MIX_DIFF, MIX_NATTEN)
EPS = 1e-6
NEG_INF = -1e30
ROPE_BASE = 10000.0

DA_HEADS = 8
DA_HEAD_DIM = 64
DA_V_DIM = 2 * DA_HEAD_DIM
DA_WIDTH = DA_HEADS * DA_V_DIM

GM_CHUNK = 128
GM_GROUPS = 8
GM_WIDTH = 2 * D_MODEL
GM_GROUP_DIM = GM_WIDTH // GM_GROUPS

NA_HEADS = 16
NA_HEAD_DIM = D_MODEL // NA_HEADS
NA_WIDTH = NA_HEADS * NA_HEAD_DIM
NA_WIN_R = 8
NA_WIN_C = 16
NA_ROW_BLOCK = 4
NA_SLAB_ROWS = NA_ROW_BLOCK + NA_WIN_R

FN_GROUPS = 4
DFT_SPLIT = 256

N_EXPERTS = 16
EC_CAPACITY_FACTOR = 2

LANES = 128
VMEM_LIMIT = 56 * 1024 * 1024


def _cparams(sem, vmem=None):
    return pltpu.CompilerParams(dimension_semantics=sem, vmem_limit_bytes=vmem)


def _rms(x):
    return x * lax.rsqrt(jnp.mean(x * x, axis=-1, keepdims=True) + EPS)


def _ada_kernel(c_ref, w_ref, b_ref, o_ref):
    c = c_ref[...]
    h = (c * jax.nn.sigmoid(c)).astype(BF16)
    o_ref[0] = jnp.dot(h, w_ref[0].astype(BF16), preferred_element_type=F32) + b_ref[0]


def ada_params(cond, ada_w, ada_b):
    r = cond.shape[0]
    depth, d, n_out = ada_w.shape
    tn = 1024
    return pl.pallas_call(
        _ada_kernel,
        grid=(depth, n_out // tn),
        in_specs=[pl.BlockSpec((r, d), lambda l, j: (0, 0)),
                  pl.BlockSpec((1, d, tn), lambda l, j: (l, 0, j)),
                  pl.BlockSpec((1, 1, tn), lambda l, j: (l, 0, j))],
        out_specs=pl.BlockSpec((1, r, tn), lambda l, j: (l, 0, j)),
        out_shape=jax.ShapeDtypeStruct((depth, r, n_out), F32),
        compiler_params=_cparams(("parallel", "parallel")),
        name="ada_params",
    )(cond, ada_w, ada_b.reshape(depth, 1, n_out))


def _moe_prologue(xn, mg_ref, sh_ref, sc_ref, wr_ref, hx_ref, lg_ref):
    h = _rms(xn) * mg_ref[...]
    h = h * (1.0 + sc_ref[0]) + sh_ref[0]
    h_hi = h.astype(BF16)
    h_lo = (h - h_hi.astype(F32)).astype(BF16)
    hx_ref[0] = h_hi
    a = jnp.dot(h_hi, wr_ref[...], preferred_element_type=F32)
    b = jnp.dot(h_lo, wr_ref[:, :LANES], preferred_element_type=F32)
    lt = a[:, :LANES] + a[:, LANES:] + b
    lg_ref[0] = jnp.transpose(lt)[:lg_ref.shape[1]]


def _router_split(w_router):
    w = jnp.pad(w_router, ((0, 0), (0, LANES - w_router.shape[1])))
    hi = w.astype(BF16)
    return jnp.concatenate([hi, (w - hi.astype(F32)).astype(BF16)], axis=1)


def _moe_prologue_specs(d, tm, n_exp):
    in_specs = [pl.BlockSpec((1, d), lambda bi, i: (0, 0)),
                pl.BlockSpec((1, 1, d), lambda bi, i: (bi, 0, 0)),
                pl.BlockSpec((1, 1, d), lambda bi, i: (bi, 0, 0)),
                pl.BlockSpec((d, 2 * LANES), lambda bi, i: (0, 0))]
    out_specs = [pl.BlockSpec((1, tm, d), lambda bi, i: (bi, i, 0)),
                 pl.BlockSpec((1, n_exp, tm), lambda bi, i: (bi, 0, i))]
    return in_specs, out_specs


def _linear_kernel(*refs, has_norm, has_res, has_moe, rope_tiles, tn):
    it = iter(refs)
    x_ref = next(it)
    if has_norm:
        g_ref, sh_ref, sc_ref = next(it), next(it), next(it)
    w_ref = next(it)
    if rope_tiles:
        cos_ref, sin_ref = next(it), next(it)
    if has_res:
        res_ref, gate_ref = next(it), next(it)
    if has_moe:
        moe_in = [next(it) for _ in range(4)]
    o_ref = next(it)

    if has_norm:
        h = _rms(x_ref[0]) * g_ref[...]
        hb = (h * (1.0 + sc_ref[0]) + sh_ref[0]).astype(BF16)
    else:
        hb = x_ref[0].astype(BF16)
    for j in range(w_ref.shape[1] // tn):
        cols = slice(j * tn, (j + 1) * tn)
        y = jnp.dot(hb, w_ref[:, cols], preferred_element_type=F32)
        if j < rope_tiles:
            cos, sin = cos_ref[...], sin_ref[...]
            for s in range(tn // LANES):
                seg = y[:, s * LANES:(s + 1) * LANES]
                rot = pltpu.roll(seg, LANES // 2, axis=1)
                lanes = slice(j * tn + s * LANES, j * tn + (s + 1) * LANES)
                o_ref[0, :, lanes] = (seg * cos + rot * sin).astype(o_ref.dtype)
        elif has_res:
            o_ref[0, :, cols] = (res_ref[0, :, cols] + gate_ref[0, :, cols] * y).astype(o_ref.dtype)
        else:
            o_ref[0, :, cols] = y.astype(o_ref.dtype)
    if has_moe:
        _moe_prologue(o_ref[0], *moe_in, next(it), next(it))


def linear(x, w, *, norm=None, res=None, rope=None, rope_tiles=0, moe=None, out_dtype=BF16, tm=512, tn=1024):
    b, n, k = x.shape
    m = w.shape[1]
    tm = min(tm, n)
    tn = min(tn, m)
    args, specs = [x], [pl.BlockSpec((1, tm, k), lambda bi, i: (bi, i, 0))]
    if norm is not None:
        g, sh, sc = norm
        args += [g, sh, sc]
        specs += [pl.BlockSpec((1, k), lambda bi, i: (0, 0)),
                  pl.BlockSpec((1, 1, k), lambda bi, i: (bi, 0, 0)),
                  pl.BlockSpec((1, 1, k), lambda bi, i: (bi, 0, 0))]
    args.append(w)
    specs.append(pl.BlockSpec((k, m), lambda bi, i: (0, 0)))
    if rope is not None:
        args += list(rope)
        specs += [pl.BlockSpec((tm, LANES), lambda bi, i: (i, 0))] * 2
    if res is not None:
        r, gate = res
        args += [r, gate]
        specs += [pl.BlockSpec((1, tm, m), lambda bi, i: (bi, i, 0)),
                  pl.BlockSpec((1, 1, m), lambda bi, i: (bi, 0, 0))]
    out_specs = [pl.BlockSpec((1, tm, m), lambda bi, i: (bi, i, 0))]
    out_shape = [jax.ShapeDtypeStruct((b, n, m), out_dtype)]
    if moe is not None:
        moe_in, moe_out = _moe_prologue_specs(m, tm, N_EXPERTS)
        args += list(moe)
        specs += moe_in
        out_specs += moe_out
        out_shape += [jax.ShapeDtypeStruct((b, n, m), BF16), jax.ShapeDtypeStruct((b, N_EXPERTS, n), F32)]
    kern = functools.partial(_linear_kernel, has_norm=norm is not None, has_res=res is not None,
                             has_moe=moe is not None, rope_tiles=rope_tiles if rope is not None else 0, tn=tn)
    out = pl.pallas_call(
        kern,
        grid=(b, n // tm),
        in_specs=specs,
        out_specs=out_specs,
        out_shape=out_shape,
        compiler_params=_cparams(("parallel", "parallel"), VMEM_LIMIT),
        name="linear",
    )(*args)
    return out if moe is not None else out[0]


DA_VT_ROWS = DA_V_DIM + 16
ATT_TQ = 256
ATT_KC = 768


def _interleave(stage_a, stage_b):
    for i in range(max(len(stage_a), len(stage_b))):
        if i < len(stage_a):
            stage_a[i]()
        if i < len(stage_b):
            stage_b[i]()


def _diff_attn_kernel(*refs, n_src, lam_init):
    for hh in range(refs[0].shape[2] // LANES):
        _diff_attn_head(slice(hh * LANES, (hh + 1) * LANES), refs, n_src, lam_init)


def _diff_attn_head(hl, refs, n_src, lam_init):
    q_ref = refs[0]
    k_refs = refs[1:1 + n_src]
    v_refs = refs[1 + n_src:1 + 2 * n_src]
    lq1_ref, lk1_ref, lq2_ref, lk2_ref, g_ref, o_ref, k_all, vt_all, s_scr, e_scr = refs[1 + 2 * n_src:]

    row = 0
    for k_ref, v_ref in zip(k_refs, v_refs):
        n = k_ref.shape[1]
        k_all[row:row + n, :] = k_ref[0, :, hl]
        vt_all[:DA_V_DIM, row:row + n] = jnp.transpose(v_ref[0, :, hl].astype(F32)).astype(BF16)
        row += n
    vt_all[DA_V_DIM:, :] = jnp.ones((DA_VT_ROWS - DA_V_DIM, vt_all.shape[1]), BF16)

    lam = (jnp.exp(jnp.sum(lq1_ref[...] * lk1_ref[...], keepdims=True))
           - jnp.exp(jnp.sum(lq2_ref[...] * lk2_ref[...], keepdims=True)) + lam_init)
    lane = lax.broadcasted_iota(jnp.int32, (1, LANES), 1)
    first = (lane % DA_HEAD_DIM) < (DA_HEAD_DIM // 2)
    dims = (((1,), (1,)), ((), ()))
    c = DA_HEAD_DIM ** -0.5 * math.log2(math.e)
    nk = k_all.shape[0]
    tq = min(ATT_TQ, q_ref.shape[1])
    n_tiles = q_ref.shape[1] // tq
    chunks = [slice(r, min(r + ATT_KC, nk)) for r in range(0, nk, ATT_KC)]
    state = [dict(m=None) for _ in range(n_tiles)]

    def scores(t):
        q = q_ref[0, t * tq:(t + 1) * tq, hl]
        zero = jnp.zeros_like(q)
        q2 = jnp.concatenate([jnp.where(first, q, zero), jnp.where(first, zero, q)], axis=0)

        def chunk(rows):
            s = lax.dot_general(k_all[rows, :], q2, dims, preferred_element_type=F32) * c
            s_scr[t % 2, rows, :] = s
            m = jnp.max(s, axis=0, keepdims=True)
            state[t]["m"] = m if state[t]["m"] is None else jnp.maximum(state[t]["m"], m)

        return [functools.partial(chunk, rows) for rows in chunks]

    def exps(t):
        def chunk(rows):
            e_scr[t % 2, rows, :] = jnp.exp2(s_scr[t % 2, rows, :] - state[t]["m"]).astype(BF16)

        return [functools.partial(chunk, rows) for rows in chunks]

    def values(t):
        ot = jnp.dot(vt_all[...], e_scr[t % 2], preferred_element_type=F32)
        ot = ot[:DA_V_DIM] / ot[DA_V_DIM:DA_V_DIM + 1]
        ot = ot[:, :tq] - lam * ot[:, tq:]
        ot = ot * lax.rsqrt(jnp.mean(ot * ot, axis=0, keepdims=True) + EPS) * g_ref[...] * (1.0 - lam_init)
        o_ref[0, t * tq:(t + 1) * tq, hl] = jnp.transpose(ot).astype(o_ref.dtype)

    for t in range(n_tiles + 2):
        if 0 <= t - 2 < n_tiles:
            values(t - 2)
        _interleave(scores(t) if t < n_tiles else [],
                    (exps(t - 1) if 0 <= t - 1 < n_tiles else []))


def diff_attention(q_arr, srcs, lams, subln_g, lam_init, heads_per_step=1):
    b, nq, _ = q_arr.shape
    nk = sum(a.shape[1] for a, _, _ in srcs)
    tq = min(ATT_TQ, nq)
    hps = heads_per_step
    w = hps * LANES
    assert all(kc % hps == 0 and vc % hps == 0 for _, kc, vc in srcs)
    vec = pl.BlockSpec((1, DA_HEAD_DIM), lambda bi, h: (0, 0))
    k_specs = [pl.BlockSpec((1, a.shape[1], w), lambda bi, h, c0=kc // hps: (bi, 0, c0 + h)) for a, kc, _ in srcs]
    v_specs = [pl.BlockSpec((1, a.shape[1], w), lambda bi, h, c0=vc // hps: (bi, 0, c0 + h)) for a, _, vc in srcs]
    arrs = [a for a, _, _ in srcs]
    return pl.pallas_call(
        functools.partial(_diff_attn_kernel, n_src=len(srcs), lam_init=lam_init),
        grid=(b, DA_HEADS // hps),
        in_specs=[pl.BlockSpec((1, nq, w), lambda bi, h: (bi, 0, h))] + k_specs + v_specs
                 + [vec, vec, vec, vec, pl.BlockSpec((DA_V_DIM, 1), lambda bi, h: (0, 0))],
        out_specs=pl.BlockSpec((1, nq, w), lambda bi, h: (bi, 0, h)),
        out_shape=jax.ShapeDtypeStruct((b, nq, DA_WIDTH), BF16),
        scratch_shapes=[pltpu.VMEM((nk, LANES), BF16), pltpu.VMEM((DA_VT_ROWS, nk), BF16),
                        pltpu.VMEM((2, nk, 2 * tq), F32), pltpu.VMEM((2, nk, 2 * tq), BF16)],
        compiler_params=_cparams(("parallel", "parallel"), VMEM_LIMIT),
        name="diff_attention",
    )(q_arr, *arrs, *arrs, *lams, subln_g)


def _da_perm():
    perm = np.zeros(LANES, np.int32)
    for l in range(LANES):
        part, within = divmod(l, 64)
        j, rem = divmod(within, 32)
        seg, i = divmod(rem, 16)
        perm[l] = j * 64 + seg * 32 + part * 16 + i
    return perm


def _rope_tables(n_tok):
    t = jnp.arange(n_tok, dtype=jnp.int32)
    n_freq = DA_HEAD_DIM // 4
    inv = ROPE_BASE ** (-jnp.arange(n_freq, dtype=F32) / n_freq)
    ang_r = (t // GRID_W).astype(F32)[:, None] * inv
    ang_c = (t % GRID_W).astype(F32)[:, None] * inv
    cos32 = jnp.concatenate([jnp.cos(ang_r), jnp.cos(ang_c)], axis=1)
    sin32 = jnp.concatenate([jnp.sin(ang_r), jnp.sin(ang_c)], axis=1)
    cos = jnp.tile(cos32, (1, 4))
    sin = jnp.concatenate([-sin32, -sin32, sin32, sin32], axis=1)
    return cos, sin


def _gmlp_kernel(x_ref, g_ref, sh_ref, sc_ref, win_ref, vg_ref, ws_ref, bs_ref, wout_ref, gate_ref,
                 mg_ref, sh2_ref, sc2_ref, wr_ref, o_ref, hx_ref, lg_ref, *, tm):
    x = x_ref[0]
    h = _rms(x) * g_ref[...]
    h = (h * (1.0 + sc_ref[0]) + sh_ref[0]).astype(BF16)
    uv = jax.nn.gelu(jnp.dot(h, win_ref[...], preferred_element_type=F32), approximate=True)
    u = uv[:, :GM_WIDTH]
    v = (_rms(uv[:, GM_WIDTH:]) * vg_ref[...]).astype(BF16)
    rows = []
    for c in range(tm // GM_CHUNK):
        cols = []
        for gi in range(GM_GROUPS):
            vv = v[c * GM_CHUNK:(c + 1) * GM_CHUNK, gi * GM_GROUP_DIM:(gi + 1) * GM_GROUP_DIM]
            sv = jnp.dot(ws_ref[gi], vv, preferred_element_type=F32) + bs_ref[:, gi:gi + 1]
            cols.append(sv)
        rows.append(jnp.concatenate(cols, axis=1))
    sv = jnp.concatenate(rows, axis=0) if len(rows) > 1 else rows[0]
    y = jnp.dot((u * sv).astype(BF16), wout_ref[...], preferred_element_type=F32)
    xn = x + gate_ref[0] * y
    o_ref[0] = xn
    _moe_prologue(xn, mg_ref, sh2_ref, sc2_ref, wr_ref, hx_ref, lg_ref)


def gmlp_layer(x, norm, w_in, v_g, w_s, b_s_t, w_out, gate, moe, tm=512):
    b, n, d = x.shape
    tm = min(tm, n)
    g, sh, sc = norm
    moe_in, moe_out = _moe_prologue_specs(d, tm, N_EXPERTS)
    full = lambda shape: pl.BlockSpec(shape, lambda bi, i: (0,) * len(shape))
    per_b = pl.BlockSpec((1, 1, d), lambda bi, i: (bi, 0, 0))
    return pl.pallas_call(
        functools.partial(_gmlp_kernel, tm=tm),
        grid=(b, n // tm),
        in_specs=[pl.BlockSpec((1, tm, d), lambda bi, i: (bi, i, 0)),
                  full((1, d)), per_b, per_b,
                  full(w_in.shape), full((1, GM_WIDTH)), full(w_s.shape), full(b_s_t.shape),
                  full(w_out.shape), per_b] + moe_in,
        out_specs=[pl.BlockSpec((1, tm, d), lambda bi, i: (bi, i, 0))] + moe_out,
        out_shape=[jax.ShapeDtypeStruct((b, n, d), F32), jax.ShapeDtypeStruct((b, n, d), BF16),
                   jax.ShapeDtypeStruct((b, N_EXPERTS, n), F32)],
        compiler_params=_cparams(("parallel", "arbitrary"), VMEM_LIMIT),
        name="gmlp_layer",
    )(x, g, sh, sc, w_in, v_g, w_s, b_s_t, w_out, gate, *moe)


NA_VT_ROWS = LANES + 16


def _natten_kernel(q_ref, k_ref, v_ref, kc_ref, vc_ref, bias_ref, o_ref, vt_all, vct, s_scr, e_scr, *, n_blocks):
    tq = NA_ROW_BLOCK * GRID_W
    slab_blocks = NA_SLAB_ROWS // NA_ROW_BLOCK
    slab = slab_blocks * tq
    lc = kc_ref.shape[1]
    for blk in range(n_blocks):
        vt_all[:LANES, blk * tq:(blk + 1) * tq] = jnp.transpose(
            v_ref[0, blk * tq:(blk + 1) * tq, :].astype(F32)).astype(BF16)
    vt_all[LANES:, :] = jnp.ones((NA_VT_ROWS - LANES, vt_all.shape[1]), BF16)
    vct[:LANES, :] = jnp.transpose(vc_ref[0].astype(F32)).astype(BF16)
    vct[LANES:, :] = jnp.ones((NA_VT_ROWS - LANES, lc), BF16)

    lane = lax.broadcasted_iota(jnp.int32, (1, LANES), 1)
    left = lane < NA_HEAD_DIM
    dims = (((1,), (1,)), ((), ()))
    c = NA_HEAD_DIM ** -0.5 * math.log2(math.e)
    state = [dict(m=None) for _ in range(n_blocks)]
    key0 = [min(max(t - 1, 0), n_blocks - slab_blocks) * tq for t in range(n_blocks)]
    kind = [0 if t == 0 else 2 if t == n_blocks - 1 else 1 for t in range(n_blocks)]

    def scores(t):
        q = q_ref[0, t * tq:(t + 1) * tq, :]
        zero = jnp.zeros_like(q)
        q2 = jnp.concatenate([jnp.where(left, q, zero), jnp.where(left, zero, q)], axis=0)

        def local():
            k = k_ref[0, key0[t]:key0[t] + slab, :]
            s = lax.dot_general(k, q2, dims, preferred_element_type=F32) * c + bias_ref[0, kind[t]]
            s_scr[t % 2, :slab, :] = s
            state[t]["m"] = jnp.max(s, axis=0, keepdims=True)

        def context():
            s = lax.dot_general(kc_ref[0], q2, dims, preferred_element_type=F32) * c
            s_scr[t % 2, slab:, :] = s
            state[t]["m"] = jnp.maximum(state[t]["m"], jnp.max(s, axis=0, keepdims=True))

        return [local, context]

    def exps(t):
        def chunk(rows):
            e_scr[t % 2, rows, :] = jnp.exp2(s_scr[t % 2, rows, :] - state[t]["m"]).astype(BF16)

        return [functools.partial(chunk, slice(0, slab)), functools.partial(chunk, slice(slab, slab + lc))]

    def values(t):
        vt = jnp.concatenate([vt_all[:, key0[t]:key0[t] + slab], vct[...]], axis=1)
        ot = jnp.dot(vt, e_scr[t % 2], preferred_element_type=F32)
        ot = ot[:LANES] / ot[LANES:LANES + 1]
        pair = jnp.concatenate([ot[:NA_HEAD_DIM, :tq], ot[NA_HEAD_DIM:, tq:]], axis=0)
        o_ref[0, t * tq:(t + 1) * tq, :] = jnp.transpose(pair).astype(o_ref.dtype)

    for t in range(n_blocks + 2):
        if 0 <= t - 2 < n_blocks:
            values(t - 2)
        _interleave(scores(t) if t < n_blocks else [],
                    (exps(t - 1) if 0 <= t - 1 < n_blocks else []))


def _natten_bias(rpb, rows):
    n_heads = rpb.shape[0]
    n_blocks = rows // NA_ROW_BLOCK
    n_dr, n_dc = 2 * NA_WIN_R - 1, 2 * NA_WIN_C - 1
    rpb = rpb * math.log2(math.e)
    span = 2 * GRID_W - 1
    lo = GRID_W - NA_WIN_C - 1
    u = jnp.pad(rpb, ((0, 0), (0, 0), (lo, span - n_dc - lo)))
    skew = jnp.tile(u, (1, 1, GRID_W))[:, :, :GRID_W * (span - 1)].reshape(n_heads, n_dr, GRID_W, span - 1)
    toep = skew[:, :, :, GRID_W - 2:2 * GRID_W - 2]
    pad = NA_ROW_BLOCK
    toep = jnp.pad(toep, ((0, 0), (pad, pad), (0, 0), (0, 0)))
    i = np.arange(NA_ROW_BLOCK)[:, None, None, None]
    c = np.arange(GRID_W)[None, :, None, None]
    m = np.arange(NA_SLAB_ROWS)[None, None, :, None]
    kc = np.arange(GRID_W)[None, None, None, :]
    win_c0 = np.clip(c - NA_WIN_C // 2, 0, GRID_W - NA_WIN_C)
    col_ok = (kc >= win_c0) & (kc < win_c0 + NA_WIN_C)
    tiles, ok = [], []
    for rb in (0, 1, n_blocks - 1):
        slab0 = int(np.clip(rb - 1, 0, n_blocks - 3)) * NA_ROW_BLOCK
        r = rb * NA_ROW_BLOCK + i
        r0 = np.clip(r - NA_WIN_R // 2, 0, rows - NA_WIN_R)
        kr = slab0 + m
        ok.append(np.broadcast_to((kr >= r0) & (kr < r0 + NA_WIN_R) & col_ok,
                                  (NA_ROW_BLOCK, GRID_W, NA_SLAB_ROWS, GRID_W)))
        for ii in range(NA_ROW_BLOCK):
            off = slab0 - (rb * NA_ROW_BLOCK + ii) + NA_WIN_R - 1 + pad
            tiles.append(toep[:, off:off + NA_SLAB_ROWS])
    tq, slab = NA_ROW_BLOCK * GRID_W, NA_SLAB_ROWS * GRID_W
    bias = jnp.stack(tiles, axis=1).reshape(n_heads // 2, 2, 3, NA_ROW_BLOCK, NA_SLAB_ROWS, GRID_W, GRID_W)
    bias = jnp.transpose(bias, (0, 2, 4, 6, 1, 3, 5)).reshape(n_heads // 2, 3, slab, 2 * tq)
    ok = np.stack(ok).transpose(0, 3, 4, 1, 2).reshape(3, slab, tq)
    ok = np.concatenate([ok, ok], axis=2)
    return jnp.where(ok[None], bias, NEG_INF)


def natten(qkv, kv_ctx, bias):
    b, n, _ = qkv.shape
    lc = kv_ctx.shape[1]
    tq = NA_ROW_BLOCK * GRID_W
    n_blocks = n // tq
    hp = NA_HEADS // 2
    return pl.pallas_call(
        functools.partial(_natten_kernel, n_blocks=n_blocks),
        grid=(hp, b),
        in_specs=[pl.BlockSpec((1, n, LANES), lambda p, bi: (bi, 0, p)),
                  pl.BlockSpec((1, n, LANES), lambda p, bi: (bi, 0, hp + p)),
                  pl.BlockSpec((1, n, LANES), lambda p, bi: (bi, 0, 2 * hp + p)),
                  pl.BlockSpec((1, lc, LANES), lambda p, bi: (bi, 0, p)),
                  pl.BlockSpec((1, lc, LANES), lambda p, bi: (bi, 0, hp + p)),
                  pl.BlockSpec((1, 3, NA_SLAB_ROWS * GRID_W, 2 * tq), lambda p, bi: (p, 0, 0, 0))],
        out_specs=pl.BlockSpec((1, n, LANES), lambda p, bi: (bi, 0, p)),
        out_shape=jax.ShapeDtypeStruct((b, n, NA_WIDTH), BF16),
        scratch_shapes=[pltpu.VMEM((NA_VT_ROWS, n), BF16), pltpu.VMEM((NA_VT_ROWS, lc), BF16),
                        pltpu.VMEM((2, NA_SLAB_ROWS * GRID_W + lc, 2 * tq), F32),
                        pltpu.VMEM((2, NA_SLAB_ROWS * GRID_W + lc, 2 * tq), BF16)],
        compiler_params=_cparams(("parallel", "parallel"), VMEM_LIMIT),
        name="natten",
    )(qkv, qkv, qkv, kv_ctx, kv_ctx, bias)


def _dft_tables(n, sign=1.0):
    k = jnp.arange(n, dtype=jnp.int32)
    w = 2.0 * math.pi / n

    def cs(m):
        ang = ((k[:, None] * m[None, :]) % n).astype(F32) * w
        return jnp.cos(ang), jnp.sin(ang)

    if n <= DFT_SPLIT:
        c, s = cs(k)
        return c, sign * s
    ca, sa = cs(jnp.arange(n // DFT_SPLIT, dtype=jnp.int32) * DFT_SPLIT)
    cb, sb = cs(jnp.arange(DFT_SPLIT, dtype=jnp.int32))
    c = ca[:, :, None] * cb[:, None, :] - sa[:, :, None] * sb[:, None, :]
    s = sa[:, :, None] * cb[:, None, :] + ca[:, :, None] * sb[:, None, :]
    return c.reshape(n, n), (sign * s).reshape(n, n)


def _fourier_chan_kernel(x_ref, g_ref, sh_ref, sc_ref, wc_ref, o_ref):
    h = _rms(x_ref[0]) * g_ref[...]
    h = (h * (1.0 + sc_ref[0]) + sh_ref[0]).astype(BF16)
    gd = D_MODEL // FN_GROUPS
    for gi in range(FN_GROUPS):
        z = jnp.dot(h[:, gi * gd:(gi + 1) * gd], wc_ref[...], preferred_element_type=F32).astype(BF16)
        o_ref[0, 0, :, gi * gd:(gi + 1) * gd] = z[:, :gd]
        o_ref[0, 1, :, gi * gd:(gi + 1) * gd] = z[:, gd:]


def _fourier_pos_kernel(wp_ref, z_ref, wout_ref, x_ref, gate_ref, mg_ref, sh2_ref, sc2_ref, wr_ref,
                        o_ref, hx_ref, lg_ref, *, scale):
    f = jnp.dot(wp_ref[...], z_ref[0], preferred_element_type=F32) * scale
    y = jnp.dot(f.astype(BF16), wout_ref[...], preferred_element_type=F32)
    xn = x_ref[0] + gate_ref[0] * y
    o_ref[0] = xn
    _moe_prologue(xn, mg_ref, sh2_ref, sc2_ref, wr_ref, hx_ref, lg_ref)


def fourier_layer(x, norm, w_out, gate, moe, tm=512):
    b, n, d = x.shape
    moe_in, moe_out = _moe_prologue_specs(d, tm, N_EXPERTS)
    gd = d // FN_GROUPS
    g, sh, sc = norm
    cc, sc_tab = _dft_tables(gd)
    wc = jnp.concatenate([cc, sc_tab], axis=1).astype(BF16)
    cn, sn = _dft_tables(n, -1.0)
    wp = jnp.concatenate([cn, sn], axis=1).astype(BF16)
    per_b = pl.BlockSpec((1, 1, d), lambda bi, i: (bi, 0, 0))
    z = pl.pallas_call(
        _fourier_chan_kernel,
        grid=(b, n // tm),
        in_specs=[pl.BlockSpec((1, tm, d), lambda bi, i: (bi, i, 0)),
                  pl.BlockSpec((1, d), lambda bi, i: (0, 0)), per_b, per_b,
                  pl.BlockSpec((gd, 2 * gd), lambda bi, i: (0, 0))],
        out_specs=pl.BlockSpec((1, 2, tm, d), lambda bi, i: (bi, 0, i, 0)),
        out_shape=jax.ShapeDtypeStruct((b, 2, n, d), BF16),
        compiler_params=_cparams(("parallel", "parallel")),
        name="fourier_chan",
    )(x, g, sh, sc, wc)
    z = z.reshape(b, 2 * n, d)
    return pl.pallas_call(
        functools.partial(_fourier_pos_kernel, scale=1.0 / math.sqrt(n * gd)),
        grid=(b, n // tm),
        in_specs=[pl.BlockSpec((tm, 2 * n), lambda bi, i: (i, 0)),
                  pl.BlockSpec((1, 2 * n, d), lambda bi, i: (bi, 0, 0)),
                  pl.BlockSpec((d, d), lambda bi, i: (0, 0)),
                  pl.BlockSpec((1, tm, d), lambda bi, i: (bi, i, 0)), per_b] + moe_in,
        out_specs=[pl.BlockSpec((1, tm, d), lambda bi, i: (bi, i, 0))] + moe_out,
        out_shape=[jax.ShapeDtypeStruct((b, n, d), F32), jax.ShapeDtypeStruct((b, n, d), BF16),
                   jax.ShapeDtypeStruct((b, N_EXPERTS, n), F32)],
        compiler_params=_cparams(("parallel", "arbitrary"), VMEM_LIMIT),
        name="fourier_pos",
    )(wp, z, w_out, x, gate, *moe)


def _excl_cumsum_lanes(mask, tri):
    e, n = mask.shape
    mf = jnp.where(mask, 1.0, 0.0)
    offset = jnp.zeros((e, 1), F32)
    parts = []
    for blk in range(n // LANES):
        part = mf[:, blk * LANES:(blk + 1) * LANES]
        parts.append(jnp.dot(part.astype(BF16), tri, preferred_element_type=F32) + offset)
        offset = offset + jnp.sum(part, axis=1, keepdims=True)
    return jnp.concatenate(parts, axis=1)


def _moe_route_kernel(lg_ref, tri_ref, aff_ref, slot_ref, *, cap):
    lg = lg_ref[...]
    e = jnp.exp(lg - jnp.max(lg, axis=1, keepdims=True))
    aff3 = e / jnp.sum(e, axis=1, keepdims=True)
    aff_ref[...] = aff3
    b, n_exp, n = lg.shape
    aff = aff3.reshape(b * n_exp, n)

    def as_float(bits):
        return lax.bitcast_convert_type(bits, F32)

    def step(it, thr):
        cand = thr | jnp.left_shift(jnp.int32(1), 30 - it)
        cnt = jnp.sum((aff >= as_float(cand)).astype(jnp.int32), axis=1, keepdims=True)
        return jnp.where(cnt >= cap, cand, thr)

    thr = lax.fori_loop(0, 31, step, jnp.zeros((b * n_exp, 1), jnp.int32))
    gt = aff >= as_float(thr + 1)
    eq = (aff >= as_float(thr)) & jnp.logical_not(gt)
    need = (cap - jnp.sum(gt.astype(jnp.int32), axis=1, keepdims=True)).astype(F32)
    tri = tri_ref[...]
    sel = gt | (eq & (_excl_cumsum_lanes(eq, tri) < need))
    pos = _excl_cumsum_lanes(sel, tri).astype(jnp.int32)
    slot_ref[...] = jnp.where(sel, pos, -1).reshape(b, n_exp, n)


def _moe_gather_kernel(slot_ref, aff_ref, hx_ref, xs_ref, ta_ref, *, cap, ge):
    e0 = pl.multiple_of(pl.program_id(1) * ge, ge)
    n = slot_ref.shape[2]
    rows = lax.broadcasted_iota(jnp.int32, (cap, n), 0)
    hits, tas = [], []
    for k in range(ge):
        hit = rows == slot_ref[0, pl.ds(e0 + k, 1), :]
        hits.append(jnp.where(hit, 1.0, 0.0).astype(BF16))
        tas.append(jnp.sum(jnp.where(hit, aff_ref[0, pl.ds(e0 + k, 1), :], 0.0), axis=1, keepdims=True))
    onehot = jnp.concatenate(hits, axis=0) if ge > 1 else hits[0]
    xs = jnp.dot(onehot, hx_ref[0], preferred_element_type=F32).astype(BF16)
    xs_ref[0] = xs.reshape(ge, cap, xs.shape[-1])
    ta = jnp.concatenate(tas, axis=0) if ge > 1 else tas[0]
    ta_ref[0] = jnp.broadcast_to(ta, (ge * cap, LANES)).reshape(ge, cap, LANES)


def _moe_ffn_kernel(xs_ref, ta_ref, wg_ref, wu_ref, wd_ref, ys_ref, wgb, wub, wdb):
    @pl.when(pl.program_id(1) == 0)
    def _():
        wgb[...] = wg_ref[0, 0].astype(BF16)
        wub[...] = wu_ref[0, 0].astype(BF16)
        wdb[...] = wd_ref[0, 0].astype(BF16)

    bb, _, cap, d = xs_ref.shape
    xs = xs_ref[...].reshape(bb * cap, d)
    ta = ta_ref[...].reshape(bb * cap, LANES)[:, :1]
    gate = jnp.dot(xs, wgb[...], preferred_element_type=F32)
    up = jnp.dot(xs, wub[...], preferred_element_type=F32)
    hid = (gate * jax.nn.sigmoid(gate) * up).astype(BF16)
    y = jnp.dot(hid, wdb[...], preferred_element_type=F32) * ta
    ys_ref[...] = y.astype(BF16).reshape(bb, 1, cap, d)


def _moe_scatter_kernel(slot_ref, ys_ref, x_ref, gate_ref, *rest, cap, final):
    if final:
        fg_ref, o_ref = rest
    else:
        (o_ref,) = rest
    slot = slot_ref[0]
    n_exp = slot.shape[1]
    if cap % LANES == 0:
        want = lax.broadcasted_iota(jnp.int32, (1, cap), 1)
        hit = jnp.concatenate([jnp.where(slot[:, e:e + 1] == want, 1.0, 0.0).astype(BF16) for e in range(n_exp)],
                              axis=1)
    else:
        col = lax.broadcasted_iota(jnp.int32, (n_exp, n_exp * cap), 1)
        row = lax.broadcasted_iota(jnp.int32, (n_exp, n_exp * cap), 0)
        spread = jnp.where(col // cap == row, 1.0, 0.0).astype(BF16)
        want = (lax.broadcasted_iota(jnp.int32, (1, n_exp * cap), 1) % cap).astype(F32)
        ids = jnp.dot(slot.astype(F32).astype(BF16), spread, preferred_element_type=F32)
        hit = jnp.where(ids == want, 1.0, 0.0).astype(BF16)
    ys = ys_ref[0].reshape(n_exp * cap, ys_ref.shape[-1])
    out = x_ref[0] + gate_ref[0] * jnp.dot(hit, ys, preferred_element_type=F32)
    if final:
        out = _rms(out) * fg_ref[...]
    o_ref[0] = out


def moe_layer(x, hx, logits, layer, w_gate, w_up, w_down, gate, final_g=None):
    b, n, d = x.shape
    n_exp = logits.shape[1]
    cap = EC_CAPACITY_FACTOR * n // n_exp
    per_b2 = pl.BlockSpec((1, 1, d), lambda bi, i: (bi, 0, 0))
    tri = (np.arange(LANES)[:, None] < np.arange(LANES)[None, :]).astype(np.float32)
    aff, slot = pl.pallas_call(
        functools.partial(_moe_route_kernel, cap=cap),
        grid=(1,),
        in_specs=[pl.BlockSpec((b, n_exp, n), lambda i: (0, 0, 0)),
                  pl.BlockSpec((LANES, LANES), lambda i: (0, 0))],
        out_specs=[pl.BlockSpec((b, n_exp, n), lambda i: (0, 0, 0))] * 2,
        out_shape=[jax.ShapeDtypeStruct((b, n_exp, n), F32), jax.ShapeDtypeStruct((b, n_exp, n), jnp.int32)],
        compiler_params=_cparams(("arbitrary",)),
        name="moe_route",
    )(logits, jnp.asarray(tri, BF16))

    rows_per_step = 1024
    ge = max(1, min(n_exp, rows_per_step // cap))
    xs, ta = pl.pallas_call(
        functools.partial(_moe_gather_kernel, cap=cap, ge=ge),
        grid=(b, n_exp // ge),
        in_specs=[pl.BlockSpec((1, n_exp, n), lambda bi, e: (bi, 0, 0)),
                  pl.BlockSpec((1, n_exp, n), lambda bi, e: (bi, 0, 0)),
                  pl.BlockSpec((1, n, d), lambda bi, e: (bi, 0, 0))],
        out_specs=[pl.BlockSpec((1, ge, cap, d), lambda bi, e: (bi, e, 0, 0)),
                   pl.BlockSpec((1, ge, cap, LANES), lambda bi, e: (bi, e, 0, 0))],
        out_shape=[jax.ShapeDtypeStruct((b, n_exp, cap, d), BF16),
                   jax.ShapeDtypeStruct((b, n_exp, cap, LANES), F32)],
        compiler_params=_cparams(("parallel", "arbitrary"), VMEM_LIMIT),
        name="moe_gather",
    )(slot, aff, hx)

    bb = max(1, min(b, rows_per_step // cap))
    f = w_gate.shape[-1]
    ys = pl.pallas_call(
        _moe_ffn_kernel,
        grid=(n_exp, b // bb),
        in_specs=[pl.BlockSpec((bb, 1, cap, d), lambda e, bi: (bi, e, 0, 0)),
                  pl.BlockSpec((bb, 1, cap, LANES), lambda e, bi: (bi, e, 0, 0)),
                  pl.BlockSpec((1, 1, d, f), lambda e, bi: (layer, e, 0, 0)),
                  pl.BlockSpec((1, 1, d, f), lambda e, bi: (layer, e, 0, 0)),
                  pl.BlockSpec((1, 1, f, d), lambda e, bi: (layer, e, 0, 0))],
        out_specs=pl.BlockSpec((bb, 1, cap, d), lambda e, bi: (bi, e, 0, 0)),
        out_shape=jax.ShapeDtypeStruct((b, n_exp, cap, d), BF16),
        scratch_shapes=[pltpu.VMEM((d, f), BF16), pltpu.VMEM((d, f), BF16), pltpu.VMEM((f, d), BF16)],
        compiler_params=_cparams(("parallel", "arbitrary"), VMEM_LIMIT),
        name="moe_ffn",
    )(xs, ta, w_gate, w_up, w_down)

    slot_t = jnp.swapaxes(slot, 1, 2)
    tn = min(1024, n)
    args = [slot_t, ys, x, gate]
    specs = [pl.BlockSpec((1, tn, n_exp), lambda bi, i: (bi, i, 0)),
             pl.BlockSpec((1, n_exp, cap, d), lambda bi, i: (bi, 0, 0, 0)),
             pl.BlockSpec((1, tn, d), lambda bi, i: (bi, i, 0)), per_b2]
    if final_g is not None:
        args.append(final_g)
        specs.append(pl.BlockSpec((1, d), lambda bi, i: (0, 0)))
    return pl.pallas_call(
        functools.partial(_moe_scatter_kernel, cap=cap, final=final_g is not None),
        grid=(b, n // tn),
        in_specs=specs,
        out_specs=pl.BlockSpec((1, tn, d), lambda bi, i: (bi, i, 0)),
        out_shape=jax.ShapeDtypeStruct((b, n, d), F32),
        compiler_params=_cparams(("parallel", "arbitrary"), VMEM_LIMIT),
        name="moe_scatter",
    )(*args)


def kernel(x, c, ctx, c_ctx, ada_w, ada_b, mixer_norm_g, moe_norm_g, router_w, moe_w_gate, moe_w_up, moe_w_down, da_w_in, da_lambda_q1, da_lambda_k1, da_lambda_q2, da_lambda_k2, da_subln_g, da_w_out, gm_w_in, gm_v_g, gm_w_s, gm_b_s, gm_w_out, na_w_in, na_rpb, na_w_out, fn_w_out, final_norm_g):
    b, n, d = x.shape
    rows = n // GRID_W
    readers = [i for i in range(DEPTH) if i % N_MIXERS in CTX_READERS]
    last_reader = max(readers) if readers else -1

    r_pad = -(-(b + 1) // 8) * 8
    cond = jnp.zeros((r_pad, d), F32).at[:b].set(c).at[b].set(c_ctx)
    mods = ada_params(cond, ada_w, ada_b)

    def lat_mod(i, k):
        return mods[i, :b, k * d:(k + 1) * d].reshape(b, 1, d)

    def ctx_mod(i, k):
        return jnp.broadcast_to(mods[i, b, k * d:(k + 1) * d].reshape(1, 1, d), (b, 1, d))

    rope = _rope_tables(n)
    perm = _da_perm()

    for i in range(DEPTH):
        kind, j = i % N_MIXERS, i // N_MIXERS
        need_ctx = i <= last_reader
        update_ctx = i < last_reader
        mg = mixer_norm_g[i].reshape(1, d)
        xnorm = (mg, lat_mod(i, 0), lat_mod(i, 1))
        cnorm = (mg, ctx_mod(i, 0), ctx_mod(i, 1)) if need_ctx else None
        g1 = lat_mod(i, 2)
        wr = _router_split(router_w[i])
        moe_g = moe_norm_g[i].reshape(1, d)
        xmoe = (moe_g, lat_mod(i, 3), lat_mod(i, 4), wr)
        cmoe = (moe_g, ctx_mod(i, 3), ctx_mod(i, 4), wr) if update_ctx else None

        if kind == MIX_DIFF:
            lam_init = 0.8 - 0.6 * math.exp(-0.3 * i)
            cols = np.arange(2 * DA_WIDTH).reshape(2 * DA_HEADS, LANES)[:, perm].reshape(-1)
            w_in = jnp.concatenate([da_w_in[j][:, cols], da_w_in[j][:, 2 * DA_WIDTH:]], axis=1).astype(BF16)
            w_out = da_w_out[j].astype(BF16)
            lams = [t[j].reshape(1, DA_HEAD_DIM) for t in (da_lambda_q1, da_lambda_k1, da_lambda_q2, da_lambda_k2)]
            sg = da_subln_g[j].reshape(DA_V_DIM, 1)
            nb = DA_WIDTH // LANES
            qkv = linear(x, w_in, norm=xnorm, rope=rope, rope_tiles=2)
            if update_ctx:
                qkv_c = linear(ctx, w_in, norm=cnorm)
                ctx_src = (qkv_c, nb, 2 * nb)
            else:
                ctx_src = (linear(ctx, w_in[:, DA_WIDTH:], norm=cnorm), 0, nb)
            o = diff_attention(qkv, [ctx_src, (qkv, nb, 2 * nb)], lams, sg, lam_init)
            x, hx, logits = linear(o, w_out, res=(x, g1), moe=xmoe, out_dtype=F32)
            if update_ctx:
                oc = diff_attention(qkv_c, [ctx_src], lams, sg, lam_init, heads_per_step=DA_HEADS)
                ctx, hc, logits_c = linear(oc, w_out, res=(ctx, ctx_mod(i, 2)), moe=cmoe, out_dtype=F32)
        elif kind == MIX_GMLP:
            w_in = gm_w_in[j].astype(BF16)
            w_out = gm_w_out[j].astype(BF16)
            vg = gm_v_g[j].reshape(1, GM_WIDTH)
            w_s = gm_w_s[j].astype(BF16)
            b_s_t = jnp.transpose(gm_b_s[j])
            x, hx, logits = gmlp_layer(x, xnorm, w_in, vg, w_s, b_s_t, w_out, g1, xmoe)
            if update_ctx:
                ctx, hc, logits_c = gmlp_layer(ctx, cnorm, w_in, vg, w_s, b_s_t, w_out, ctx_mod(i, 2), cmoe)
        elif kind == MIX_NATTEN:
            w_in = na_w_in[j].astype(BF16)
            w_out = na_w_out[j].astype(BF16)
            qkv = linear(x, w_in, norm=xnorm)
            kv_c = linear(ctx, w_in[:, NA_WIDTH:], norm=cnorm)
            o = natten(qkv, kv_c, _natten_bias(na_rpb[j], rows))
            x, hx, logits = linear(o, w_out, res=(x, g1), moe=xmoe, out_dtype=F32)
            if update_ctx:
                raise NotImplementedError("context update after a neighbourhood layer")
        else:
            x, hx, logits = fourier_layer(x, xnorm, fn_w_out[j].astype(BF16), g1, xmoe)
            if update_ctx:
                ctx, hc, logits_c = fourier_layer(ctx, cnorm, fn_w_out[j].astype(BF16), ctx_mod(i, 2), cmoe)

        last = i == DEPTH - 1
        x = moe_layer(x, hx, logits, i, moe_w_gate, moe_w_up, moe_w_down, lat_mod(i, 5),
                      final_g=final_norm_g.reshape(1, d) if last else None)
        if update_ctx:
            ctx = moe_layer(ctx, hc, logits_c, i, moe_w_gate, moe_w_up, moe_w_down, ctx_mod(i, 5))
    return x
```

```python
import functools
import math

import numpy as np
import jax
import jax.numpy as jnp
from jax import lax
from jax.experimental import pallas as pl
from jax.experimental.pallas import tpu as pltpu

F32 = jnp.float32
BF16 = jnp.bfloat16

D_MODEL = 1024
DEPTH = 4
GRID_W = 64
N_MIXERS = 4
MIX_DIFF, MIX_GMLP, MIX_NATTEN, MIX_FOURIER = 0, 1, 2, 3
CTX_READERS = (MIX_DIFF, MIX_NATTEN)
EPS = 1e-6
NEG_INF = -1e30
ROPE_BASE = 10000.0

DA_HEADS = 8
DA_HEAD_DIM = 64
DA_V_DIM = 2 * DA_HEAD_DIM
DA_WIDTH = DA_HEADS * DA_V_DIM

GM_CHUNK = 128
GM_GROUPS = 8
GM_WIDTH = 2 * D_MODEL
GM_GROUP_DIM = GM_WIDTH // GM_GROUPS

NA_HEADS = 16
NA_HEAD_DIM = D_MODEL // NA_HEADS
NA_WIDTH = NA_HEADS * NA_HEAD_DIM
NA_WIN_R = 8
NA_WIN_C = 16
NA_ROW_BLOCK = 4
NA_SLAB_ROWS = NA_ROW_BLOCK + NA_WIN_R

FN_GROUPS = 4
DFT_SPLIT = 256

N_EXPERTS = 16
EC_CAPACITY_FACTOR = 2

LANES = 128
VMEM_LIMIT = 56 * 1024 * 1024


def _cparams(sem, vmem=None):
    return pltpu.CompilerParams(dimension_semantics=sem, vmem_limit_bytes=vmem)


def _rms(x):
    return x * lax.rsqrt(jnp.mean(x * x, axis=-1, keepdims=True) + EPS)


def _ada_kernel(c_ref, w_ref, b_ref, o_ref):
    c = c_ref[...]
    h = (c * jax.nn.sigmoid(c)).astype(BF16)
    o_ref[0] = jnp.dot(h, w_ref[0].astype(BF16), preferred_element_type=F32) + b_ref[0]


def ada_params(cond, ada_w, ada_b):
    r = cond.shape[0]
    depth, d, n_out = ada_w.shape
    tn = 1024
    return pl.pallas_call(
        _ada_kernel,
        grid=(depth, n_out // tn),
        in_specs=[pl.BlockSpec((r, d), lambda l, j: (0, 0)),
                  pl.BlockSpec((1, d, tn), lambda l, j: (l, 0, j)),
                  pl.BlockSpec((1, 1, tn), lambda l, j: (l, 0, j))],
        out_specs=pl.BlockSpec((1, r, tn), lambda l, j: (l, 0, j)),
        out_shape=jax.ShapeDtypeStruct((depth, r, n_out), F32),
        compiler_params=_cparams(("parallel", "parallel")),
        name="ada_params",
    )(cond, ada_w, ada_b.reshape(depth, 1, n_out))


def _moe_prologue(xn, mg_ref, sh_ref, sc_ref, wr_ref, hx_ref, lg_ref):
    h = _rms(xn) * mg_ref[...]
    h = h * (1.0 + sc_ref[0]) + sh_ref[0]
    h_hi = h.astype(BF16)
    h_lo = (h - h_hi.astype(F32)).astype(BF16)
    hx_ref[0] = h_hi
    a = jnp.dot(h_hi, wr_ref[...], preferred_element_type=F32)
    b = jnp.dot(h_lo, wr_ref[:, :LANES], preferred_element_type=F32)
    lt = a[:, :LANES] + a[:, LANES:] + b
    lg_ref[0] = jnp.transpose(lt)[:lg_ref.shape[1]]


def _router_split(w_router):
    w = jnp.pad(w_router, ((0, 0), (0, LANES - w_router.shape[1])))
    hi = w.astype(BF16)
    return jnp.concatenate([hi, (w - hi.astype(F32)).astype(BF16)], axis=1)


def _moe_prologue_specs(d, tm, n_exp):
    in_specs = [pl.BlockSpec((1, d), lambda bi, i: (0, 0)),
                pl.BlockSpec((1, 1, d), lambda bi, i: (bi, 0, 0)),
                pl.BlockSpec((1, 1, d), lambda bi, i: (bi, 0, 0)),
                pl.BlockSpec((d, 2 * LANES), lambda bi, i: (0, 0))]
    out_specs = [pl.BlockSpec((1, tm, d), lambda bi, i: (bi, i, 0)),
                 pl.BlockSpec((1, n_exp, tm), lambda bi, i: (bi, 0, i))]
    return in_specs, out_specs


def _linear_kernel(*refs, has_norm, has_res, has_moe, rope_tiles, tn):
    it = iter(refs)
    x_ref = next(it)
    if has_norm:
        g_ref, sh_ref, sc_ref = next(it), next(it), next(it)
    w_ref = next(it)
    if rope_tiles:
        cos_ref, sin_ref = next(it), next(it)
    if has_res:
        res_ref, gate_ref = next(it), next(it)
    if has_moe:
        moe_in = [next(it) for _ in range(4)]
    o_ref = next(it)

    if has_norm:
        h = _rms(x_ref[0]) * g_ref[...]
        hb = (h * (1.0 + sc_ref[0]) + sh_ref[0]).astype(BF16)
    else:
        hb = x_ref[0].astype(BF16)
    for j in range(w_ref.shape[1] // tn):
        cols = slice(j * tn, (j + 1) * tn)
        y = jnp.dot(hb, w_ref[:, cols], preferred_element_type=F32)
        if j < rope_tiles:
            cos, sin = cos_ref[...], sin_ref[...]
            for s in range(tn // LANES):
                seg = y[:, s * LANES:(s + 1) * LANES]
                rot = pltpu.roll(seg, LANES // 2, axis=1)
                lanes = slice(j * tn + s * LANES, j * tn + (s + 1) * LANES)
                o_ref[0, :, lanes] = (seg * cos + rot * sin).astype(o_ref.dtype)
        elif has_res:
            o_ref[0, :, cols] = (res_ref[0, :, cols] + gate_ref[0, :, cols] * y).astype(o_ref.dtype)
        else:
            o_ref[0, :, cols] = y.astype(o_ref.dtype)
    if has_moe:
        _moe_prologue(o_ref[0], *moe_in, next(it), next(it))


def linear(x, w, *, norm=None, res=None, rope=None, rope_tiles=0, moe=None, out_dtype=BF16, tm=1024, tn=1024):
    b, n, k = x.shape
    m = w.shape[1]
    tm = min(tm, n)
    tn = min(tn, m)
    args, specs = [x], [pl.BlockSpec((1, tm, k), lambda bi, i: (bi, i, 0))]
    if norm is not None:
        g, sh, sc = norm
        args += [g, sh, sc]
        specs += [pl.BlockSpec((1, k), lambda bi, i: (0, 0)),
                  pl.BlockSpec((1, 1, k), lambda bi, i: (bi, 0, 0)),
                  pl.BlockSpec((1, 1, k), lambda bi, i: (bi, 0, 0))]
    args.append(w)
    specs.append(pl.BlockSpec((k, m), lambda bi, i: (0, 0)))
    if rope is not None:
        args += list(rope)
        specs += [pl.BlockSpec((tm, LANES), lambda bi, i: (i, 0))] * 2
    if res is not None:
        r, gate = res
        args += [r, gate]
        specs += [pl.BlockSpec((1, tm, m), lambda bi, i: (bi, i, 0)),
                  pl.BlockSpec((1, 1, m), lambda bi, i: (bi, 0, 0))]
    out_specs = [pl.BlockSpec((1, tm, m), lambda bi, i: (bi, i, 0))]
    out_shape = [jax.ShapeDtypeStruct((b, n, m), out_dtype)]
    if moe is not None:
        moe_in, moe_out = _moe_prologue_specs(m, tm, N_EXPERTS)
        args += list(moe)
        specs += moe_in
        out_specs += moe_out
        out_shape += [jax.ShapeDtypeStruct((b, n, m), BF16), jax.ShapeDtypeStruct((b, N_EXPERTS, n), F32)]
    kern = functools.partial(_linear_kernel, has_norm=norm is not None, has_res=res is not None,
                             has_moe=moe is not None, rope_tiles=rope_tiles if rope is not None else 0, tn=tn)
    out = pl.pallas_call(
        kern,
        grid=(b, n // tm),
        in_specs=specs,
        out_specs=out_specs,
        out_shape=out_shape,
        compiler_params=_cparams(("parallel", "parallel"), VMEM_LIMIT),
        name="linear",
    )(*args)
    return out if moe is not None else out[0]


DA_VT_ROWS = DA_V_DIM + 16
ATT_TQ = 256
ATT_KC = 768


def _interleave(stage_a, stage_b):
    for i in range(max(len(stage_a), len(stage_b))):
        if i < len(stage_a):
            stage_a[i]()
        if i < len(stage_b):
            stage_b[i]()


def _diff_attn_kernel(*refs, n_src, lam_init):
    for hh in range(refs[0].shape[2] // LANES):
        _diff_attn_head(slice(hh * LANES, (hh + 1) * LANES), refs, n_src, lam_init)


def _diff_attn_head(hl, refs, n_src, lam_init):
    q_ref = refs[0]
    k_refs = refs[1:1 + n_src]
    v_refs = refs[1 + n_src:1 + 2 * n_src]
    lq1_ref, lk1_ref, lq2_ref, lk2_ref, g_ref, o_ref, k_all, vt_all, s_scr, e_scr = refs[1 + 2 * n_src:]

    row = 0
    for k_ref, v_ref in zip(k_refs, v_refs):
        n = k_ref.shape[1]
        k_all[row:row + n, :] = k_ref[0, :, hl]
        vt_all[:DA_V_DIM, row:row + n] = jnp.transpose(v_ref[0, :, hl].astype(F32)).astype(BF16)
        row += n
    vt_all[DA_V_DIM:, :] = jnp.ones((DA_VT_ROWS - DA_V_DIM, vt_all.shape[1]), BF16)

    lam = (jnp.exp(jnp.sum(lq1_ref[...] * lk1_ref[...], keepdims=True))
           - jnp.exp(jnp.sum(lq2_ref[...] * lk2_ref[...], keepdims=True)) + lam_init)
    lane = lax.broadcasted_iota(jnp.int32, (1, LANES), 1)
    first = (lane % DA_HEAD_DIM) < (DA_HEAD_DIM // 2)
    dims = (((1,), (1,)), ((), ()))
    c = DA_HEAD_DIM ** -0.5 * math.log2(math.e)
    nk = k_all.shape[0]
    tq = min(ATT_TQ, q_ref.shape[1])
    n_tiles = q_ref.shape[1] // tq
    chunks = [slice(r, min(r + ATT_KC, nk)) for r in range(0, nk, ATT_KC)]
    state = [dict(m=None) for _ in range(n_tiles)]

    def scores(t):
        q = q_ref[0, t * tq:(t + 1) * tq, hl]
        zero = jnp.zeros_like(q)
        q2 = jnp.concatenate([jnp.where(first, q, zero), jnp.where(first, zero, q)], axis=0)

        def chunk(rows):
            s = lax.dot_general(k_all[rows, :], q2, dims, preferred_element_type=F32) * c
            s_scr[t % 2, rows, :] = s
            m = jnp.max(s, axis=0, keepdims=True)
            state[t]["m"] = m if state[t]["m"] is None else jnp.maximum(state[t]["m"], m)

        return [functools.partial(chunk, rows) for rows in chunks]

    def exps(t):
        def chunk(rows):
            e_scr[t % 2, rows, :] = jnp.exp2(s_scr[t % 2, rows, :] - state[t]["m"]).astype(BF16)

        return [functools.partial(chunk, rows) for rows in chunks]

    def values(t):
        ot = jnp.dot(vt_all[...], e_scr[t % 2], preferred_element_type=F32)
        ot = ot[:DA_V_DIM] / ot[DA_V_DIM:DA_V_DIM + 1]
        ot = ot[:, :tq] - lam * ot[:, tq:]
        ot = ot * lax.rsqrt(jnp.mean(ot * ot, axis=0, keepdims=True) + EPS) * g_ref[...] * (1.0 - lam_init)
        o_ref[0, t * tq:(t + 1) * tq, hl] = jnp.transpose(ot).astype(o_ref.dtype)

    for t in range(n_tiles + 2):
        if 0 <= t - 2 < n_tiles:
            values(t - 2)
        _interleave(scores(t) if t < n_tiles else [],
                    (exps(t - 1) if 0 <= t - 1 < n_tiles else []))


def diff_attention(q_arr, srcs, lams, subln_g, lam_init, heads_per_step=1):
    b, nq, _ = q_arr.shape
    nk = sum(a.shape[1] for a, _, _ in srcs)
    tq = min(ATT_TQ, nq)
    hps = heads_per_step
    w = hps * LANES
    assert all(kc % hps == 0 and vc % hps == 0 for _, kc, vc in srcs)
    vec = pl.BlockSpec((1, DA_HEAD_DIM), lambda bi, h: (0, 0))
    k_specs = [pl.BlockSpec((1, a.shape[1], w), lambda bi, h, c0=kc // hps: (bi, 0, c0 + h)) for a, kc, _ in srcs]
    v_specs = [pl.BlockSpec((1, a.shape[1], w), lambda bi, h, c0=vc // hps: (bi, 0, c0 + h)) for a, _, vc in srcs]
    arrs = [a for a, _, _ in srcs]
    return pl.pallas_call(
        functools.partial(_diff_attn_kernel, n_src=len(srcs), lam_init=lam_init),
        grid=(b, DA_HEADS // hps),
        in_specs=[pl.BlockSpec((1, nq, w), lambda bi, h: (bi, 0, h))] + k_specs + v_specs
                 + [vec, vec, vec, vec, pl.BlockSpec((DA_V_DIM, 1), lambda bi, h: (0, 0))],
        out_specs=pl.BlockSpec((1, nq, w), lambda bi, h: (bi, 0, h)),
        out_shape=jax.ShapeDtypeStruct((b, nq, DA_WIDTH), BF16),
        scratch_shapes=[pltpu.VMEM((nk, LANES), BF16), pltpu.VMEM((DA_VT_ROWS, nk), BF16),
                        pltpu.VMEM((2, nk, 2 * tq), F32), pltpu.VMEM((2, nk, 2 * tq), BF16)],
        compiler_params=_cparams(("parallel", "parallel"), VMEM_LIMIT),
        name="diff_attention",
    )(q_arr, *arrs, *arrs, *lams, subln_g)


def _da_perm():
    perm = np.zeros(LANES, np.int32)
    for l in range(LANES):
        part, within = divmod(l, 64)
        j, rem = divmod(within, 32)
        seg, i = divmod(rem, 16)
        perm[l] = j * 64 + seg * 32 + part * 16 + i
    return perm


def _rope_tables(n_tok):
    t = jnp.arange(n_tok, dtype=jnp.int32)
    n_freq = DA_HEAD_DIM // 4
    inv = ROPE_BASE ** (-jnp.arange(n_freq, dtype=F32) / n_freq)
    ang_r = (t // GRID_W).astype(F32)[:, None] * inv
    ang_c = (t % GRID_W).astype(F32)[:, None] * inv
    cos32 = jnp.concatenate([jnp.cos(ang_r), jnp.cos(ang_c)], axis=1)
    sin32 = jnp.concatenate([jnp.sin(ang_r), jnp.sin(ang_c)], axis=1)
    cos = jnp.tile(cos32, (1, 4))
    sin = jnp.concatenate([-sin32, -sin32, sin32, sin32], axis=1)
    return cos, sin


def _gmlp_kernel(x_ref, g_ref, sh_ref, sc_ref, win_ref, vg_ref, ws_ref, bs_ref, wout_ref, gate_ref,
                 mg_ref, sh2_ref, sc2_ref, wr_ref, o_ref, hx_ref, lg_ref, *, tm):
    x = x_ref[0]
    h = _rms(x) * g_ref[...]
    h = (h * (1.0 + sc_ref[0]) + sh_ref[0]).astype(BF16)
    uv = jax.nn.gelu(jnp.dot(h, win_ref[...], preferred_element_type=F32), approximate=True)
    u = uv[:, :GM_WIDTH]
    v = (_rms(uv[:, GM_WIDTH:]) * vg_ref[...]).astype(BF16)
    rows = []
    for c in range(tm // GM_CHUNK):
        cols = []
        for gi in range(GM_GROUPS):
            vv = v[c * GM_CHUNK:(c + 1) * GM_CHUNK, gi * GM_GROUP_DIM:(gi + 1) * GM_GROUP_DIM]
            sv = jnp.dot(ws_ref[gi], vv, preferred_element_type=F32) + bs_ref[:, gi:gi + 1]
            cols.append(sv)
        rows.append(jnp.concatenate(cols, axis=1))
    sv = jnp.concatenate(rows, axis=0) if len(rows) > 1 else rows[0]
    y = jnp.dot((u * sv).astype(BF16), wout_ref[...], preferred_element_type=F32)
    xn = x + gate_ref[0] * y
    o_ref[0] = xn
    _moe_prologue(xn, mg_ref, sh2_ref, sc2_ref, wr_ref, hx_ref, lg_ref)


def gmlp_layer(x, norm, w_in, v_g, w_s, b_s_t, w_out, gate, moe, tm=512):
    b, n, d = x.shape
    tm = min(tm, n)
    g, sh, sc = norm
    moe_in, moe_out = _moe_prologue_specs(d, tm, N_EXPERTS)
    full = lambda shape: pl.BlockSpec(shape, lambda bi, i: (0,) * len(shape))
    per_b = pl.BlockSpec((1, 1, d), lambda bi, i: (bi, 0, 0))
    return pl.pallas_call(
        functools.partial(_gmlp_kernel, tm=tm),
        grid=(b, n // tm),
        in_specs=[pl.BlockSpec((1, tm, d), lambda bi, i: (bi, i, 0)),
                  full((1, d)), per_b, per_b,
                  full(w_in.shape), full((1, GM_WIDTH)), full(w_s.shape), full(b_s_t.shape),
                  full(w_out.shape), per_b] + moe_in,
        out_specs=[pl.BlockSpec((1, tm, d), lambda bi, i: (bi, i, 0))] + moe_out,
        out_shape=[jax.ShapeDtypeStruct((b, n, d), F32), jax.ShapeDtypeStruct((b, n, d), BF16),
                   jax.ShapeDtypeStruct((b, N_EXPERTS, n), F32)],
        compiler_params=_cparams(("parallel", "arbitrary"), VMEM_LIMIT),
        name="gmlp_layer",
    )(x, g, sh, sc, w_in, v_g, w_s, b_s_t, w_out, gate, *moe)


NA_VT_ROWS = LANES + 16


def _natten_kernel(q_ref, k_ref, v_ref, kc_ref, vc_ref, bias_ref, o_ref, vt_all, vct, s_scr, e_scr, *, n_blocks):
    tq = NA_ROW_BLOCK * GRID_W
    slab_blocks = NA_SLAB_ROWS // NA_ROW_BLOCK
    slab = slab_blocks * tq
    lc = kc_ref.shape[1]
    for blk in range(n_blocks):
        vt_all[:LANES, blk * tq:(blk + 1) * tq] = jnp.transpose(
            v_ref[0, blk * tq:(blk + 1) * tq, :].astype(F32)).astype(BF16)
    vt_all[LANES:, :] = jnp.ones((NA_VT_ROWS - LANES, vt_all.shape[1]), BF16)
    vct[:LANES, :] = jnp.transpose(vc_ref[0].astype(F32)).astype(BF16)
    vct[LANES:, :] = jnp.ones((NA_VT_ROWS - LANES, lc), BF16)

    lane = lax.broadcasted_iota(jnp.int32, (1, LANES), 1)
    left = lane < NA_HEAD_DIM
    dims = (((1,), (1,)), ((), ()))
    c = NA_HEAD_DIM ** -0.5 * math.log2(math.e)
    state = [dict(m=None) for _ in range(n_blocks)]
    key0 = [min(max(t - 1, 0), n_blocks - slab_blocks) * tq for t in range(n_blocks)]
    kind = [0 if t == 0 else 2 if t == n_blocks - 1 else 1 for t in range(n_blocks)]

    def scores(t):
        q = q_ref[0, t * tq:(t + 1) * tq, :]
        zero = jnp.zeros_like(q)
        q2 = jnp.concatenate([jnp.where(left, q, zero), jnp.where(left, zero, q)], axis=0)

        def local():
            k = k_ref[0, key0[t]:key0[t] + slab, :]
            s = lax.dot_general(k, q2, dims, preferred_element_type=F32) * c + bias_ref[0, kind[t]]
            s_scr[t % 2, :slab, :] = s
            state[t]["m"] = jnp.max(s, axis=0, keepdims=True)

        def context():
            s = lax.dot_general(kc_ref[0], q2, dims, preferred_element_type=F32) * c
            s_scr[t % 2, slab:, :] = s
            state[t]["m"] = jnp.maximum(state[t]["m"], jnp.max(s, axis=0, keepdims=True))

        return [local, context]

    def exps(t):
        def chunk(rows):
            e_scr[t % 2, rows, :] = jnp.exp2(s_scr[t % 2, rows, :] - state[t]["m"]).astype(BF16)

        return [functools.partial(chunk, slice(0, slab)), functools.partial(chunk, slice(slab, slab + lc))]

    def values(t):
        vt = jnp.concatenate([vt_all[:, key0[t]:key0[t] + slab], vct[...]], axis=1)
        ot = jnp.dot(vt, e_scr[t % 2], preferred_element_type=F32)
        ot = ot[:LANES] / ot[LANES:LANES + 1]
        pair = jnp.concatenate([ot[:NA_HEAD_DIM, :tq], ot[NA_HEAD_DIM:, tq:]], axis=0)
        o_ref[0, t * tq:(t + 1) * tq, :] = jnp.transpose(pair).astype(o_ref.dtype)

    for t in range(n_blocks + 2):
        if 0 <= t - 2 < n_blocks:
            values(t - 2)
        _interleave(scores(t) if t < n_blocks else [],
                    (exps(t - 1) if 0 <= t - 1 < n_blocks else []))


def _natten_bias(rpb, rows):
    n_heads = rpb.shape[0]
    n_blocks = rows // NA_ROW_BLOCK
    n_dr, n_dc = 2 * NA_WIN_R - 1, 2 * NA_WIN_C - 1
    f = rpb[:, :, ::-1] * math.log2(math.e)
    span = 2 * GRID_W - 1
    lo = GRID_W - NA_WIN_C - 1
    u = jnp.pad(f, ((0, 0), (0, 0), (lo, span - n_dc - lo)))
    skew = jnp.tile(u, (1, 1, GRID_W))[:, :, :GRID_W * (span - 1)].reshape(n_heads, n_dr, GRID_W, span - 1)
    toep = skew[:, :, :, GRID_W - 2:2 * GRID_W - 2]
    pad = NA_ROW_BLOCK
    toep = jnp.pad(toep, ((0, 0), (pad, pad), (0, 0), (0, 0)))
    i = np.arange(NA_ROW_BLOCK)[:, None, None, None]
    c = np.arange(GRID_W)[None, :, None, None]
    m = np.arange(NA_SLAB_ROWS)[None, None, :, None]
    kc = np.arange(GRID_W)[None, None, None, :]
    win_c0 = np.clip(c - NA_WIN_C // 2, 0, GRID_W - NA_WIN_C)
    col_ok = (kc >= win_c0) & (kc < win_c0 + NA_WIN_C)
    tq, slab = NA_ROW_BLOCK * GRID_W, NA_SLAB_ROWS * GRID_W
    kinds, ok = [], []
    for rb in (0, 1, n_blocks - 1):
        slab0 = int(np.clip(rb - 1, 0, n_blocks - 3)) * NA_ROW_BLOCK
        r = rb * NA_ROW_BLOCK + i
        r0 = np.clip(r - NA_WIN_R // 2, 0, rows - NA_WIN_R)
        kr = slab0 + m
        ok.append(np.broadcast_to((kr >= r0) & (kr < r0 + NA_WIN_R) & col_ok,
                                  (NA_ROW_BLOCK, GRID_W, NA_SLAB_ROWS, GRID_W)))
        rows_i = []
        for ii in range(NA_ROW_BLOCK):
            off = slab0 - (rb * NA_ROW_BLOCK + ii) + NA_WIN_R - 1 + pad
            rows_i.append(toep[:, off:off + NA_SLAB_ROWS].reshape(n_heads, slab, GRID_W))
        kinds.append(jnp.concatenate(rows_i, axis=2))
    bias = jnp.stack(kinds).reshape(3, n_heads // 2, 2, slab, tq)
    bias = jnp.transpose(bias, (1, 0, 3, 2, 4)).reshape(n_heads // 2, 3, slab, 2 * tq)
    ok = np.stack(ok).transpose(0, 3, 4, 1, 2).reshape(3, slab, tq)
    ok = np.concatenate([ok, ok], axis=2)
    return jnp.where(ok[None], bias, NEG_INF)


def natten(qkv, kv_ctx, bias):
    b, n, _ = qkv.shape
    lc = kv_ctx.shape[1]
    tq = NA_ROW_BLOCK * GRID_W
    n_blocks = n // tq
    hp = NA_HEADS // 2
    return pl.pallas_call(
        functools.partial(_natten_kernel, n_blocks=n_blocks),
        grid=(hp, b),
        in_specs=[pl.BlockSpec((1, n, LANES), lambda p, bi: (bi, 0, p)),
                  pl.BlockSpec((1, n, LANES), lambda p, bi: (bi, 0, hp + p)),
                  pl.BlockSpec((1, n, LANES), lambda p, bi: (bi, 0, 2 * hp + p)),
                  pl.BlockSpec((1, lc, LANES), lambda p, bi: (bi, 0, p)),
                  pl.BlockSpec((1, lc, LANES), lambda p, bi: (bi, 0, hp + p)),
                  pl.BlockSpec((1, 3, NA_SLAB_ROWS * GRID_W, 2 * tq), lambda p, bi: (p, 0, 0, 0))],
        out_specs=pl.BlockSpec((1, n, LANES), lambda p, bi: (bi, 0, p)),
        out_shape=jax.ShapeDtypeStruct((b, n, NA_WIDTH), BF16),
        scratch_shapes=[pltpu.VMEM((NA_VT_ROWS, n), BF16), pltpu.VMEM((NA_VT_ROWS, lc), BF16),
                        pltpu.VMEM((2, NA_SLAB_ROWS * GRID_W + lc, 2 * tq), F32),
                        pltpu.VMEM((2, NA_SLAB_ROWS * GRID_W + lc, 2 * tq), BF16)],
        compiler_params=_cparams(("parallel", "parallel"), VMEM_LIMIT),
        name="natten",
    )(qkv, qkv, qkv, kv_ctx, kv_ctx, bias)


def _dft_tables(n, sign=1.0):
    k = jnp.arange(n, dtype=jnp.int32)
    w = 2.0 * math.pi / n

    def cs(m):
        ang = ((k[:, None] * m[None, :]) % n).astype(F32) * w
        return jnp.cos(ang), jnp.sin(ang)

    if n <= DFT_SPLIT:
        c, s = cs(k)
        return c, sign * s
    ca, sa = cs(jnp.arange(n // DFT_SPLIT, dtype=jnp.int32) * DFT_SPLIT)
    cb, sb = cs(jnp.arange(DFT_SPLIT, dtype=jnp.int32))
    c = ca[:, :, None] * cb[:, None, :] - sa[:, :, None] * sb[:, None, :]
    s = sa[:, :, None] * cb[:, None, :] + ca[:, :, None] * sb[:, None, :]
    return c.reshape(n, n), (sign * s).reshape(n, n)


def _fourier_chan_kernel(x_ref, g_ref, sh_ref, sc_ref, wc_ref, o_ref):
    h = _rms(x_ref[0]) * g_ref[...]
    h = (h * (1.0 + sc_ref[0]) + sh_ref[0]).astype(BF16)
    gd = D_MODEL // FN_GROUPS
    for gi in range(FN_GROUPS):
        z = jnp.dot(h[:, gi * gd:(gi + 1) * gd], wc_ref[...], preferred_element_type=F32).astype(BF16)
        o_ref[0, 0, :, gi * gd:(gi + 1) * gd] = z[:, :gd]
        o_ref[0, 1, :, gi * gd:(gi + 1) * gd] = z[:, gd:]


def _fourier_pos_kernel(wp_ref, z_ref, wout_ref, x_ref, gate_ref, mg_ref, sh2_ref, sc2_ref, wr_ref,
                        o_ref, hx_ref, lg_ref, *, scale):
    f = jnp.dot(wp_ref[...], z_ref[0], preferred_element_type=F32) * scale
    y = jnp.dot(f.astype(BF16), wout_ref[...], preferred_element_type=F32)
    xn = x_ref[0] + gate_ref[0] * y
    o_ref[0] = xn
    _moe_prologue(xn, mg_ref, sh2_ref, sc2_ref, wr_ref, hx_ref, lg_ref)


def fourier_layer(x, norm, w_out, gate, moe, tm=512):
    b, n, d = x.shape
    moe_in, moe_out = _moe_prologue_specs(d, tm, N_EXPERTS)
    gd = d // FN_GROUPS
    g, sh, sc = norm
    cc, sc_tab = _dft_tables(gd)
    wc = jnp.concatenate([cc, sc_tab], axis=1).astype(BF16)
    cn, sn = _dft_tables(n, -1.0)
    wp = jnp.concatenate([cn, sn], axis=1).astype(BF16)
    per_b = pl.BlockSpec((1, 1, d), lambda bi, i: (bi, 0, 0))
    z = pl.pallas_call(
        _fourier_chan_kernel,
        grid=(b, n // tm),
        in_specs=[pl.BlockSpec((1, tm, d), lambda bi, i: (bi, i, 0)),
                  pl.BlockSpec((1, d), lambda bi, i: (0, 0)), per_b, per_b,
                  pl.BlockSpec((gd, 2 * gd), lambda bi, i: (0, 0))],
        out_specs=pl.BlockSpec((1, 2, tm, d), lambda bi, i: (bi, 0, i, 0)),
        out_shape=jax.ShapeDtypeStruct((b, 2, n, d), BF16),
        compiler_params=_cparams(("parallel", "parallel")),
        name="fourier_chan",
    )(x, g, sh, sc, wc)
    z = z.reshape(b, 2 * n, d)
    return pl.pallas_call(
        functools.partial(_fourier_pos_kernel, scale=1.0 / math.sqrt(n * gd)),
        grid=(b, n // tm),
        in_specs=[pl.BlockSpec((tm, 2 * n), lambda bi, i: (i, 0)),
                  pl.BlockSpec((1, 2 * n, d), lambda bi, i: (bi, 0, 0)),
                  pl.BlockSpec((d, d), lambda bi, i: (0, 0)),
                  pl.BlockSpec((1, tm, d), lambda bi, i: (bi, i, 0)), per_b] + moe_in,
        out_specs=[pl.BlockSpec((1, tm, d), lambda bi, i: (bi, i, 0))] + moe_out,
        out_shape=[jax.ShapeDtypeStruct((b, n, d), F32), jax.ShapeDtypeStruct((b, n, d), BF16),
                   jax.ShapeDtypeStruct((b, N_EXPERTS, n), F32)],
        compiler_params=_cparams(("parallel", "arbitrary"), VMEM_LIMIT),
        name="fourier_pos",
    )(wp, z, w_out, x, gate, *moe)


def _excl_cumsum_lanes(mask, tri):
    e, n = mask.shape
    mf = jnp.where(mask, 1.0, 0.0)
    offset = jnp.zeros((e, 1), F32)
    parts = []
    for blk in range(n // LANES):
        part = mf[:, blk * LANES:(blk + 1) * LANES]
        parts.append(jnp.dot(part.astype(BF16), tri, preferred_element_type=F32) + offset)
        offset = offset + jnp.sum(part, axis=1, keepdims=True)
    return jnp.concatenate(parts, axis=1)


def _moe_route_kernel(lg_ref, tri_ref, aff_ref, slot_ref, *, cap):
    lg = lg_ref[...]
    e = jnp.exp(lg - jnp.max(lg, axis=1, keepdims=True))
    aff3 = e / jnp.sum(e, axis=1, keepdims=True)
    aff_ref[...] = aff3
    b, n_exp, n = lg.shape
    aff = aff3.reshape(b * n_exp, n)

    def as_float(bits):
        return lax.bitcast_convert_type(bits, F32)

    def step(it, thr):
        cand = thr | jnp.left_shift(jnp.int32(1), 30 - it)
        cnt = jnp.sum((aff >= as_float(cand)).astype(jnp.int32), axis=1, keepdims=True)
        return jnp.where(cnt >= cap, cand, thr)

    thr = lax.fori_loop(0, 31, step, jnp.zeros((b * n_exp, 1), jnp.int32))
    gt = aff >= as_float(thr + 1)
    eq = (aff >= as_float(thr)) & jnp.logical_not(gt)
    need = (cap - jnp.sum(gt.astype(jnp.int32), axis=1, keepdims=True)).astype(F32)
    tri = tri_ref[...]
    sel = gt | (eq & (_excl_cumsum_lanes(eq, tri) < need))
    pos = _excl_cumsum_lanes(sel, tri).astype(jnp.int32)
    slot_ref[...] = jnp.where(sel, pos, -1).reshape(b, n_exp, n)


def _moe_gather_kernel(slot_ref, aff_ref, hx_ref, xs_ref, ta_ref, *, cap, ge):
    e0 = pl.multiple_of(pl.program_id(1) * ge, ge)
    n = slot_ref.shape[2]
    rows = lax.broadcasted_iota(jnp.int32, (cap, n), 0)
    hits, tas = [], []
    for k in range(ge):
        hit = rows == slot_ref[0, pl.ds(e0 + k, 1), :]
        hits.append(jnp.where(hit, 1.0, 0.0).astype(BF16))
        tas.append(jnp.sum(jnp.where(hit, aff_ref[0, pl.ds(e0 + k, 1), :], 0.0), axis=1, keepdims=True))
    onehot = jnp.concatenate(hits, axis=0) if ge > 1 else hits[0]
    xs = jnp.dot(onehot, hx_ref[0], preferred_element_type=F32).astype(BF16)
    xs_ref[0] = xs.reshape(ge, cap, xs.shape[-1])
    ta = jnp.concatenate(tas, axis=0) if ge > 1 else tas[0]
    ta_ref[0] = jnp.broadcast_to(ta, (ge * cap, LANES)).reshape(ge, cap, LANES)


def _moe_ffn_kernel(xs_ref, ta_ref, wg_ref, wu_ref, wd_ref, ys_ref, wgb, wub, wdb):
    @pl.when(pl.program_id(1) == 0)
    def _():
        wgb[...] = wg_ref[0, 0].astype(BF16)
        wub[...] = wu_ref[0, 0].astype(BF16)
        wdb[...] = wd_ref[0, 0].astype(BF16)

    bb, _, cap, d = xs_ref.shape
    xs = xs_ref[...].reshape(bb * cap, d)
    ta = ta_ref[...].reshape(bb * cap, LANES)[:, :1]
    gate = jnp.dot(xs, wgb[...], preferred_element_type=F32)
    up = jnp.dot(xs, wub[...], preferred_element_type=F32)
    hid = (gate * jax.nn.sigmoid(gate) * up).astype(BF16)
    y = jnp.dot(hid, wdb[...], preferred_element_type=F32) * ta
    ys_ref[...] = y.astype(BF16).reshape(bb, 1, cap, d)


def _moe_scatter_kernel(slot_ref, ys_ref, x_ref, gate_ref, *rest, cap, final):
    if final:
        fg_ref, o_ref = rest
    else:
        (o_ref,) = rest
    slot = slot_ref[0]
    n_exp = slot.shape[1]
    if cap % LANES == 0:
        want = lax.broadcasted_iota(jnp.int32, (1, cap), 1)
        hit = jnp.concatenate([jnp.where(slot[:, e:e + 1] == want, 1.0, 0.0).astype(BF16) for e in range(n_exp)],
                              axis=1)
    else:
        col = lax.broadcasted_iota(jnp.int32, (n_exp, n_exp * cap), 1)
        row = lax.broadcasted_iota(jnp.int32, (n_exp, n_exp * cap), 0)
        spread = jnp.where(col // cap == row, 1.0, 0.0).astype(BF16)
        want = (lax.broadcasted_iota(jnp.int32, (1, n_exp * cap), 1) % cap).astype(F32)
        ids = jnp.dot(slot.astype(F32).astype(BF16), spread, preferred_element_type=F32)
        hit = jnp.where(ids == want, 1.0, 0.0).astype(BF16)
    ys = ys_ref[0].reshape(n_exp * cap, ys_ref.shape[-1])
    out = x_ref[0] + gate_ref[0] * jnp.dot(hit, ys, preferred_element_type=F32)
    if final:
        out = _rms(out) * fg_ref[...]
    o_ref[0] = out


def moe_layer(x, hx, logits, layer, w_gate, w_up, w_down, gate, final_g=None):
    b, n, d = x.shape
    n_exp = logits.shape[1]
    cap = EC_CAPACITY_FACTOR * n // n_exp
    per_b2 = pl.BlockSpec((1, 1, d), lambda bi, i: (bi, 0, 0))
    tri = (np.arange(LANES)[:, None] < np.arange(LANES)[None, :]).astype(np.float32)
    aff, slot = pl.pallas_call(
        functools.partial(_moe_route_kernel, cap=cap),
        grid=(1,),
        in_specs=[pl.BlockSpec((b, n_exp, n), lambda i: (0, 0, 0)),
                  pl.BlockSpec((LANES, LANES), lambda i: (0, 0))],
        out_specs=[pl.BlockSpec((b, n_exp, n), lambda i: (0, 0, 0))] * 2,
        out_shape=[jax.ShapeDtypeStruct((b, n_exp, n), F32), jax.ShapeDtypeStruct((b, n_exp, n), jnp.int32)],
        compiler_params=_cparams(("arbitrary",)),
        name="moe_route",
    )(logits, jnp.asarray(tri, BF16))

    rows_per_step = 1024
    ge = max(1, min(n_exp, 2 * rows_per_step // cap))
    xs, ta = pl.pallas_call(
        functools.partial(_moe_gather_kernel, cap=cap, ge=ge),
        grid=(b, n_exp // ge),
        in_specs=[pl.BlockSpec((1, n_exp, n), lambda bi, e: (bi, 0, 0)),
                  pl.BlockSpec((1, n_exp, n), lambda bi, e: (bi, 0, 0)),
                  pl.BlockSpec((1, n, d), lambda bi, e: (bi, 0, 0))],
        out_specs=[pl.BlockSpec((1, ge, cap, d), lambda bi, e: (bi, e, 0, 0)),
                   pl.BlockSpec((1, ge, cap, LANES), lambda bi, e: (bi, e, 0, 0))],
        out_shape=[jax.ShapeDtypeStruct((b, n_exp, cap, d), BF16),
                   jax.ShapeDtypeStruct((b, n_exp, cap, LANES), F32)],
        compiler_params=_cparams(("parallel", "arbitrary"), VMEM_LIMIT),
        name="moe_gather",
    )(slot, aff, hx)

    bb = max(1, min(b, rows_per_step // cap))
    f = w_gate.shape[-1]
    ys = pl.pallas_call(
        _moe_ffn_kernel,
        grid=(n_exp, b // bb),
        in_specs=[pl.BlockSpec((bb, 1, cap, d), lambda e, bi: (bi, e, 0, 0)),
                  pl.BlockSpec((bb, 1, cap, LANES), lambda e, bi: (bi, e, 0, 0)),
                  pl.BlockSpec((1, 1, d, f), lambda e, bi: (layer, e, 0, 0)),
                  pl.BlockSpec((1, 1, d, f), lambda e, bi: (layer, e, 0, 0)),
                  pl.BlockSpec((1, 1, f, d), lambda e, bi: (layer, e, 0, 0))],
        out_specs=pl.BlockSpec((bb, 1, cap, d), lambda e, bi: (bi, e, 0, 0)),
        out_shape=jax.ShapeDtypeStruct((b, n_exp, cap, d), BF16),
        scratch_shapes=[pltpu.VMEM((d, f), BF16), pltpu.VMEM((d, f), BF16), pltpu.VMEM((f, d), BF16)],
        compiler_params=_cparams(("parallel", "arbitrary"), VMEM_LIMIT),
        name="moe_ffn",
    )(xs, ta, w_gate, w_up, w_down)

    slot_t = jnp.swapaxes(slot, 1, 2)
    tn = min(1024, n)
    args = [slot_t, ys, x, gate]
    specs = [pl.BlockSpec((1, tn, n_exp), lambda bi, i: (bi, i, 0)),
             pl.BlockSpec((1, n_exp, cap, d), lambda bi, i: (bi, 0, 0, 0)),
             pl.BlockSpec((1, tn, d), lambda bi, i: (bi, i, 0)), per_b2]
    if final_g is not None:
        args.append(final_g)
        specs.append(pl.BlockSpec((1, d), lambda bi, i: (0, 0)))
    return pl.pallas_call(
        functools.partial(_moe_scatter_kernel, cap=cap, final=final_g is not None),
        grid=(b, n // tn),
        in_specs=specs,
        out_specs=pl.BlockSpec((1, tn, d), lambda bi, i: (bi, i, 0)),
        out_shape=jax.ShapeDtypeStruct((b, n, d), F32),
        compiler_params=_cparams(("parallel", "arbitrary"), VMEM_LIMIT),
        name="moe_scatter",
    )(*args)


def kernel(x, c, ctx, c_ctx, ada_w, ada_b, mixer_norm_g, moe_norm_g, router_w, moe_w_gate, moe_w_up, moe_w_down, da_w_in, da_lambda_q1, da_lambda_k1, da_lambda_q2, da_lambda_k2, da_subln_g, da_w_out, gm_w_in, gm_v_g, gm_w_s, gm_b_s, gm_w_out, na_w_in, na_rpb, na_w_out, fn_w_out, final_norm_g):
    b, n, d = x.shape
    rows = n // GRID_W
    readers = [i for i in range(DEPTH) if i % N_MIXERS in CTX_READERS]
    last_reader = max(readers) if readers else -1

    r_pad = -(-(b + 1) // 8) * 8
    cond = jnp.zeros((r_pad, d), F32).at[:b].set(c).at[b].set(c_ctx)
    mods = ada_params(cond, ada_w, ada_b)

    def lat_mod(i, k):
        return mods[i, :b, k * d:(k + 1) * d].reshape(b, 1, d)

    def ctx_mod(i, k):
        return jnp.broadcast_to(mods[i, b, k * d:(k + 1) * d].reshape(1, 1, d), (b, 1, d))

    rope = _rope_tables(n)
    perm = _da_perm()

    for i in range(DEPTH):
        kind, j = i % N_MIXERS, i // N_MIXERS
        need_ctx = i <= last_reader
        update_ctx = i < last_reader
        mg = mixer_norm_g[i].reshape(1, d)
        xnorm = (mg, lat_mod(i, 0), lat_mod(i, 1))
        cnorm = (mg, ctx_mod(i, 0), ctx_mod(i, 1)) if need_ctx else None
        g1 = lat_mod(i, 2)
        wr = _router_split(router_w[i])
        moe_g = moe_norm_g[i].reshape(1, d)
        xmoe = (moe_g, lat_mod(i, 3), lat_mod(i, 4), wr)
        cmoe = (moe_g, ctx_mod(i, 3), ctx_mod(i, 4), wr) if update_ctx else None

        if kind == MIX_DIFF:
            lam_init = 0.8 - 0.6 * math.exp(-0.3 * i)
            cols = np.arange(2 * DA_WIDTH).reshape(2 * DA_HEADS, LANES)[:, perm].reshape(-1)
            w_in = jnp.concatenate([da_w_in[j][:, cols], da_w_in[j][:, 2 * DA_WIDTH:]], axis=1).astype(BF16)
            w_out = da_w_out[j].astype(BF16)
            lams = [t[j].reshape(1, DA_HEAD_DIM) for t in (da_lambda_q1, da_lambda_k1, da_lambda_q2, da_lambda_k2)]
            sg = da_subln_g[j].reshape(DA_V_DIM, 1)
            nb = DA_WIDTH // LANES
            qkv = linear(x, w_in, norm=xnorm, rope=rope, rope_tiles=2)
            if update_ctx:
                qkv_c = linear(ctx, w_in, norm=cnorm)
                ctx_src = (qkv_c, nb, 2 * nb)
            else:
                ctx_src = (linear(ctx, w_in[:, DA_WIDTH:], norm=cnorm), 0, nb)
            o = diff_attention(qkv, [ctx_src, (qkv, nb, 2 * nb)], lams, sg, lam_init)
            x, hx, logits = linear(o, w_out, res=(x, g1), moe=xmoe, out_dtype=F32)
            if update_ctx:
                oc = diff_attention(qkv_c, [ctx_src], lams, sg, lam_init, heads_per_step=DA_HEADS)
                ctx, hc, logits_c = linear(oc, w_out, res=(ctx, ctx_mod(i, 2)), moe=cmoe, out_dtype=F32)
        elif kind == MIX_GMLP:
            w_in = gm_w_in[j].astype(BF16)
            w_out = gm_w_out[j].astype(BF16)
            vg = gm_v_g[j].reshape(1, GM_WIDTH)
            w_s = gm_w_s[j].astype(BF16)
            b_s_t = jnp.transpose(gm_b_s[j])
            x, hx, logits = gmlp_layer(x, xnorm, w_in, vg, w_s, b_s_t, w_out, g1, xmoe)
            if update_ctx:
                ctx, hc, logits_c = gmlp_layer(ctx, cnorm, w_in, vg, w_s, b_s_t, w_out, ctx_mod(i, 2), cmoe)
        elif kind == MIX_NATTEN:
            w_in = na_w_in[j].astype(BF16)
            w_out = na_w_out[j].astype(BF16)
            qkv = linear(x, w_in, norm=xnorm)
            kv_c = linear(ctx, w_in[:, NA_WIDTH:], norm=cnorm)
            o = natten(qkv, kv_c, _natten_bias(na_rpb[j], rows))
            x, hx, logits = linear(o, w_out, res=(x, g1), moe=xmoe, out_dtype=F32)
            if update_ctx:
                raise NotImplementedError("context update after a neighbourhood layer")
        else:
            x, hx, logits = fourier_layer(x, xnorm, fn_w_out[j].astype(BF16), g1, xmoe)
            if update_ctx:
                ctx, hc, logits_c = fourier_layer(ctx, cnorm, fn_w_out[j].astype(BF16), ctx_mod(i, 2), cmoe)

        last = i == DEPTH - 1
        x = moe_layer(x, hx, logits, i, moe_w_gate, moe_w_up, moe_w_down, lat_mod(i, 5),
                      final_g=final_norm_g.reshape(1, d) if last else None)
        if update_ctx:
            ctx = moe_layer(ctx, hc, logits_c, i, moe_w_gate, moe_w_up, moe_w_down, ctx_mod(i, 5))
    return x
```

```python
import functools
import math

import numpy as np
import jax
import jax.numpy as jnp
from jax import lax
from jax.experimental import pallas as pl
from jax.experimental.pallas import tpu as pltpu

F32 = jnp.float32
BF16 = jnp.bfloat16

D_MODEL = 1024
DEPTH = 4
GRID_W = 64
N_MIXERS = 4
MIX_DIFF, MIX_GMLP, MIX_NATTEN, MIX_FOURIER = 0, 1, 2, 3
CTX_READERS = (MIX_DIFF, MIX_NATTEN)
EPS = 1e-6
NEG_INF = -1e30
ROPE_BASE = 10000.0

DA_HEADS = 8
DA_HEAD_DIM = 64
DA_V_DIM = 2 * DA_HEAD_DIM
DA_WIDTH = DA_HEADS * DA_V_DIM

GM_CHUNK = 128
GM_GROUPS = 8
GM_WIDTH = 2 * D_MODEL
GM_GROUP_DIM = GM_WIDTH // GM_GROUPS

NA_HEADS = 16
NA_HEAD_DIM = D_MODEL // NA_HEADS
NA_WIDTH = NA_HEADS * NA_HEAD_DIM
NA_WIN_R = 8
NA_WIN_C = 16
NA_ROW_BLOCK = 4
NA_SLAB_ROWS = NA_ROW_BLOCK + NA_WIN_R

FN_GROUPS = 4
DFT_SPLIT = 256

N_EXPERTS = 16
EC_CAPACITY_FACTOR = 2

LANES = 128
VMEM_LIMIT = 56 * 1024 * 1024


def _cparams(sem, vmem=None):
    return pltpu.CompilerParams(dimension_semantics=sem, vmem_limit_bytes=vmem)


def _rms(x):
    return x * lax.rsqrt(jnp.mean(x * x, axis=-1, keepdims=True) + EPS)


def _ada_kernel(c_ref, w_ref, b_ref, o_ref):
    c = c_ref[...]
    h = (c * jax.nn.sigmoid(c)).astype(BF16)
    o_ref[0] = jnp.dot(h, w_ref[0].astype(BF16), preferred_element_type=F32) + b_ref[0]


def ada_params(cond, ada_w, ada_b):
    r = cond.shape[0]
    depth, d, n_out = ada_w.shape
    tn = 1024
    return pl.pallas_call(
        _ada_kernel,
        grid=(depth, n_out // tn),
        in_specs=[pl.BlockSpec((r, d), lambda l, j: (0, 0)),
                  pl.BlockSpec((1, d, tn), lambda l, j: (l, 0, j)),
                  pl.BlockSpec((1, 1, tn), lambda l, j: (l, 0, j))],
        out_specs=pl.BlockSpec((1, r, tn), lambda l, j: (l, 0, j)),
        out_shape=jax.ShapeDtypeStruct((depth, r, n_out), F32),
        compiler_params=_cparams(("parallel", "parallel")),
        name="ada_params",
    )(cond, ada_w, ada_b.reshape(depth, 1, n_out))


def _moe_prologue(xn, mg_ref, sh_ref, sc_ref, wr_ref, hx_ref, lg_ref):
    h = _rms(xn) * mg_ref[...]
    h = h * (1.0 + sc_ref[0]) + sh_ref[0]
    h_hi = h.astype(BF16)
    h_lo = (h - h_hi.astype(F32)).astype(BF16)
    hx_ref[0] = h_hi
    a = jnp.dot(h_hi, wr_ref[...], preferred_element_type=F32)
    b = jnp.dot(h_lo, wr_ref[:, :LANES], preferred_element_type=F32)
    lt = a[:, :LANES] + a[:, LANES:] + b
    lg_ref[0] = jnp.transpose(lt)[:lg_ref.shape[1]]


def _router_split(w_router):
    w = jnp.pad(w_router, ((0, 0), (0, LANES - w_router.shape[1])))
    hi = w.astype(BF16)
    return jnp.concatenate([hi, (w - hi.astype(F32)).astype(BF16)], axis=1)


def _moe_prologue_specs(d, tm, n_exp):
    in_specs = [pl.BlockSpec((1, d), lambda bi, i: (0, 0)),
                pl.BlockSpec((1, 1, d), lambda bi, i: (bi, 0, 0)),
                pl.BlockSpec((1, 1, d), lambda bi, i: (bi, 0, 0)),
                pl.BlockSpec((d, 2 * LANES), lambda bi, i: (0, 0))]
    out_specs = [pl.BlockSpec((1, tm, d), lambda bi, i: (bi, i, 0)),
                 pl.BlockSpec((1, n_exp, tm), lambda bi, i: (bi, 0, i))]
    return in_specs, out_specs


def _linear_kernel(*refs, has_norm, has_res, has_moe, rope_tiles, tn):
    it = iter(refs)
    x_ref = next(it)
    if has_norm:
        g_ref, sh_ref, sc_ref = next(it), next(it), next(it)
    w_ref = next(it)
    if rope_tiles:
        cos_ref, sin_ref = next(it), next(it)
    if has_res:
        res_ref, gate_ref = next(it), next(it)
    if has_moe:
        moe_in = [next(it) for _ in range(4)]
    o_ref = next(it)

    if has_norm:
        h = _rms(x_ref[0]) * g_ref[...]
        hb = (h * (1.0 + sc_ref[0]) + sh_ref[0]).astype(BF16)
    else:
        hb = x_ref[0].astype(BF16)
    for j in range(w_ref.shape[1] // tn):
        cols = slice(j * tn, (j + 1) * tn)
        y = jnp.dot(hb, w_ref[:, cols], preferred_element_type=F32)
        if j < rope_tiles:
            cos, sin = cos_ref[...], sin_ref[...]
            for s in range(tn // LANES):
                seg = y[:, s * LANES:(s + 1) * LANES]
                rot = pltpu.roll(seg, LANES // 2, axis=1)
                lanes = slice(j * tn + s * LANES, j * tn + (s + 1) * LANES)
                o_ref[0, :, lanes] = (seg * cos + rot * sin).astype(o_ref.dtype)
        elif has_res:
            o_ref[0, :, cols] = (res_ref[0, :, cols] + gate_ref[0, :, cols] * y).astype(o_ref.dtype)
        else:
            o_ref[0, :, cols] = y.astype(o_ref.dtype)
    if has_moe:
        _moe_prologue(o_ref[0], *moe_in, next(it), next(it))


def linear(x, w, *, norm=None, res=None, rope=None, rope_tiles=0, moe=None, out_dtype=BF16, tm=1024, tn=1024):
    b, n, k = x.shape
    m = w.shape[1]
    tm = min(tm, n)
    tn = min(tn, m)
    args, specs = [x], [pl.BlockSpec((1, tm, k), lambda bi, i: (bi, i, 0))]
    if norm is not None:
        g, sh, sc = norm
        args += [g, sh, sc]
        specs += [pl.BlockSpec((1, k), lambda bi, i: (0, 0)),
                  pl.BlockSpec((1, 1, k), lambda bi, i: (bi, 0, 0)),
                  pl.BlockSpec((1, 1, k), lambda bi, i: (bi, 0, 0))]
    args.append(w)
    specs.append(pl.BlockSpec((k, m), lambda bi, i: (0, 0)))
    if rope is not None:
        args += list(rope)
        specs += [pl.BlockSpec((tm, LANES), lambda bi, i: (i, 0))] * 2
    if res is not None:
        r, gate = res
        args += [r, gate]
        specs += [pl.BlockSpec((1, tm, m), lambda bi, i: (bi, i, 0)),
                  pl.BlockSpec((1, 1, m), lambda bi, i: (bi, 0, 0))]
    out_specs = [pl.BlockSpec((1, tm, m), lambda bi, i: (bi, i, 0))]
    out_shape = [jax.ShapeDtypeStruct((b, n, m), out_dtype)]
    if moe is not None:
        moe_in, moe_out = _moe_prologue_specs(m, tm, N_EXPERTS)
        args += list(moe)
        specs += moe_in
        out_specs += moe_out
        out_shape += [jax.ShapeDtypeStruct((b, n, m), BF16), jax.ShapeDtypeStruct((b, N_EXPERTS, n), F32)]
    kern = functools.partial(_linear_kernel, has_norm=norm is not None, has_res=res is not None,
                             has_moe=moe is not None, rope_tiles=rope_tiles if rope is not None else 0, tn=tn)
    out = pl.pallas_call(
        kern,
        grid=(b, n // tm),
        in_specs=specs,
        out_specs=out_specs,
        out_shape=out_shape,
        compiler_params=_cparams(("parallel", "parallel"), VMEM_LIMIT),
        name="linear",
    )(*args)
    return out if moe is not None else out[0]


DA_VT_ROWS = DA_V_DIM + 16
ATT_TQ = 256
ATT_KC = 768


def _interleave(stage_a, stage_b):
    for i in range(max(len(stage_a), len(stage_b))):
        if i < len(stage_a):
            stage_a[i]()
        if i < len(stage_b):
            stage_b[i]()


def _diff_attn_kernel(*refs, n_src, lam_init):
    for hh in range(refs[0].shape[2] // LANES):
        _diff_attn_head(slice(hh * LANES, (hh + 1) * LANES), refs, n_src, lam_init)


def _diff_attn_head(hl, refs, n_src, lam_init):
    q_ref = refs[0]
    k_refs = refs[1:1 + n_src]
    v_refs = refs[1 + n_src:1 + 2 * n_src]
    lq1_ref, lk1_ref, lq2_ref, lk2_ref, g_ref, o_ref, k_all, vt_all, s_scr, e_scr = refs[1 + 2 * n_src:]

    row = 0
    for k_ref, v_ref in zip(k_refs, v_refs):
        n = k_ref.shape[1]
        k_all[row:row + n, :] = k_ref[0, :, hl]
        vt_all[:DA_V_DIM, row:row + n] = jnp.transpose(v_ref[0, :, hl].astype(F32)).astype(BF16)
        row += n
    vt_all[DA_V_DIM:, :] = jnp.ones((DA_VT_ROWS - DA_V_DIM, vt_all.shape[1]), BF16)

    lam = (jnp.exp(jnp.sum(lq1_ref[...] * lk1_ref[...], keepdims=True))
           - jnp.exp(jnp.sum(lq2_ref[...] * lk2_ref[...], keepdims=True)) + lam_init)
    lane = lax.broadcasted_iota(jnp.int32, (1, LANES), 1)
    first = (lane % DA_HEAD_DIM) < (DA_HEAD_DIM // 2)
    dims = (((1,), (1,)), ((), ()))
    c = DA_HEAD_DIM ** -0.5 * math.log2(math.e)
    nk = k_all.shape[0]
    tq = min(ATT_TQ, q_ref.shape[1])
    n_tiles = q_ref.shape[1] // tq
    chunks = [slice(r, min(r + ATT_KC, nk)) for r in range(0, nk, ATT_KC)]
    state = [dict(m=None) for _ in range(n_tiles)]

    def scores(t):
        q = q_ref[0, t * tq:(t + 1) * tq, hl]
        zero = jnp.zeros_like(q)
        q2 = jnp.concatenate([jnp.where(first, q, zero), jnp.where(first, zero, q)], axis=0)

        def chunk(rows):
            s = lax.dot_general(k_all[rows, :], q2, dims, preferred_element_type=F32) * c
            s_scr[t % 2, rows, :] = s
            m = jnp.max(s, axis=0, keepdims=True)
            state[t]["m"] = m if state[t]["m"] is None else jnp.maximum(state[t]["m"], m)

        return [functools.partial(chunk, rows) for rows in chunks]

    def exps(t):
        def chunk(rows):
            e_scr[t % 2, rows, :] = jnp.exp2(s_scr[t % 2, rows, :] - state[t]["m"]).astype(BF16)

        return [functools.partial(chunk, rows) for rows in chunks]

    def values(t):
        ot = jnp.dot(vt_all[...], e_scr[t % 2], preferred_element_type=F32)
        ot = ot[:DA_V_DIM] / ot[DA_V_DIM:DA_V_DIM + 1]
        ot = ot[:, :tq] - lam * ot[:, tq:]
        ot = ot * lax.rsqrt(jnp.mean(ot * ot, axis=0, keepdims=True) + EPS) * g_ref[...] * (1.0 - lam_init)
        o_ref[0, t * tq:(t + 1) * tq, hl] = jnp.transpose(ot).astype(o_ref.dtype)

    for t in range(n_tiles + 2):
        if 0 <= t - 2 < n_tiles:
            values(t - 2)
        _interleave(scores(t) if t < n_tiles else [],
                    (exps(t - 1) if 0 <= t - 1 < n_tiles else []))


def diff_attention(q_arr, srcs, lams, subln_g, lam_init, heads_per_step=1):
    b, nq, _ = q_arr.shape
    nk = sum(a.shape[1] for a, _, _ in srcs)
    tq = min(ATT_TQ, nq)
    hps = heads_per_step
    w = hps * LANES
    assert all(kc % hps == 0 and vc % hps == 0 for _, kc, vc in srcs)
    vec = pl.BlockSpec((1, DA_HEAD_DIM), lambda bi, h: (0, 0))
    k_specs = [pl.BlockSpec((1, a.shape[1], w), lambda bi, h, c0=kc // hps: (bi, 0, c0 + h)) for a, kc, _ in srcs]
    v_specs = [pl.BlockSpec((1, a.shape[1], w), lambda bi, h, c0=vc // hps: (bi, 0, c0 + h)) for a, _, vc in srcs]
    arrs = [a for a, _, _ in srcs]
    return pl.pallas_call(
        functools.partial(_diff_attn_kernel, n_src=len(srcs), lam_init=lam_init),
        grid=(b, DA_HEADS // hps),
        in_specs=[pl.BlockSpec((1, nq, w), lambda bi, h: (bi, 0, h))] + k_specs + v_specs
                 + [vec, vec, vec, vec, pl.BlockSpec((DA_V_DIM, 1), lambda bi, h: (0, 0))],
        out_specs=pl.BlockSpec((1, nq, w), lambda bi, h: (bi, 0, h)),
        out_shape=jax.ShapeDtypeStruct((b, nq, DA_WIDTH), BF16),
        scratch_shapes=[pltpu.VMEM((nk, LANES), BF16), pltpu.VMEM((DA_VT_ROWS, nk), BF16),
                        pltpu.VMEM((2, nk, 2 * tq), F32), pltpu.VMEM((2, nk, 2 * tq), BF16)],
        compiler_params=_cparams(("parallel", "parallel"), VMEM_LIMIT),
        name="diff_attention",
    )(q_arr, *arrs, *arrs, *lams, subln_g)


def _da_perm():
    perm = np.zeros(LANES, np.int32)
    for l in range(LANES):
        part, within = divmod(l, 64)
        j, rem = divmod(within, 32)
        seg, i = divmod(rem, 16)
        perm[l] = j * 64 + seg * 32 + part * 16 + i
    return perm


def _rope_tables(n_tok):
    t = jnp.arange(n_tok, dtype=jnp.int32)
    n_freq = DA_HEAD_DIM // 4
    inv = ROPE_BASE ** (-jnp.arange(n_freq, dtype=F32) / n_freq)
    ang_r = (t // GRID_W).astype(F32)[:, None] * inv
    ang_c = (t % GRID_W).astype(F32)[:, None] * inv
    cos32 = jnp.concatenate([jnp.cos(ang_r), jnp.cos(ang_c)], axis=1)
    sin32 = jnp.concatenate([jnp.sin(ang_r), jnp.sin(ang_c)], axis=1)
    cos = jnp.tile(cos32, (1, 4))
    sin = jnp.concatenate([-sin32, -sin32, sin32, sin32], axis=1)
    return cos, sin


def _gmlp_kernel(x_ref, g_ref, sh_ref, sc_ref, win_ref, vg_ref, ws_ref, bs_ref, wout_ref, gate_ref,
                 mg_ref, sh2_ref, sc2_ref, wr_ref, o_ref, hx_ref, lg_ref, *, tm):
    x = x_ref[0]
    h = _rms(x) * g_ref[...]
    h = (h * (1.0 + sc_ref[0]) + sh_ref[0]).astype(BF16)
    uv = jax.nn.gelu(jnp.dot(h, win_ref[...], preferred_element_type=F32), approximate=True)
    u = uv[:, :GM_WIDTH]
    v = (_rms(uv[:, GM_WIDTH:]) * vg_ref[...]).astype(BF16)
    rows = []
    for c in range(tm // GM_CHUNK):
        cols = []
        for gi in range(GM_GROUPS):
            vv = v[c * GM_CHUNK:(c + 1) * GM_CHUNK, gi * GM_GROUP_DIM:(gi + 1) * GM_GROUP_DIM]
            sv = jnp.dot(ws_ref[gi], vv, preferred_element_type=F32) + bs_ref[:, gi:gi + 1]
            cols.append(sv)
        rows.append(jnp.concatenate(cols, axis=1))
    sv = jnp.concatenate(rows, axis=0) if len(rows) > 1 else rows[0]
    y = jnp.dot((u * sv).astype(BF16), wout_ref[...], preferred_element_type=F32)
    xn = x + gate_ref[0] * y
    o_ref[0] = xn
    _moe_prologue(xn, mg_ref, sh2_ref, sc2_ref, wr_ref, hx_ref, lg_ref)


def gmlp_layer(x, norm, w_in, v_g, w_s, b_s_t, w_out, gate, moe, tm=512):
    b, n, d = x.shape
    tm = min(tm, n)
    g, sh, sc = norm
    moe_in, moe_out = _moe_prologue_specs(d, tm, N_EXPERTS)
    full = lambda shape: pl.BlockSpec(shape, lambda bi, i: (0,) * len(shape))
    per_b = pl.BlockSpec((1, 1, d), lambda bi, i: (bi, 0, 0))
    return pl.pallas_call(
        functools.partial(_gmlp_kernel, tm=tm),
        grid=(b, n // tm),
        in_specs=[pl.BlockSpec((1, tm, d), lambda bi, i: (bi, i, 0)),
                  full((1, d)), per_b, per_b,
                  full(w_in.shape), full((1, GM_WIDTH)), full(w_s.shape), full(b_s_t.shape),
                  full(w_out.shape), per_b] + moe_in,
        out_specs=[pl.BlockSpec((1, tm, d), lambda bi, i: (bi, i, 0))] + moe_out,
        out_shape=[jax.ShapeDtypeStruct((b, n, d), F32), jax.ShapeDtypeStruct((b, n, d), BF16),
                   jax.ShapeDtypeStruct((b, N_EXPERTS, n), F32)],
        compiler_params=_cparams(("parallel", "arbitrary"), VMEM_LIMIT),
        name="gmlp_layer",
    )(x, g, sh, sc, w_in, v_g, w_s, b_s_t, w_out, gate, *moe)


NA_VT_ROWS = LANES + 16


def _natten_kernel(q_ref, k_ref, v_ref, kc_ref, vc_ref, bias_ref, o_ref, vt_all, vct, s_scr, e_scr, *, n_blocks):
    tq = NA_ROW_BLOCK * GRID_W
    slab_blocks = NA_SLAB_ROWS // NA_ROW_BLOCK
    slab = slab_blocks * tq
    lc = kc_ref.shape[1]
    for blk in range(n_blocks):
        vt_all[:LANES, blk * tq:(blk + 1) * tq] = jnp.transpose(
            v_ref[0, blk * tq:(blk + 1) * tq, :].astype(F32)).astype(BF16)
    vt_all[LANES:, :] = jnp.ones((NA_VT_ROWS - LANES, vt_all.shape[1]), BF16)
    vct[:LANES, :] = jnp.transpose(vc_ref[0].astype(F32)).astype(BF16)
    vct[LANES:, :] = jnp.ones((NA_VT_ROWS - LANES, lc), BF16)

    lane = lax.broadcasted_iota(jnp.int32, (1, LANES), 1)
    left = lane < NA_HEAD_DIM
    dims = (((1,), (1,)), ((), ()))
    c = NA_HEAD_DIM ** -0.5 * math.log2(math.e)
    state = [dict(m=None) for _ in range(n_blocks)]
    key0 = [min(max(t - 1, 0), n_blocks - slab_blocks) * tq for t in range(n_blocks)]
    kind = [0 if t == 0 else 2 if t == n_blocks - 1 else 1 for t in range(n_blocks)]

    def scores(t):
        q = q_ref[0, t * tq:(t + 1) * tq, :]
        zero = jnp.zeros_like(q)
        q2 = jnp.concatenate([jnp.where(left, q, zero), jnp.where(left, zero, q)], axis=0)

        def local():
            k = k_ref[0, key0[t]:key0[t] + slab, :]
            s = lax.dot_general(k, q2, dims, preferred_element_type=F32) * c + bias_ref[0, kind[t]]
            s_scr[t % 2, :slab, :] = s
            state[t]["m"] = jnp.max(s, axis=0, keepdims=True)

        def context():
            s = lax.dot_general(kc_ref[0], q2, dims, preferred_element_type=F32) * c
            s_scr[t % 2, slab:, :] = s
            state[t]["m"] = jnp.maximum(state[t]["m"], jnp.max(s, axis=0, keepdims=True))

        return [local, context]

    def exps(t):
        def chunk(rows):
            e_scr[t % 2, rows, :] = jnp.exp2(s_scr[t % 2, rows, :] - state[t]["m"]).astype(BF16)

        return [functools.partial(chunk, slice(0, slab)), functools.partial(chunk, slice(slab, slab + lc))]

    def values(t):
        vt = jnp.concatenate([vt_all[:, key0[t]:key0[t] + slab], vct[...]], axis=1)
        ot = jnp.dot(vt, e_scr[t % 2], preferred_element_type=F32)
        ot = ot[:LANES] / ot[LANES:LANES + 1]
        pair = jnp.concatenate([ot[:NA_HEAD_DIM, :tq], ot[NA_HEAD_DIM:, tq:]], axis=0)
        o_ref[0, t * tq:(t + 1) * tq, :] = jnp.transpose(pair).astype(o_ref.dtype)

    for t in range(n_blocks + 2):
        if 0 <= t - 2 < n_blocks:
            values(t - 2)
        _interleave(scores(t) if t < n_blocks else [],
                    (exps(t - 1) if 0 <= t - 1 < n_blocks else []))


def _natten_bias(rpb, rows):
    n_heads = rpb.shape[0]
    n_blocks = rows // NA_ROW_BLOCK
    n_dr, n_dc = 2 * NA_WIN_R - 1, 2 * NA_WIN_C - 1
    rpb = rpb * math.log2(math.e)
    span = 2 * GRID_W - 1
    lo = GRID_W - NA_WIN_C - 1
    u = jnp.pad(rpb, ((0, 0), (0, 0), (lo, span - n_dc - lo)))
    skew = jnp.tile(u, (1, 1, GRID_W))[:, :, :GRID_W * (span - 1)].reshape(n_heads, n_dr, GRID_W, span - 1)
    toep = skew[:, :, :, GRID_W - 2:2 * GRID_W - 2]
    pad = NA_ROW_BLOCK
    toep = jnp.pad(toep, ((0, 0), (pad, pad), (0, 0), (0, 0)))
    i = np.arange(NA_ROW_BLOCK)[:, None, None, None]
    c = np.arange(GRID_W)[None, :, None, None]
    m = np.arange(NA_SLAB_ROWS)[None, None, :, None]
    kc = np.arange(GRID_W)[None, None, None, :]
    win_c0 = np.clip(c - NA_WIN_C // 2, 0, GRID_W - NA_WIN_C)
    col_ok = (kc >= win_c0) & (kc < win_c0 + NA_WIN_C)
    tiles, ok = [], []
    for rb in (0, 1, n_blocks - 1):
        slab0 = int(np.clip(rb - 1, 0, n_blocks - 3)) * NA_ROW_BLOCK
        r = rb * NA_ROW_BLOCK + i
        r0 = np.clip(r - NA_WIN_R // 2, 0, rows - NA_WIN_R)
        kr = slab0 + m
        ok.append(np.broadcast_to((kr >= r0) & (kr < r0 + NA_WIN_R) & col_ok,
                                  (NA_ROW_BLOCK, GRID_W, NA_SLAB_ROWS, GRID_W)))
        for ii in range(NA_ROW_BLOCK):
            off = slab0 - (rb * NA_ROW_BLOCK + ii) + NA_WIN_R - 1 + pad
            tiles.append(toep[:, off:off + NA_SLAB_ROWS])
    tq, slab = NA_ROW_BLOCK * GRID_W, NA_SLAB_ROWS * GRID_W
    bias = jnp.stack(tiles, axis=1).reshape(n_heads // 2, 2, 3, NA_ROW_BLOCK, NA_SLAB_ROWS, GRID_W, GRID_W)
    bias = jnp.transpose(bias, (0, 2, 4, 6, 1, 3, 5)).reshape(n_heads // 2, 3, slab, 2 * tq)
    ok = np.stack(ok).transpose(0, 3, 4, 1, 2).reshape(3, slab, tq)
    ok = np.concatenate([ok, ok], axis=2)
    return jnp.where(ok[None], bias, NEG_INF)


def natten(qkv, kv_ctx, bias):
    b, n, _ = qkv.shape
    lc = kv_ctx.shape[1]
    tq = NA_ROW_BLOCK * GRID_W
    n_blocks = n // tq
    hp = NA_HEADS // 2
    return pl.pallas_call(
        functools.partial(_natten_kernel, n_blocks=n_blocks),
        grid=(hp, b),
        in_specs=[pl.BlockSpec((1, n, LANES), lambda p, bi: (bi, 0, p)),
                  pl.BlockSpec((1, n, LANES), lambda p, bi: (bi, 0, hp + p)),
                  pl.BlockSpec((1, n, LANES), lambda p, bi: (bi, 0, 2 * hp + p)),
                  pl.BlockSpec((1, lc, LANES), lambda p, bi: (bi, 0, p)),
                  pl.BlockSpec((1, lc, LANES), lambda p, bi: (bi, 0, hp + p)),
                  pl.BlockSpec((1, 3, NA_SLAB_ROWS * GRID_W, 2 * tq), lambda p, bi: (p, 0, 0, 0))],
        out_specs=pl.BlockSpec((1, n, LANES), lambda p, bi: (bi, 0, p)),
        out_shape=jax.ShapeDtypeStruct((b, n, NA_WIDTH), BF16),
        scratch_shapes=[pltpu.VMEM((NA_VT_ROWS, n), BF16), pltpu.VMEM((NA_VT_ROWS, lc), BF16),
                        pltpu.VMEM((2, NA_SLAB_ROWS * GRID_W + lc, 2 * tq), F32),
                        pltpu.VMEM((2, NA_SLAB_ROWS * GRID_W + lc, 2 * tq), BF16)],
        compiler_params=_cparams(("parallel", "parallel"), VMEM_LIMIT),
        name="natten",
    )(qkv, qkv, qkv, kv_ctx, kv_ctx, bias)


def _dft_tables(n, sign=1.0):
    k = jnp.arange(n, dtype=jnp.int32)
    w = 2.0 * math.pi / n

    def cs(m):
        ang = ((k[:, None] * m[None, :]) % n).astype(F32) * w
        return jnp.cos(ang), jnp.sin(ang)

    if n <= DFT_SPLIT:
        c, s = cs(k)
        return c, sign * s
    ca, sa = cs(jnp.arange(n // DFT_SPLIT, dtype=jnp.int32) * DFT_SPLIT)
    cb, sb = cs(jnp.arange(DFT_SPLIT, dtype=jnp.int32))
    c = ca[:, :, None] * cb[:, None, :] - sa[:, :, None] * sb[:, None, :]
    s = sa[:, :, None] * cb[:, None, :] + ca[:, :, None] * sb[:, None, :]
    return c.reshape(n, n), (sign * s).reshape(n, n)


def _fourier_chan_kernel(x_ref, g_ref, sh_ref, sc_ref, wc_ref, o_ref):
    h = _rms(x_ref[0]) * g_ref[...]
    h = (h * (1.0 + sc_ref[0]) + sh_ref[0]).astype(BF16)
    gd = D_MODEL // FN_GROUPS
    for gi in range(FN_GROUPS):
        z = jnp.dot(h[:, gi * gd:(gi + 1) * gd], wc_ref[...], preferred_element_type=F32).astype(BF16)
        o_ref[0, 0, :, gi * gd:(gi + 1) * gd] = z[:, :gd]
        o_ref[0, 1, :, gi * gd:(gi + 1) * gd] = z[:, gd:]


def _fourier_pos_kernel(wp_ref, z_ref, wout_ref, x_ref, gate_ref, mg_ref, sh2_ref, sc2_ref, wr_ref,
                        o_ref, hx_ref, lg_ref, *, scale):
    f = jnp.dot(wp_ref[...], z_ref[0], preferred_element_type=F32) * scale
    y = jnp.dot(f.astype(BF16), wout_ref[...], preferred_element_type=F32)
    xn = x_ref[0] + gate_ref[0] * y
    o_ref[0] = xn
    _moe_prologue(xn, mg_ref, sh2_ref, sc2_ref, wr_ref, hx_ref, lg_ref)


def fourier_layer(x, norm, w_out, gate, moe, tm=512):
    b, n, d = x.shape
    moe_in, moe_out = _moe_prologue_specs(d, tm, N_EXPERTS)
    gd = d // FN_GROUPS
    g, sh, sc = norm
    cc, sc_tab = _dft_tables(gd)
    wc = jnp.concatenate([cc, sc_tab], axis=1).astype(BF16)
    cn, sn = _dft_tables(n, -1.0)
    wp = jnp.concatenate([cn, sn], axis=1).astype(BF16)
    per_b = pl.BlockSpec((1, 1, d), lambda bi, i: (bi, 0, 0))
    z = pl.pallas_call(
        _fourier_chan_kernel,
        grid=(b, n // tm),
        in_specs=[pl.BlockSpec((1, tm, d), lambda bi, i: (bi, i, 0)),
                  pl.BlockSpec((1, d), lambda bi, i: (0, 0)), per_b, per_b,
                  pl.BlockSpec((gd, 2 * gd), lambda bi, i: (0, 0))],
        out_specs=pl.BlockSpec((1, 2, tm, d), lambda bi, i: (bi, 0, i, 0)),
        out_shape=jax.ShapeDtypeStruct((b, 2, n, d), BF16),
        compiler_params=_cparams(("parallel", "parallel")),
        name="fourier_chan",
    )(x, g, sh, sc, wc)
    z = z.reshape(b, 2 * n, d)
    return pl.pallas_call(
        functools.partial(_fourier_pos_kernel, scale=1.0 / math.sqrt(n * gd)),
        grid=(b, n // tm),
        in_specs=[pl.BlockSpec((tm, 2 * n), lambda bi, i: (i, 0)),
                  pl.BlockSpec((1, 2 * n, d), lambda bi, i: (bi, 0, 0)),
                  pl.BlockSpec((d, d), lambda bi, i: (0, 0)),
                  pl.BlockSpec((1, tm, d), lambda bi, i: (bi, i, 0)), per_b] + moe_in,
        out_specs=[pl.BlockSpec((1, tm, d), lambda bi, i: (bi, i, 0))] + moe_out,
        out_shape=[jax.ShapeDtypeStruct((b, n, d), F32), jax.ShapeDtypeStruct((b, n, d), BF16),
                   jax.ShapeDtypeStruct((b, N_EXPERTS, n), F32)],
        compiler_params=_cparams(("parallel", "arbitrary"), VMEM_LIMIT),
        name="fourier_pos",
    )(wp, z, w_out, x, gate, *moe)


def _excl_cumsum_lanes(mask, tri):
    e, n = mask.shape
    mf = jnp.where(mask, 1.0, 0.0)
    offset = jnp.zeros((e, 1), F32)
    parts = []
    for blk in range(n // LANES):
        part = mf[:, blk * LANES:(blk + 1) * LANES]
        parts.append(jnp.dot(part.astype(BF16), tri, preferred_element_type=F32) + offset)
        offset = offset + jnp.sum(part, axis=1, keepdims=True)
    return jnp.concatenate(parts, axis=1)


def _moe_route_kernel(lg_ref, tri_ref, aff_ref, slot_ref, *, cap):
    lg = lg_ref[...]
    e = jnp.exp(lg - jnp.max(lg, axis=1, keepdims=True))
    aff3 = e / jnp.sum(e, axis=1, keepdims=True)
    aff_ref[...] = aff3
    b, n_exp, n = lg.shape
    aff = aff3.reshape(b * n_exp, n)

    def as_float(bits):
        return lax.bitcast_convert_type(bits, F32)

    def step(it, thr):
        cand = thr | jnp.left_shift(jnp.int32(1), 30 - it)
        cnt = jnp.sum((aff >= as_float(cand)).astype(jnp.int32), axis=1, keepdims=True)
        return jnp.where(cnt >= cap, cand, thr)

    thr = lax.fori_loop(0, 31, step, jnp.zeros((b * n_exp, 1), jnp.int32))
    gt = aff >= as_float(thr + 1)
    eq = (aff >= as_float(thr)) & jnp.logical_not(gt)
    need = (cap - jnp.sum(gt.astype(jnp.int32), axis=1, keepdims=True)).astype(F32)
    tri = tri_ref[...]
    sel = gt | (eq & (_excl_cumsum_lanes(eq, tri) < need))
    pos = _excl_cumsum_lanes(sel, tri).astype(jnp.int32)
    slot_ref[...] = jnp.where(sel, pos, -1).reshape(b, n_exp, n)


def _moe_gather_kernel(slot_ref, aff_ref, hx_ref, xs_ref, ta_ref, *, cap, ge):
    e0 = pl.multiple_of(pl.program_id(1) * ge, ge)
    n = slot_ref.shape[2]
    rows = lax.broadcasted_iota(jnp.int32, (cap, n), 0)
    hits, tas = [], []
    for k in range(ge):
        hit = rows == slot_ref[0, pl.ds(e0 + k, 1), :]
        hits.append(jnp.where(hit, 1.0, 0.0).astype(BF16))
        tas.append(jnp.sum(jnp.where(hit, aff_ref[0, pl.ds(e0 + k, 1), :], 0.0), axis=1, keepdims=True))
    onehot = jnp.concatenate(hits, axis=0) if ge > 1 else hits[0]
    xs = jnp.dot(onehot, hx_ref[0], preferred_element_type=F32).astype(BF16)
    xs_ref[0] = xs.reshape(ge, cap, xs.shape[-1])
    ta = jnp.concatenate(tas, axis=0) if ge > 1 else tas[0]
    ta_ref[0] = jnp.broadcast_to(ta, (ge * cap, LANES)).reshape(ge, cap, LANES)


def _moe_ffn_kernel(xs_ref, ta_ref, wg_ref, wu_ref, wd_ref, ys_ref, wgb, wub, wdb):
    @pl.when(pl.program_id(1) == 0)
    def _():
        wgb[...] = wg_ref[0, 0].astype(BF16)
        wub[...] = wu_ref[0, 0].astype(BF16)
        wdb[...] = wd_ref[0, 0].astype(BF16)

    bb, _, cap, d = xs_ref.shape
    xs = xs_ref[...].reshape(bb * cap, d)
    ta = ta_ref[...].reshape(bb * cap, LANES)[:, :1]
    gate = jnp.dot(xs, wgb[...], preferred_element_type=F32)
    up = jnp.dot(xs, wub[...], preferred_element_type=F32)
    hid = (gate * jax.nn.sigmoid(gate) * up).astype(BF16)
    y = jnp.dot(hid, wdb[...], preferred_element_type=F32) * ta
    ys_ref[...] = y.astype(BF16).reshape(bb, 1, cap, d)


def _moe_scatter_kernel(slot_ref, ys_ref, x_ref, gate_ref, *rest, cap, final):
    if final:
        fg_ref, o_ref = rest
    else:
        (o_ref,) = rest
    slot = slot_ref[0]
    n_exp = slot.shape[1]
    if cap % LANES == 0:
        want = lax.broadcasted_iota(jnp.int32, (1, cap), 1)
        hit = jnp.concatenate([jnp.where(slot[:, e:e + 1] == want, 1.0, 0.0).astype(BF16) for e in range(n_exp)],
                              axis=1)
    else:
        col = lax.broadcasted_iota(jnp.int32, (n_exp, n_exp * cap), 1)
        row = lax.broadcasted_iota(jnp.int32, (n_exp, n_exp * cap), 0)
        spread = jnp.where(col // cap == row, 1.0, 0.0).astype(BF16)
        want = (lax.broadcasted_iota(jnp.int32, (1, n_exp * cap), 1) % cap).astype(F32)
        ids = jnp.dot(slot.astype(F32).astype(BF16), spread, preferred_element_type=F32)
        hit = jnp.where(ids == want, 1.0, 0.0).astype(BF16)
    ys = ys_ref[0].reshape(n_exp * cap, ys_ref.shape[-1])
    out = x_ref[0] + gate_ref[0] * jnp.dot(hit, ys, preferred_element_type=F32)
    if final:
        out = _rms(out) * fg_ref[...]
    o_ref[0] = out


def moe_layer(x, hx, logits, layer, w_gate, w_up, w_down, gate, final_g=None):
    b, n, d = x.shape
    n_exp = logits.shape[1]
    cap = EC_CAPACITY_FACTOR * n // n_exp
    per_b2 = pl.BlockSpec((1, 1, d), lambda bi, i: (bi, 0, 0))
    tri = (np.arange(LANES)[:, None] < np.arange(LANES)[None, :]).astype(np.float32)
    aff, slot = pl.pallas_call(
        functools.partial(_moe_route_kernel, cap=cap),
        grid=(1,),
        in_specs=[pl.BlockSpec((b, n_exp, n), lambda i: (0, 0, 0)),
                  pl.BlockSpec((LANES, LANES), lambda i: (0, 0))],
        out_specs=[pl.BlockSpec((b, n_exp, n), lambda i: (0, 0, 0))] * 2,
        out_shape=[jax.ShapeDtypeStruct((b, n_exp, n), F32), jax.ShapeDtypeStruct((b, n_exp, n), jnp.int32)],
        compiler_params=_cparams(("arbitrary",)),
        name="moe_route",
    )(logits, jnp.asarray(tri, BF16))

    rows_per_step = 1024
    ge = max(1, min(n_exp, 2 * rows_per_step // cap))
    xs, ta = pl.pallas_call(
        functools.partial(_moe_gather_kernel, cap=cap, ge=ge),
        grid=(b, n_exp // ge),
        in_specs=[pl.BlockSpec((1, n_exp, n), lambda bi, e: (bi, 0, 0)),
                  pl.BlockSpec((1, n_exp, n), lambda bi, e: (bi, 0, 0)),
                  pl.BlockSpec((1, n, d), lambda bi, e: (bi, 0, 0))],
        out_specs=[pl.BlockSpec((1, ge, cap, d), lambda bi, e: (bi, e, 0, 0)),
                   pl.BlockSpec((1, ge, cap, LANES), lambda bi, e: (bi, e, 0, 0))],
        out_shape=[jax.ShapeDtypeStruct((b, n_exp, cap, d), BF16),
                   jax.ShapeDtypeStruct((b, n_exp, cap, LANES), F32)],
        compiler_params=_cparams(("parallel", "arbitrary"), VMEM_LIMIT),
        name="moe_gather",
    )(slot, aff, hx)

    bb = max(1, min(b, rows_per_step // cap))
    f = w_gate.shape[-1]
    ys = pl.pallas_call(
        _moe_ffn_kernel,
        grid=(n_exp, b // bb),
        in_specs=[pl.BlockSpec((bb, 1, cap, d), lambda e, bi: (bi, e, 0, 0)),
                  pl.BlockSpec((bb, 1, cap, LANES), lambda e, bi: (bi, e, 0, 0)),
                  pl.BlockSpec((1, 1, d, f), lambda e, bi: (layer, e, 0, 0)),
                  pl.BlockSpec((1, 1, d, f), lambda e, bi: (layer, e, 0, 0)),
                  pl.BlockSpec((1, 1, f, d), lambda e, bi: (layer, e, 0, 0))],
        out_specs=pl.BlockSpec((bb, 1, cap, d), lambda e, bi: (bi, e, 0, 0)),
        out_shape=jax.ShapeDtypeStruct((b, n_exp, cap, d), BF16),
        scratch_shapes=[pltpu.VMEM((d, f), BF16), pltpu.VMEM((d, f), BF16), pltpu.VMEM((f, d), BF16)],
        compiler_params=_cparams(("parallel", "arbitrary"), VMEM_LIMIT),
        name="moe_ffn",
    )(xs, ta, w_gate, w_up, w_down)

    slot_t = jnp.swapaxes(slot, 1, 2)
    tn = min(1024, n)
    args = [slot_t, ys, x, gate]
    specs = [pl.BlockSpec((1, tn, n_exp), lambda bi, i: (bi, i, 0)),
             pl.BlockSpec((1, n_exp, cap, d), lambda bi, i: (bi, 0, 0, 0)),
             pl.BlockSpec((1, tn, d), lambda bi, i: (bi, i, 0)), per_b2]
    if final_g is not None:
        args.append(final_g)
        specs.append(pl.BlockSpec((1, d), lambda bi, i: (0, 0)))
    return pl.pallas_call(
        functools.partial(_moe_scatter_kernel, cap=cap, final=final_g is not None),
        grid=(b, n // tn),
        in_specs=specs,
        out_specs=pl.BlockSpec((1, tn, d), lambda bi, i: (bi, i, 0)),
        out_shape=jax.ShapeDtypeStruct((b, n, d), F32),
        compiler_params=_cparams(("parallel", "arbitrary"), VMEM_LIMIT),
        name="moe_scatter",
    )(*args)


def kernel(x, c, ctx, c_ctx, ada_w, ada_b, mixer_norm_g, moe_norm_g, router_w, moe_w_gate, moe_w_up, moe_w_down, da_w_in, da_lambda_q1, da_lambda_k1, da_lambda_q2, da_lambda_k2, da_subln_g, da_w_out, gm_w_in, gm_v_g, gm_w_s, gm_b_s, gm_w_out, na_w_in, na_rpb, na_w_out, fn_w_out, final_norm_g):
    b, n, d = x.shape
    rows = n // GRID_W
    readers = [i for i in range(DEPTH) if i % N_MIXERS in CTX_READERS]
    last_reader = max(readers) if readers else -1

    r_pad = -(-(b + 1) // 8) * 8
    cond = jnp.zeros((r_pad, d), F32).at[:b].set(c).at[b].set(c_ctx)
    mods = ada_params(cond, ada_w, ada_b)

    def lat_mod(i, k):
        return mods[i, :b, k * d:(k + 1) * d].reshape(b, 1, d)

    def ctx_mod(i, k):
        return jnp.broadcast_to(mods[i, b, k * d:(k + 1) * d].reshape(1, 1, d), (b, 1, d))

    rope = _rope_tables(n)
    perm = _da_perm()

    for i in range(DEPTH):
        kind, j = i % N_MIXERS, i // N_MIXERS
        need_ctx = i <= last_reader
        update_ctx = i < last_reader
        mg = mixer_norm_g[i].reshape(1, d)
        xnorm = (mg, lat_mod(i, 0), lat_mod(i, 1))
        cnorm = (mg, ctx_mod(i, 0), ctx_mod(i, 1)) if need_ctx else None
        g1 = lat_mod(i, 2)
        wr = _router_split(router_w[i])
        moe_g = moe_norm_g[i].reshape(1, d)
        xmoe = (moe_g, lat_mod(i, 3), lat_mod(i, 4), wr)
        cmoe = (moe_g, ctx_mod(i, 3), ctx_mod(i, 4), wr) if update_ctx else None

        if kind == MIX_DIFF:
            lam_init = 0.8 - 0.6 * math.exp(-0.3 * i)
            cols = np.arange(2 * DA_WIDTH).reshape(2 * DA_HEADS, LANES)[:, perm].reshape(-1)
            w_in = jnp.concatenate([da_w_in[j][:, cols], da_w_in[j][:, 2 * DA_WIDTH:]], axis=1).astype(BF16)
            w_out = da_w_out[j].astype(BF16)
            lams = [t[j].reshape(1, DA_HEAD_DIM) for t in (da_lambda_q1, da_lambda_k1, da_lambda_q2, da_lambda_k2)]
            sg = da_subln_g[j].reshape(DA_V_DIM, 1)
            nb = DA_WIDTH // LANES
            qkv = linear(x, w_in, norm=xnorm, rope=rope, rope_tiles=2)
            if update_ctx:
                qkv_c = linear(ctx, w_in, norm=cnorm)
                ctx_src = (qkv_c, nb, 2 * nb)
            else:
                ctx_src = (linear(ctx, w_in[:, DA_WIDTH:], norm=cnorm), 0, nb)
            o = diff_attention(qkv, [ctx_src, (qkv, nb, 2 * nb)], lams, sg, lam_init)
            x, hx, logits = linear(o, w_out, res=(x, g1), moe=xmoe, out_dtype=F32)
            if update_ctx:
                oc = diff_attention(qkv_c, [ctx_src], lams, sg, lam_init, heads_per_step=DA_HEADS)
                ctx, hc, logits_c = linear(oc, w_out, res=(ctx, ctx_mod(i, 2)), moe=cmoe, out_dtype=F32)
        elif kind == MIX_GMLP:
            w_in = gm_w_in[j].astype(BF16)
            w_out = gm_w_out[j].astype(BF16)
            vg = gm_v_g[j].reshape(1, GM_WIDTH)
            w_s = gm_w_s[j].astype(BF16)
            b_s_t = jnp.transpose(gm_b_s[j])
            x, hx, logits = gmlp_layer(x, xnorm, w_in, vg, w_s, b_s_t, w_out, g1, xmoe)
            if update_ctx:
                ctx, hc, logits_c = gmlp_layer(ctx, cnorm, w_in, vg, w_s, b_s_t, w_out, ctx_mod(i, 2), cmoe)
        elif kind == MIX_NATTEN:
            w_in = na_w_in[j].astype(BF16)
            w_out = na_w_out[j].astype(BF16)
            qkv = linear(x, w_in, norm=xnorm)
            kv_c = linear(ctx, w_in[:, NA_WIDTH:], norm=cnorm)
            o = natten(qkv, kv_c, _natten_bias(na_rpb[j], rows))
            x, hx, logits = linear(o, w_out, res=(x, g1), moe=xmoe, out_dtype=F32)
            if update_ctx:
                raise NotImplementedError("context update after a neighbourhood layer")
        else:
            x, hx, logits = fourier_layer(x, xnorm, fn_w_out[j].astype(BF16), g1, xmoe)
            if update_ctx:
                ctx, hc, logits_c = fourier_layer(ctx, cnorm, fn_w_out[j].astype(BF16), ctx_mod(i, 2), cmoe)

        last = i == DEPTH - 1
        x = moe_layer(x, hx, logits, i, moe_w_gate, moe_w_up, moe_w_down, lat_mod(i, 5),
                      final_g=final_norm_g.reshape(1, d) if last else None)
        if update_ctx:
            ctx = moe_layer(ctx, hc, logits_c, i, moe_w_gate, moe_w_up, moe_w_down, ctx_mod(i, 5))
    return x
```

```python
import functools
import math

import numpy as np
import jax
import jax.numpy as jnp
from jax import lax
from jax.experimental import pallas as pl
from jax.experimental.pallas import tpu as pltpu

F32 = jnp.float32
BF16 = jnp.bfloat16

D_MODEL = 1024
DEPTH = 4
GRID_W = 64
N_MIXERS = 4
MIX_DIFF, MIX_GMLP, MIX_NATTEN, MIX_FOURIER = 0, 1, 2, 3
CTX_READERS = (MIX_DIFF, MIX_NATTEN)
EPS = 1e-6
NEG_INF = -1e30
ROPE_BASE = 10000.0

DA_HEADS = 8
DA_HEAD_DIM = 64
DA_V_DIM = 2 * DA_HEAD_DIM
DA_WIDTH = DA_HEADS * DA_V_DIM

GM_CHUNK = 128
GM_GROUPS = 8
GM_WIDTH = 2 * D_MODEL
GM_GROUP_DIM = GM_WIDTH // GM_GROUPS

NA_HEADS = 16
NA_HEAD_DIM = D_MODEL // NA_HEADS
NA_WIDTH = NA_HEADS * NA_HEAD_DIM
NA_WIN_R = 8
NA_WIN_C = 16
NA_ROW_BLOCK = 4
NA_SLAB_ROWS = NA_ROW_BLOCK + NA_WIN_R

FN_GROUPS = 4
DFT_SPLIT = 256

N_EXPERTS = 16
EC_CAPACITY_FACTOR = 2

LANES = 128
VMEM_LIMIT = 56 * 1024 * 1024


def _cparams(sem, vmem=None):
    return pltpu.CompilerParams(dimension_semantics=sem, vmem_limit_bytes=vmem)


def _rms(x):
    return x * lax.rsqrt(jnp.mean(x * x, axis=-1, keepdims=True) + EPS)


def _ada_kernel(c_ref, w_ref, b_ref, o_ref):
    c = c_ref[...]
    h = (c * jax.nn.sigmoid(c)).astype(BF16)
    o_ref[0] = jnp.dot(h, w_ref[0].astype(BF16), preferred_element_type=F32) + b_ref[0]


def ada_params(cond, ada_w, ada_b):
    r = cond.shape[0]
    depth, d, n_out = ada_w.shape
    tn = 1024
    return pl.pallas_call(
        _ada_kernel,
        grid=(depth, n_out // tn),
        in_specs=[pl.BlockSpec((r, d), lambda l, j: (0, 0)),
                  pl.BlockSpec((1, d, tn), lambda l, j: (l, 0, j)),
                  pl.BlockSpec((1, 1, tn), lambda l, j: (l, 0, j))],
        out_specs=pl.BlockSpec((1, r, tn), lambda l, j: (l, 0, j)),
        out_shape=jax.ShapeDtypeStruct((depth, r, n_out), F32),
        compiler_params=_cparams(("parallel", "parallel")),
        name="ada_params",
    )(cond, ada_w, ada_b.reshape(depth, 1, n_out))


def _moe_prologue(xn, mg_ref, sh_ref, sc_ref, wr_ref, hx_ref, lg_ref):
    h = _rms(xn) * mg_ref[...]
    h = h * (1.0 + sc_ref[0]) + sh_ref[0]
    h_hi = h.astype(BF16)
    h_lo = (h - h_hi.astype(F32)).astype(BF16)
    hx_ref[0] = h_hi
    a = jnp.dot(h_hi, wr_ref[...], preferred_element_type=F32)
    b = jnp.dot(h_lo, wr_ref[:, :LANES], preferred_element_type=F32)
    lt = a[:, :LANES] + a[:, LANES:] + b
    lg_ref[0] = jnp.transpose(lt)[:lg_ref.shape[1]]


def _router_split(w_router):
    w = jnp.pad(w_router, ((0, 0), (0, LANES - w_router.shape[1])))
    hi = w.astype(BF16)
    return jnp.concatenate([hi, (w - hi.astype(F32)).astype(BF16)], axis=1)


def _moe_prologue_specs(d, tm, n_exp):
    in_specs = [pl.BlockSpec((1, d), lambda bi, i: (0, 0)),
                pl.BlockSpec((1, 1, d), lambda bi, i: (bi, 0, 0)),
                pl.BlockSpec((1, 1, d), lambda bi, i: (bi, 0, 0)),
                pl.BlockSpec((d, 2 * LANES), lambda bi, i: (0, 0))]
    out_specs = [pl.BlockSpec((1, tm, d), lambda bi, i: (bi, i, 0)),
                 pl.BlockSpec((1, n_exp, tm), lambda bi, i: (bi, 0, i))]
    return in_specs, out_specs


def _linear_kernel(*refs, has_norm, has_res, has_moe, rope_tiles, tn):
    it = iter(refs)
    x_ref = next(it)
    if has_norm:
        g_ref, sh_ref, sc_ref = next(it), next(it), next(it)
    w_ref = next(it)
    if rope_tiles:
        cos_ref, sin_ref = next(it), next(it)
    if has_res:
        res_ref, gate_ref = next(it), next(it)
    if has_moe:
        moe_in = [next(it) for _ in range(4)]
    o_ref = next(it)

    if has_norm:
        h = _rms(x_ref[0]) * g_ref[...]
        hb = (h * (1.0 + sc_ref[0]) + sh_ref[0]).astype(BF16)
    else:
        hb = x_ref[0].astype(BF16)
    for j in range(w_ref.shape[1] // tn):
        cols = slice(j * tn, (j + 1) * tn)
        y = jnp.dot(hb, w_ref[:, cols], preferred_element_type=F32)
        if j < rope_tiles:
            cos, sin = cos_ref[...], sin_ref[...]
            for s in range(tn // LANES):
                seg = y[:, s * LANES:(s + 1) * LANES]
                rot = pltpu.roll(seg, LANES // 2, axis=1)
                lanes = slice(j * tn + s * LANES, j * tn + (s + 1) * LANES)
                o_ref[0, :, lanes] = (seg * cos + rot * sin).astype(o_ref.dtype)
        elif has_res:
            o_ref[0, :, cols] = (res_ref[0, :, cols] + gate_ref[0, :, cols] * y).astype(o_ref.dtype)
        else:
            o_ref[0, :, cols] = y.astype(o_ref.dtype)
    if has_moe:
        _moe_prologue(o_ref[0], *moe_in, next(it), next(it))


def linear(x, w, *, norm=None, res=None, rope=None, rope_tiles=0, moe=None, out_dtype=BF16, tm=1024, tn=1024):
    b, n, k = x.shape
    m = w.shape[1]
    tm = min(tm, n)
    tn = min(tn, m)
    args, specs = [x], [pl.BlockSpec((1, tm, k), lambda bi, i: (bi, i, 0))]
    if norm is not None:
        g, sh, sc = norm
        args += [g, sh, sc]
        specs += [pl.BlockSpec((1, k), lambda bi, i: (0, 0)),
                  pl.BlockSpec((1, 1, k), lambda bi, i: (bi, 0, 0)),
                  pl.BlockSpec((1, 1, k), lambda bi, i: (bi, 0, 0))]
    args.append(w)
    specs.append(pl.BlockSpec((k, m), lambda bi, i: (0, 0)))
    if rope is not None:
        args += list(rope)
        specs += [pl.BlockSpec((tm, LANES), lambda bi, i: (i, 0))] * 2
    if res is not None:
        r, gate = res
        args += [r, gate]
        specs += [pl.BlockSpec((1, tm, m), lambda bi, i: (bi, i, 0)),
                  pl.BlockSpec((1, 1, m), lambda bi, i: (bi, 0, 0))]
    out_specs = [pl.BlockSpec((1, tm, m), lambda bi, i: (bi, i, 0))]
    out_shape = [jax.ShapeDtypeStruct((b, n, m), out_dtype)]
    if moe is not None:
        moe_in, moe_out = _moe_prologue_specs(m, tm, N_EXPERTS)
        args += list(moe)
        specs += moe_in
        out_specs += moe_out
        out_shape += [jax.ShapeDtypeStruct((b, n, m), BF16), jax.ShapeDtypeStruct((b, N_EXPERTS, n), F32)]
    kern = functools.partial(_linear_kernel, has_norm=norm is not None, has_res=res is not None,
                             has_moe=moe is not None, rope_tiles=rope_tiles if rope is not None else 0, tn=tn)
    out = pl.pallas_call(
        kern,
        grid=(b, n // tm),
        in_specs=specs,
        out_specs=out_specs,
        out_shape=out_shape,
        compiler_params=_cparams(("parallel", "parallel"), VMEM_LIMIT),
        name="linear",
    )(*args)
    return out if moe is not None else out[0]


DA_VT_ROWS = DA_V_DIM + 16
ATT_TQ = 256
ATT_KC = 768


def _interleave(stage_a, stage_b):
    for i in range(max(len(stage_a), len(stage_b))):
        if i < len(stage_a):
            stage_a[i]()
        if i < len(stage_b):
            stage_b[i]()


def _diff_attn_kernel(*refs, n_src, lam_init):
    q_ref = refs[0]
    k_refs = refs[1:1 + n_src]
    v_refs = refs[1 + n_src:1 + 2 * n_src]
    lq1_ref, lk1_ref, lq2_ref, lk2_ref, g_ref, o_ref, k_all, vt_all, s_scr, e_scr = refs[1 + 2 * n_src:]
    n_heads = q_ref.shape[2] // LANES
    lanes = [slice(hh * LANES, (hh + 1) * LANES) for hh in range(n_heads)]

    for hh in range(n_heads):
        row = 0
        for k_ref, v_ref in zip(k_refs, v_refs):
            n = k_ref.shape[1]
            k_all[hh, row:row + n, :] = k_ref[0, :, lanes[hh]]
            vt_all[hh, :DA_V_DIM, row:row + n] = jnp.transpose(v_ref[0, :, lanes[hh]].astype(F32)).astype(BF16)
            row += n
        vt_all[hh, DA_V_DIM:, :] = jnp.ones((DA_VT_ROWS - DA_V_DIM, vt_all.shape[2]), BF16)

    lam = (jnp.exp(jnp.sum(lq1_ref[...] * lk1_ref[...], keepdims=True))
           - jnp.exp(jnp.sum(lq2_ref[...] * lk2_ref[...], keepdims=True)) + lam_init)
    lane = lax.broadcasted_iota(jnp.int32, (1, LANES), 1)
    first = (lane % DA_HEAD_DIM) < (DA_HEAD_DIM // 2)
    dims = (((1,), (1,)), ((), ()))
    c = DA_HEAD_DIM ** -0.5 * math.log2(math.e)
    nk = k_all.shape[1]
    tq = min(ATT_TQ, q_ref.shape[1])
    items = [(hh, t) for hh in range(n_heads) for t in range(q_ref.shape[1] // tq)]
    state = [dict(m=None) for _ in items]
    chunks = [slice(r, min(r + ATT_KC, nk)) for r in range(0, nk, ATT_KC)]

    def scores(w):
        hh, t = items[w]
        q = q_ref[0, t * tq:(t + 1) * tq, lanes[hh]]
        zero = jnp.zeros_like(q)
        q2 = jnp.concatenate([jnp.where(first, q, zero), jnp.where(first, zero, q)], axis=0)

        def chunk(rows):
            s = lax.dot_general(k_all[hh, rows, :], q2, dims, preferred_element_type=F32) * c
            s_scr[w % 2, rows, :] = s
            m = jnp.max(s, axis=0, keepdims=True)
            state[w]["m"] = m if state[w]["m"] is None else jnp.maximum(state[w]["m"], m)

        return [functools.partial(chunk, rows) for rows in chunks]

    def exps(w):
        def chunk(rows):
            e_scr[w % 2, rows, :] = jnp.exp2(s_scr[w % 2, rows, :] - state[w]["m"]).astype(BF16)

        return [functools.partial(chunk, rows) for rows in chunks]

    def values(w):
        hh, t = items[w]
        ot = jnp.dot(vt_all[hh], e_scr[w % 2], preferred_element_type=F32)
        ot = ot[:DA_V_DIM] / ot[DA_V_DIM:DA_V_DIM + 1]
        ot = ot[:, :tq] - lam * ot[:, tq:]
        ot = ot * lax.rsqrt(jnp.mean(ot * ot, axis=0, keepdims=True) + EPS) * g_ref[...] * (1.0 - lam_init)
        o_ref[0, t * tq:(t + 1) * tq, lanes[hh]] = jnp.transpose(ot).astype(o_ref.dtype)

    for w in range(len(items) + 2):
        if 0 <= w - 2 < len(items):
            values(w - 2)
        _interleave(scores(w) if w < len(items) else [],
                    (exps(w - 1) if 0 <= w - 1 < len(items) else []))


def diff_attention(q_arr, srcs, lams, subln_g, lam_init, heads_per_step=1):
    b, nq, _ = q_arr.shape
    nk = sum(a.shape[1] for a, _, _ in srcs)
    tq = min(ATT_TQ, nq)
    hps = heads_per_step
    w = hps * LANES
    assert all(kc % hps == 0 and vc % hps == 0 for _, kc, vc in srcs)
    vec = pl.BlockSpec((1, DA_HEAD_DIM), lambda bi, h: (0, 0))
    k_specs = [pl.BlockSpec((1, a.shape[1], w), lambda bi, h, c0=kc // hps: (bi, 0, c0 + h)) for a, kc, _ in srcs]
    v_specs = [pl.BlockSpec((1, a.shape[1], w), lambda bi, h, c0=vc // hps: (bi, 0, c0 + h)) for a, _, vc in srcs]
    arrs = [a for a, _, _ in srcs]
    return pl.pallas_call(
        functools.partial(_diff_attn_kernel, n_src=len(srcs), lam_init=lam_init),
        grid=(b, DA_HEADS // hps),
        in_specs=[pl.BlockSpec((1, nq, w), lambda bi, h: (bi, 0, h))] + k_specs + v_specs
                 + [vec, vec, vec, vec, pl.BlockSpec((DA_V_DIM, 1), lambda bi, h: (0, 0))],
        out_specs=pl.BlockSpec((1, nq, w), lambda bi, h: (bi, 0, h)),
        out_shape=jax.ShapeDtypeStruct((b, nq, DA_WIDTH), BF16),
        scratch_shapes=[pltpu.VMEM((hps, nk, LANES), BF16), pltpu.VMEM((hps, DA_VT_ROWS, nk), BF16),
                        pltpu.VMEM((2, nk, 2 * tq), F32), pltpu.VMEM((2, nk, 2 * tq), BF16)],
        compiler_params=_cparams(("parallel", "parallel"), VMEM_LIMIT),
        name="diff_attention",
    )(q_arr, *arrs, *arrs, *lams, subln_g)


def _da_perm():
    perm = np.zeros(LANES, np.int32)
    for l in range(LANES):
        part, within = divmod(l, 64)
        j, rem = divmod(within, 32)
        seg, i = divmod(rem, 16)
        perm[l] = j * 64 + seg * 32 + part * 16 + i
    return perm


def _rope_tables(n_tok):
    t = jnp.arange(n_tok, dtype=jnp.int32)
    n_freq = DA_HEAD_DIM // 4
    inv = ROPE_BASE ** (-jnp.arange(n_freq, dtype=F32) / n_freq)
    ang_r = (t // GRID_W).astype(F32)[:, None] * inv
    ang_c = (t % GRID_W).astype(F32)[:, None] * inv
    cos32 = jnp.concatenate([jnp.cos(ang_r), jnp.cos(ang_c)], axis=1)
    sin32 = jnp.concatenate([jnp.sin(ang_r), jnp.sin(ang_c)], axis=1)
    cos = jnp.tile(cos32, (1, 4))
    sin = jnp.concatenate([-sin32, -sin32, sin32, sin32], axis=1)
    return cos, sin


def _gmlp_kernel(x_ref, g_ref, sh_ref, sc_ref, win_ref, vg_ref, ws_ref, bs_ref, wout_ref, gate_ref,
                 mg_ref, sh2_ref, sc2_ref, wr_ref, o_ref, hx_ref, lg_ref, *, tm):
    x = x_ref[0]
    h = _rms(x) * g_ref[...]
    h = (h * (1.0 + sc_ref[0]) + sh_ref[0]).astype(BF16)
    uv = jax.nn.gelu(jnp.dot(h, win_ref[...], preferred_element_type=F32), approximate=True)
    u = uv[:, :GM_WIDTH]
    v = (_rms(uv[:, GM_WIDTH:]) * vg_ref[...]).astype(BF16)
    rows = []
    for c in range(tm // GM_CHUNK):
        cols = []
        for gi in range(GM_GROUPS):
            vv = v[c * GM_CHUNK:(c + 1) * GM_CHUNK, gi * GM_GROUP_DIM:(gi + 1) * GM_GROUP_DIM]
            sv = jnp.dot(ws_ref[gi], vv, preferred_element_type=F32) + bs_ref[:, gi:gi + 1]
            cols.append(sv)
        rows.append(jnp.concatenate(cols, axis=1))
    sv = jnp.concatenate(rows, axis=0) if len(rows) > 1 else rows[0]
    y = jnp.dot((u * sv).astype(BF16), wout_ref[...], preferred_element_type=F32)
    xn = x + gate_ref[0] * y
    o_ref[0] = xn
    _moe_prologue(xn, mg_ref, sh2_ref, sc2_ref, wr_ref, hx_ref, lg_ref)


def gmlp_layer(x, norm, w_in, v_g, w_s, b_s_t, w_out, gate, moe, tm=512):
    b, n, d = x.shape
    tm = min(tm, n)
    g, sh, sc = norm
    moe_in, moe_out = _moe_prologue_specs(d, tm, N_EXPERTS)
    full = lambda shape: pl.BlockSpec(shape, lambda bi, i: (0,) * len(shape))
    per_b = pl.BlockSpec((1, 1, d), lambda bi, i: (bi, 0, 0))
    return pl.pallas_call(
        functools.partial(_gmlp_kernel, tm=tm),
        grid=(b, n // tm),
        in_specs=[pl.BlockSpec((1, tm, d), lambda bi, i: (bi, i, 0)),
                  full((1, d)), per_b, per_b,
                  full(w_in.shape), full((1, GM_WIDTH)), full(w_s.shape), full(b_s_t.shape),
                  full(w_out.shape), per_b] + moe_in,
        out_specs=[pl.BlockSpec((1, tm, d), lambda bi, i: (bi, i, 0))] + moe_out,
        out_shape=[jax.ShapeDtypeStruct((b, n, d), F32), jax.ShapeDtypeStruct((b, n, d), BF16),
                   jax.ShapeDtypeStruct((b, N_EXPERTS, n), F32)],
        compiler_params=_cparams(("parallel", "arbitrary"), VMEM_LIMIT),
        name="gmlp_layer",
    )(x, g, sh, sc, w_in, v_g, w_s, b_s_t, w_out, gate, *moe)


NA_VT_ROWS = LANES + 16


def _natten_kernel(q_ref, k_ref, v_ref, kc_ref, vc_ref, bias_ref, o_ref, vt_all, vct, s_scr, e_scr, *, n_blocks):
    tq = NA_ROW_BLOCK * GRID_W
    slab_blocks = NA_SLAB_ROWS // NA_ROW_BLOCK
    slab = slab_blocks * tq
    lc = kc_ref.shape[1]
    n_pairs = q_ref.shape[2] // LANES
    lanes = [slice(p * LANES, (p + 1) * LANES) for p in range(n_pairs)]
    for p in range(n_pairs):
        for blk in range(n_blocks):
            vt_all[p, :LANES, blk * tq:(blk + 1) * tq] = jnp.transpose(
                v_ref[0, blk * tq:(blk + 1) * tq, lanes[p]].astype(F32)).astype(BF16)
        vt_all[p, LANES:, :] = jnp.ones((NA_VT_ROWS - LANES, vt_all.shape[2]), BF16)
        vct[p, :LANES, :] = jnp.transpose(vc_ref[0, :, lanes[p]].astype(F32)).astype(BF16)
        vct[p, LANES:, :] = jnp.ones((NA_VT_ROWS - LANES, lc), BF16)

    lane = lax.broadcasted_iota(jnp.int32, (1, LANES), 1)
    left = lane < NA_HEAD_DIM
    dims = (((1,), (1,)), ((), ()))
    c = NA_HEAD_DIM ** -0.5 * math.log2(math.e)
    items = [(p, t) for p in range(n_pairs) for t in range(n_blocks)]
    state = [dict(m=None) for _ in items]
    key0 = [min(max(t - 1, 0), n_blocks - slab_blocks) * tq for t in range(n_blocks)]
    kind = [0 if t == 0 else 2 if t == n_blocks - 1 else 1 for t in range(n_blocks)]

    def scores(w):
        p, t = items[w]
        q = q_ref[0, t * tq:(t + 1) * tq, lanes[p]]
        zero = jnp.zeros_like(q)
        q2 = jnp.concatenate([jnp.where(left, q, zero), jnp.where(left, zero, q)], axis=0)

        def local():
            k = k_ref[0, key0[t]:key0[t] + slab, lanes[p]]
            s = lax.dot_general(k, q2, dims, preferred_element_type=F32) * c + bias_ref[p, kind[t]]
            s_scr[w % 2, :slab, :] = s
            state[w]["m"] = jnp.max(s, axis=0, keepdims=True)

        def context():
            s = lax.dot_general(kc_ref[0, :, lanes[p]], q2, dims, preferred_element_type=F32) * c
            s_scr[w % 2, slab:, :] = s
            state[w]["m"] = jnp.maximum(state[w]["m"], jnp.max(s, axis=0, keepdims=True))

        return [local, context]

    def exps(w):
        def chunk(rows):
            e_scr[w % 2, rows, :] = jnp.exp2(s_scr[w % 2, rows, :] - state[w]["m"]).astype(BF16)

        return [functools.partial(chunk, slice(0, slab)), functools.partial(chunk, slice(slab, slab + lc))]

    def values(w):
        p, t = items[w]
        vt = jnp.concatenate([vt_all[p, :, key0[t]:key0[t] + slab], vct[p]], axis=1)
        ot = jnp.dot(vt, e_scr[w % 2], preferred_element_type=F32)
        ot = ot[:LANES] / ot[LANES:LANES + 1]
        pair = jnp.concatenate([ot[:NA_HEAD_DIM, :tq], ot[NA_HEAD_DIM:, tq:]], axis=0)
        o_ref[0, t * tq:(t + 1) * tq, lanes[p]] = jnp.transpose(pair).astype(o_ref.dtype)

    for w in range(len(items) + 2):
        if 0 <= w - 2 < len(items):
            values(w - 2)
        _interleave(scores(w) if w < len(items) else [],
                    (exps(w - 1) if 0 <= w - 1 < len(items) else []))


def _natten_bias(rpb, rows):
    n_heads = rpb.shape[0]
    n_blocks = rows // NA_ROW_BLOCK
    n_dr, n_dc = 2 * NA_WIN_R - 1, 2 * NA_WIN_C - 1
    rpb = rpb * math.log2(math.e)
    span = 2 * GRID_W - 1
    lo = GRID_W - NA_WIN_C - 1
    u = jnp.pad(rpb, ((0, 0), (0, 0), (lo, span - n_dc - lo)))
    skew = jnp.tile(u, (1, 1, GRID_W))[:, :, :GRID_W * (span - 1)].reshape(n_heads, n_dr, GRID_W, span - 1)
    toep = skew[:, :, :, GRID_W - 2:2 * GRID_W - 2]
    pad = NA_ROW_BLOCK
    toep = jnp.pad(toep, ((0, 0), (pad, pad), (0, 0), (0, 0)))
    i = np.arange(NA_ROW_BLOCK)[:, None, None, None]
    c = np.arange(GRID_W)[None, :, None, None]
    m = np.arange(NA_SLAB_ROWS)[None, None, :, None]
    kc = np.arange(GRID_W)[None, None, None, :]
    win_c0 = np.clip(c - NA_WIN_C // 2, 0, GRID_W - NA_WIN_C)
    col_ok = (kc >= win_c0) & (kc < win_c0 + NA_WIN_C)
    tiles, ok = [], []
    for rb in (0, 1, n_blocks - 1):
        slab0 = int(np.clip(rb - 1, 0, n_blocks - 3)) * NA_ROW_BLOCK
        r = rb * NA_ROW_BLOCK + i
        r0 = np.clip(r - NA_WIN_R // 2, 0, rows - NA_WIN_R)
        kr = slab0 + m
        ok.append(np.broadcast_to((kr >= r0) & (kr < r0 + NA_WIN_R) & col_ok,
                                  (NA_ROW_BLOCK, GRID_W, NA_SLAB_ROWS, GRID_W)))
        for ii in range(NA_ROW_BLOCK):
            off = slab0 - (rb * NA_ROW_BLOCK + ii) + NA_WIN_R - 1 + pad
            tiles.append(toep[:, off:off + NA_SLAB_ROWS])
    tq, slab = NA_ROW_BLOCK * GRID_W, NA_SLAB_ROWS * GRID_W
    bias = jnp.stack(tiles, axis=1).reshape(n_heads // 2, 2, 3, NA_ROW_BLOCK, NA_SLAB_ROWS, GRID_W, GRID_W)
    bias = jnp.transpose(bias, (0, 2, 4, 6, 1, 3, 5)).reshape(n_heads // 2, 3, slab, 2 * tq)
    ok = np.stack(ok).transpose(0, 3, 4, 1, 2).reshape(3, slab, tq)
    ok = np.concatenate([ok, ok], axis=2)
    return jnp.where(ok[None], bias, NEG_INF)


def natten(qkv, kv_ctx, bias, pairs_per_step=2):
    b, n, _ = qkv.shape
    lc = kv_ctx.shape[1]
    tq = NA_ROW_BLOCK * GRID_W
    n_blocks = n // tq
    pps = pairs_per_step
    steps = NA_HEADS // 2 // pps
    w = pps * LANES
    return pl.pallas_call(
        functools.partial(_natten_kernel, n_blocks=n_blocks),
        grid=(steps, b),
        in_specs=[pl.BlockSpec((1, n, w), lambda p, bi: (bi, 0, p)),
                  pl.BlockSpec((1, n, w), lambda p, bi: (bi, 0, steps + p)),
                  pl.BlockSpec((1, n, w), lambda p, bi: (bi, 0, 2 * steps + p)),
                  pl.BlockSpec((1, lc, w), lambda p, bi: (bi, 0, p)),
                  pl.BlockSpec((1, lc, w), lambda p, bi: (bi, 0, steps + p)),
                  pl.BlockSpec((pps, 3, NA_SLAB_ROWS * GRID_W, 2 * tq), lambda p, bi: (p, 0, 0, 0))],
        out_specs=pl.BlockSpec((1, n, w), lambda p, bi: (bi, 0, p)),
        out_shape=jax.ShapeDtypeStruct((b, n, NA_WIDTH), BF16),
        scratch_shapes=[pltpu.VMEM((pps, NA_VT_ROWS, n), BF16), pltpu.VMEM((pps, NA_VT_ROWS, lc), BF16),
                        pltpu.VMEM((2, NA_SLAB_ROWS * GRID_W + lc, 2 * tq), F32),
                        pltpu.VMEM((2, NA_SLAB_ROWS * GRID_W + lc, 2 * tq), BF16)],
        compiler_params=_cparams(("parallel", "parallel"), VMEM_LIMIT),
        name="natten",
    )(qkv, qkv, qkv, kv_ctx, kv_ctx, bias)


def _dft_tables(n, sign=1.0):
    k = jnp.arange(n, dtype=jnp.int32)
    w = 2.0 * math.pi / n

    def cs(m):
        ang = ((k[:, None] * m[None, :]) % n).astype(F32) * w
        return jnp.cos(ang), jnp.sin(ang)

    if n <= DFT_SPLIT:
        c, s = cs(k)
        return c, sign * s
    ca, sa = cs(jnp.arange(n // DFT_SPLIT, dtype=jnp.int32) * DFT_SPLIT)
    cb, sb = cs(jnp.arange(DFT_SPLIT, dtype=jnp.int32))
    c = ca[:, :, None] * cb[:, None, :] - sa[:, :, None] * sb[:, None, :]
    s = sa[:, :, None] * cb[:, None, :] + ca[:, :, None] * sb[:, None, :]
    return c.reshape(n, n), (sign * s).reshape(n, n)


def _fourier_chan_kernel(x_ref, g_ref, sh_ref, sc_ref, wc_ref, o_ref):
    h = _rms(x_ref[0]) * g_ref[...]
    h = (h * (1.0 + sc_ref[0]) + sh_ref[0]).astype(BF16)
    gd = D_MODEL // FN_GROUPS
    for gi in range(FN_GROUPS):
        z = jnp.dot(h[:, gi * gd:(gi + 1) * gd], wc_ref[...], preferred_element_type=F32).astype(BF16)
        o_ref[0, 0, :, gi * gd:(gi + 1) * gd] = z[:, :gd]
        o_ref[0, 1, :, gi * gd:(gi + 1) * gd] = z[:, gd:]


def _fourier_pos_kernel(wp_ref, z_ref, wout_ref, x_ref, gate_ref, mg_ref, sh2_ref, sc2_ref, wr_ref,
                        o_ref, hx_ref, lg_ref, *, scale):
    f = jnp.dot(wp_ref[...], z_ref[0], preferred_element_type=F32) * scale
    y = jnp.dot(f.astype(BF16), wout_ref[...], preferred_element_type=F32)
    xn = x_ref[0] + gate_ref[0] * y
    o_ref[0] = xn
    _moe_prologue(xn, mg_ref, sh2_ref, sc2_ref, wr_ref, hx_ref, lg_ref)


def fourier_layer(x, norm, w_out, gate, moe, tm=512):
    b, n, d = x.shape
    moe_in, moe_out = _moe_prologue_specs(d, tm, N_EXPERTS)
    gd = d // FN_GROUPS
    g, sh, sc = norm
    cc, sc_tab = _dft_tables(gd)
    wc = jnp.concatenate([cc, sc_tab], axis=1).astype(BF16)
    cn, sn = _dft_tables(n, -1.0)
    wp = jnp.concatenate([cn, sn], axis=1).astype(BF16)
    per_b = pl.BlockSpec((1, 1, d), lambda bi, i: (bi, 0, 0))
    z = pl.pallas_call(
        _fourier_chan_kernel,
        grid=(b, n // tm),
        in_specs=[pl.BlockSpec((1, tm, d), lambda bi, i: (bi, i, 0)),
                  pl.BlockSpec((1, d), lambda bi, i: (0, 0)), per_b, per_b,
                  pl.BlockSpec((gd, 2 * gd), lambda bi, i: (0, 0))],
        out_specs=pl.BlockSpec((1, 2, tm, d), lambda bi, i: (bi, 0, i, 0)),
        out_shape=jax.ShapeDtypeStruct((b, 2, n, d), BF16),
        compiler_params=_cparams(("parallel", "parallel")),
        name="fourier_chan",
    )(x, g, sh, sc, wc)
    z = z.reshape(b, 2 * n, d)
    return pl.pallas_call(
        functools.partial(_fourier_pos_kernel, scale=1.0 / math.sqrt(n * gd)),
        grid=(b, n // tm),
        in_specs=[pl.BlockSpec((tm, 2 * n), lambda bi, i: (i, 0)),
                  pl.BlockSpec((1, 2 * n, d), lambda bi, i: (bi, 0, 0)),
                  pl.BlockSpec((d, d), lambda bi, i: (0, 0)),
                  pl.BlockSpec((1, tm, d), lambda bi, i: (bi, i, 0)), per_b] + moe_in,
        out_specs=[pl.BlockSpec((1, tm, d), lambda bi, i: (bi, i, 0))] + moe_out,
        out_shape=[jax.ShapeDtypeStruct((b, n, d), F32), jax.ShapeDtypeStruct((b, n, d), BF16),
                   jax.ShapeDtypeStruct((b, N_EXPERTS, n), F32)],
        compiler_params=_cparams(("parallel", "arbitrary"), VMEM_LIMIT),
        name="fourier_pos",
    )(wp, z, w_out, x, gate, *moe)


def _excl_cumsum_lanes(mask, tri):
    e, n = mask.shape
    mf = jnp.where(mask, 1.0, 0.0)
    offset = jnp.zeros((e, 1), F32)
    parts = []
    for blk in range(n // LANES):
        part = mf[:, blk * LANES:(blk + 1) * LANES]
        parts.append(jnp.dot(part.astype(BF16), tri, preferred_element_type=F32) + offset)
        offset = offset + jnp.sum(part, axis=1, keepdims=True)
    return jnp.concatenate(parts, axis=1)


def _moe_route_kernel(lg_ref, tri_ref, aff_ref, slot_ref, *, cap):
    lg = lg_ref[...]
    e = jnp.exp(lg - jnp.max(lg, axis=1, keepdims=True))
    aff3 = e / jnp.sum(e, axis=1, keepdims=True)
    aff_ref[...] = aff3
    b, n_exp, n = lg.shape
    aff = aff3.reshape(b * n_exp, n)

    def as_float(bits):
        return lax.bitcast_convert_type(bits, F32)

    def step(it, thr):
        cand = thr | jnp.left_shift(jnp.int32(1), 30 - it)
        cnt = jnp.sum((aff >= as_float(cand)).astype(jnp.int32), axis=1, keepdims=True)
        return jnp.where(cnt >= cap, cand, thr)

    thr = lax.fori_loop(0, 31, step, jnp.zeros((b * n_exp, 1), jnp.int32))
    gt = aff >= as_float(thr + 1)
    eq = (aff >= as_float(thr)) & jnp.logical_not(gt)
    need = (cap - jnp.sum(gt.astype(jnp.int32), axis=1, keepdims=True)).astype(F32)
    tri = tri_ref[...]
    sel = gt | (eq & (_excl_cumsum_lanes(eq, tri) < need))
    pos = _excl_cumsum_lanes(sel, tri).astype(jnp.int32)
    slot_ref[...] = jnp.where(sel, pos, -1).reshape(b, n_exp, n)


def _moe_gather_kernel(slot_ref, aff_ref, hx_ref, xs_ref, ta_ref, *, cap, ge):
    e0 = pl.multiple_of(pl.program_id(1) * ge, ge)
    n = slot_ref.shape[2]
    rows = lax.broadcasted_iota(jnp.int32, (cap, n), 0)
    hits, tas = [], []
    for k in range(ge):
        hit = rows == slot_ref[0, pl.ds(e0 + k, 1), :]
        hits.append(jnp.where(hit, 1.0, 0.0).astype(BF16))
        tas.append(jnp.sum(jnp.where(hit, aff_ref[0, pl.ds(e0 + k, 1), :], 0.0), axis=1, keepdims=True))
    onehot = jnp.concatenate(hits, axis=0) if ge > 1 else hits[0]
    xs = jnp.dot(onehot, hx_ref[0], preferred_element_type=F32).astype(BF16)
    xs_ref[0] = xs.reshape(ge, cap, xs.shape[-1])
    ta = jnp.concatenate(tas, axis=0) if ge > 1 else tas[0]
    ta_ref[0] = jnp.broadcast_to(ta, (ge * cap, LANES)).reshape(ge, cap, LANES)


def _moe_ffn_kernel(xs_ref, ta_ref, wg_ref, wu_ref, wd_ref, ys_ref, wgb, wub, wdb):
    @pl.when(pl.program_id(1) == 0)
    def _():
        wgb[...] = wg_ref[0, 0].astype(BF16)
        wub[...] = wu_ref[0, 0].astype(BF16)
        wdb[...] = wd_ref[0, 0].astype(BF16)

    bb, _, cap, d = xs_ref.shape
    xs = xs_ref[...].reshape(bb * cap, d)
    ta = ta_ref[...].reshape(bb * cap, LANES)[:, :1]
    gate = jnp.dot(xs, wgb[...], preferred_element_type=F32)
    up = jnp.dot(xs, wub[...], preferred_element_type=F32)
    hid = (gate * jax.nn.sigmoid(gate) * up).astype(BF16)
    y = jnp.dot(hid, wdb[...], preferred_element_type=F32) * ta
    ys_ref[...] = y.astype(BF16).reshape(bb, 1, cap, d)


def _moe_scatter_kernel(slot_ref, ys_ref, x_ref, gate_ref, *rest, cap, final):
    if final:
        fg_ref, o_ref = rest
    else:
        (o_ref,) = rest
    slot = slot_ref[0]
    n_exp = slot.shape[1]
    if cap % LANES == 0:
        want = lax.broadcasted_iota(jnp.int32, (1, cap), 1)
        hit = jnp.concatenate([jnp.where(slot[:, e:e + 1] == want, 1.0, 0.0).astype(BF16) for e in range(n_exp)],
                              axis=1)
    else:
        col = lax.broadcasted_iota(jnp.int32, (n_exp, n_exp * cap), 1)
        row = lax.broadcasted_iota(jnp.int32, (n_exp, n_exp * cap), 0)
        spread = jnp.where(col // cap == row, 1.0, 0.0).astype(BF16)
        want = (lax.broadcasted_iota(jnp.int32, (1, n_exp * cap), 1) % cap).astype(F32)
        ids = jnp.dot(slot.astype(F32).astype(BF16), spread, preferred_element_type=F32)
        hit = jnp.where(ids == want, 1.0, 0.0).astype(BF16)
    ys = ys_ref[0].reshape(n_exp * cap, ys_ref.shape[-1])
    out = x_ref[0] + gate_ref[0] * jnp.dot(hit, ys, preferred_element_type=F32)
    if final:
        out = _rms(out) * fg_ref[...]
    o_ref[0] = out


def moe_layer(x, hx, logits, layer, w_gate, w_up, w_down, gate, final_g=None):
    b, n, d = x.shape
    n_exp = logits.shape[1]
    cap = EC_CAPACITY_FACTOR * n // n_exp
    per_b2 = pl.BlockSpec((1, 1, d), lambda bi, i: (bi, 0, 0))
    tri = (np.arange(LANES)[:, None] < np.arange(LANES)[None, :]).astype(np.float32)
    aff, slot = pl.pallas_call(
        functools.partial(_moe_route_kernel, cap=cap),
        grid=(1,),
        in_specs=[pl.BlockSpec((b, n_exp, n), lambda i: (0, 0, 0)),
                  pl.BlockSpec((LANES, LANES), lambda i: (0, 0))],
        out_specs=[pl.BlockSpec((b, n_exp, n), lambda i: (0, 0, 0))] * 2,
        out_shape=[jax.ShapeDtypeStruct((b, n_exp, n), F32), jax.ShapeDtypeStruct((b, n_exp, n), jnp.int32)],
        compiler_params=_cparams(("arbitrary",)),
        name="moe_route",
    )(logits, jnp.asarray(tri, BF16))

    rows_per_step = 1024
    ge = max(1, min(n_exp, 2 * rows_per_step // cap))
    xs, ta = pl.pallas_call(
        functools.partial(_moe_gather_kernel, cap=cap, ge=ge),
        grid=(b, n_exp // ge),
        in_specs=[pl.BlockSpec((1, n_exp, n), lambda bi, e: (bi, 0, 0)),
                  pl.BlockSpec((1, n_exp, n), lambda bi, e: (bi, 0, 0)),
                  pl.BlockSpec((1, n, d), lambda bi, e: (bi, 0, 0))],
        out_specs=[pl.BlockSpec((1, ge, cap, d), lambda bi, e: (bi, e, 0, 0)),
                   pl.BlockSpec((1, ge, cap, LANES), lambda bi, e: (bi, e, 0, 0))],
        out_shape=[jax.ShapeDtypeStruct((b, n_exp, cap, d), BF16),
                   jax.ShapeDtypeStruct((b, n_exp, cap, LANES), F32)],
        compiler_params=_cparams(("parallel", "arbitrary"), VMEM_LIMIT),
        name="moe_gather",
    )(slot, aff, hx)

    bb = max(1, min(b, rows_per_step // cap))
    f = w_gate.shape[-1]
    ys = pl.pallas_call(
        _moe_ffn_kernel,
        grid=(n_exp, b // bb),
        in_specs=[pl.BlockSpec((bb, 1, cap, d), lambda e, bi: (bi, e, 0, 0)),
                  pl.BlockSpec((bb, 1, cap, LANES), lambda e, bi: (bi, e, 0, 0)),
                  pl.BlockSpec((1, 1, d, f), lambda e, bi: (layer, e, 0, 0)),
                  pl.BlockSpec((1, 1, d, f), lambda e, bi: (layer, e, 0, 0)),
                  pl.BlockSpec((1, 1, f, d), lambda e, bi: (layer, e, 0, 0))],
        out_specs=pl.BlockSpec((bb, 1, cap, d), lambda e, bi: (bi, e, 0, 0)),
        out_shape=jax.ShapeDtypeStruct((b, n_exp, cap, d), BF16),
        scratch_shapes=[pltpu.VMEM((d, f), BF16), pltpu.VMEM((d, f), BF16), pltpu.VMEM((f, d), BF16)],
        compiler_params=_cparams(("parallel", "arbitrary"), VMEM_LIMIT),
        name="moe_ffn",
    )(xs, ta, w_gate, w_up, w_down)

    slot_t = jnp.swapaxes(slot, 1, 2)
    tn = min(1024, n)
    args = [slot_t, ys, x, gate]
    specs = [pl.BlockSpec((1, tn, n_exp), lambda bi, i: (bi, i, 0)),
             pl.BlockSpec((1, n_exp, cap, d), lambda bi, i: (bi, 0, 0, 0)),
             pl.BlockSpec((1, tn, d), lambda bi, i: (bi, i, 0)), per_b2]
    if final_g is not None:
        args.append(final_g)
        specs.append(pl.BlockSpec((1, d), lambda bi, i: (0, 0)))
    return pl.pallas_call(
        functools.partial(_moe_scatter_kernel, cap=cap, final=final_g is not None),
        grid=(b, n // tn),
        in_specs=specs,
        out_specs=pl.BlockSpec((1, tn, d), lambda bi, i: (bi, i, 0)),
        out_shape=jax.ShapeDtypeStruct((b, n, d), F32),
        compiler_params=_cparams(("parallel", "arbitrary"), VMEM_LIMIT),
        name="moe_scatter",
    )(*args)


def kernel(x, c, ctx, c_ctx, ada_w, ada_b, mixer_norm_g, moe_norm_g, router_w, moe_w_gate, moe_w_up, moe_w_down, da_w_in, da_lambda_q1, da_lambda_k1, da_lambda_q2, da_lambda_k2, da_subln_g, da_w_out, gm_w_in, gm_v_g, gm_w_s, gm_b_s, gm_w_out, na_w_in, na_rpb, na_w_out, fn_w_out, final_norm_g):
    b, n, d = x.shape
    rows = n // GRID_W
    readers = [i for i in range(DEPTH) if i % N_MIXERS in CTX_READERS]
    last_reader = max(readers) if readers else -1

    r_pad = -(-(b + 1) // 8) * 8
    cond = jnp.zeros((r_pad, d), F32).at[:b].set(c).at[b].set(c_ctx)
    mods = ada_params(cond, ada_w, ada_b)

    def lat_mod(i, k):
        return mods[i, :b, k * d:(k + 1) * d].reshape(b, 1, d)

    def ctx_mod(i, k):
        return jnp.broadcast_to(mods[i, b, k * d:(k + 1) * d].reshape(1, 1, d), (b, 1, d))

    rope = _rope_tables(n)
    perm = _da_perm()

    for i in range(DEPTH):
        kind, j = i % N_MIXERS, i // N_MIXERS
        need_ctx = i <= last_reader
        update_ctx = i < last_reader
        mg = mixer_norm_g[i].reshape(1, d)
        xnorm = (mg, lat_mod(i, 0), lat_mod(i, 1))
        cnorm = (mg, ctx_mod(i, 0), ctx_mod(i, 1)) if need_ctx else None
        g1 = lat_mod(i, 2)
        wr = _router_split(router_w[i])
        moe_g = moe_norm_g[i].reshape(1, d)
        xmoe = (moe_g, lat_mod(i, 3), lat_mod(i, 4), wr)
        cmoe = (moe_g, ctx_mod(i, 3), ctx_mod(i, 4), wr) if update_ctx else None

        if kind == MIX_DIFF:
            lam_init = 0.8 - 0.6 * math.exp(-0.3 * i)
            cols = np.arange(2 * DA_WIDTH).reshape(2 * DA_HEADS, LANES)[:, perm].reshape(-1)
            w_in = jnp.concatenate([da_w_in[j][:, cols], da_w_in[j][:, 2 * DA_WIDTH:]], axis=1).astype(BF16)
            w_out = da_w_out[j].astype(BF16)
            lams = [t[j].reshape(1, DA_HEAD_DIM) for t in (da_lambda_q1, da_lambda_k1, da_lambda_q2, da_lambda_k2)]
            sg = da_subln_g[j].reshape(DA_V_DIM, 1)
            nb = DA_WIDTH // LANES
            qkv = linear(x, w_in, norm=xnorm, rope=rope, rope_tiles=2)
            if update_ctx:
                qkv_c = linear(ctx, w_in, norm=cnorm)
                ctx_src = (qkv_c, nb, 2 * nb)
            else:
                ctx_src = (linear(ctx, w_in[:, DA_WIDTH:], norm=cnorm), 0, nb)
            o = diff_attention(qkv, [ctx_src, (qkv, nb, 2 * nb)], lams, sg, lam_init, heads_per_step=2)
            x, hx, logits = linear(o, w_out, res=(x, g1), moe=xmoe, out_dtype=F32)
            if update_ctx:
                oc = diff_attention(qkv_c, [ctx_src], lams, sg, lam_init, heads_per_step=DA_HEADS)
                ctx, hc, logits_c = linear(oc, w_out, res=(ctx, ctx_mod(i, 2)), moe=cmoe, out_dtype=F32)
        elif kind == MIX_GMLP:
            w_in = gm_w_in[j].astype(BF16)
            w_out = gm_w_out[j].astype(BF16)
            vg = gm_v_g[j].reshape(1, GM_WIDTH)
            w_s = gm_w_s[j].astype(BF16)
            b_s_t = jnp.transpose(gm_b_s[j])
            x, hx, logits = gmlp_layer(x, xnorm, w_in, vg, w_s, b_s_t, w_out, g1, xmoe)
            if update_ctx:
                ctx, hc, logits_c = gmlp_layer(ctx, cnorm, w_in, vg, w_s, b_s_t, w_out, ctx_mod(i, 2), cmoe)
        elif kind == MIX_NATTEN:
            w_in = na_w_in[j].astype(BF16)
            w_out = na_w_out[j].astype(BF16)
            qkv = linear(x, w_in, norm=xnorm)
            kv_c = linear(ctx, w_in[:, NA_WIDTH:], norm=cnorm)
            o = natten(qkv, kv_c, _natten_bias(na_rpb[j], rows))
            x, hx, logits = linear(o, w_out, res=(x, g1), moe=xmoe, out_dtype=F32)
            if update_ctx:
                raise NotImplementedError("context update after a neighbourhood layer")
        else:
            x, hx, logits = fourier_layer(x, xnorm, fn_w_out[j].astype(BF16), g1, xmoe)
            if update_ctx:
                ctx, hc, logits_c = fourier_layer(ctx, cnorm, fn_w_out[j].astype(BF16), ctx_mod(i, 2), cmoe)

        last = i == DEPTH - 1
        x = moe_layer(x, hx, logits, i, moe_w_gate, moe_w_up, moe_w_down, lat_mod(i, 5),
                      final_g=final_norm_g.reshape(1, d) if last else None)
        if update_ctx:
            ctx = moe_layer(ctx, hc, logits_c, i, moe_w_gate, moe_w_up, moe_w_down, ctx_mod(i, 5))
    return x
```

```python
import functools
import math

import numpy as np
import jax
import jax.numpy as jnp
from jax import lax
from jax.experimental import pallas as pl
from jax.experimental.pallas import tpu as pltpu

F32 = jnp.float32
BF16 = jnp.bfloat16

D_MODEL = 1024
DEPTH = 4
GRID_W = 64
N_MIXERS = 4
MIX_DIFF, MIX_GMLP, MIX_NATTEN, MIX_FOURIER = 0, 1, 2, 3
CTX_READERS = (MIX_DIFF, MIX_NATTEN)
EPS = 1e-6
NEG_INF = -1e30
ROPE_BASE = 10000.0

DA_HEADS = 8
DA_HEAD_DIM = 64
DA_V_DIM = 2 * DA_HEAD_DIM
DA_WIDTH = DA_HEADS * DA_V_DIM

GM_CHUNK = 128
GM_GROUPS = 8
GM_WIDTH = 2 * D_MODEL
GM_GROUP_DIM = GM_WIDTH // GM_GROUPS

NA_HEADS = 16
NA_HEAD_DIM = D_MODEL // NA_HEADS
NA_WIDTH = NA_HEADS * NA_HEAD_DIM
NA_WIN_R = 8
NA_WIN_C = 16
NA_ROW_BLOCK = 4
NA_SLAB_ROWS = NA_ROW_BLOCK + NA_WIN_R

FN_GROUPS = 4
DFT_SPLIT = 256

N_EXPERTS = 16
EC_CAPACITY_FACTOR = 2

LANES = 128
VMEM_LIMIT = 56 * 1024 * 1024


def _cparams(sem, vmem=None):
    return pltpu.CompilerParams(dimension_semantics=sem, vmem_limit_bytes=vmem)


def _rms(x):
    return x * lax.rsqrt(jnp.mean(x * x, axis=-1, keepdims=True) + EPS)


def _ada_kernel(c_ref, w_ref, b_ref, o_ref):
    c = c_ref[...]
    h = (c * jax.nn.sigmoid(c)).astype(BF16)
    o_ref[0] = jnp.dot(h, w_ref[0].astype(BF16), preferred_element_type=F32) + b_ref[0]


def ada_params(cond, ada_w, ada_b):
    r = cond.shape[0]
    depth, d, n_out = ada_w.shape
    tn = 1024
    return pl.pallas_call(
        _ada_kernel,
        grid=(depth, n_out // tn),
        in_specs=[pl.BlockSpec((r, d), lambda l, j: (0, 0)),
                  pl.BlockSpec((1, d, tn), lambda l, j: (l, 0, j)),
                  pl.BlockSpec((1, 1, tn), lambda l, j: (l, 0, j))],
        out_specs=pl.BlockSpec((1, r, tn), lambda l, j: (l, 0, j)),
        out_shape=jax.ShapeDtypeStruct((depth, r, n_out), F32),
        compiler_params=_cparams(("parallel", "parallel")),
        name="ada_params",
    )(cond, ada_w, ada_b.reshape(depth, 1, n_out))


def _moe_prologue(xn, mg_ref, sh_ref, sc_ref, wr_ref, hx_ref, lg_ref):
    h = _rms(xn) * mg_ref[...]
    h = h * (1.0 + sc_ref[0]) + sh_ref[0]
    h_hi = h.astype(BF16)
    h_lo = (h - h_hi.astype(F32)).astype(BF16)
    hx_ref[0] = h_hi
    a = jnp.dot(h_hi, wr_ref[...], preferred_element_type=F32)
    b = jnp.dot(h_lo, wr_ref[:, :LANES], preferred_element_type=F32)
    lt = a[:, :LANES] + a[:, LANES:] + b
    lg_ref[0] = jnp.transpose(lt)[:lg_ref.shape[1]]


def _router_split(w_router):
    w = jnp.pad(w_router, ((0, 0), (0, LANES - w_router.shape[1])))
    hi = w.astype(BF16)
    return jnp.concatenate([hi, (w - hi.astype(F32)).astype(BF16)], axis=1)


def _moe_prologue_specs(d, tm, n_exp):
    in_specs = [pl.BlockSpec((1, d), lambda bi, i: (0, 0)),
                pl.BlockSpec((1, 1, d), lambda bi, i: (bi, 0, 0)),
                pl.BlockSpec((1, 1, d), lambda bi, i: (bi, 0, 0)),
                pl.BlockSpec((d, 2 * LANES), lambda bi, i: (0, 0))]
    out_specs = [pl.BlockSpec((1, tm, d), lambda bi, i: (bi, i, 0)),
                 pl.BlockSpec((1, n_exp, tm), lambda bi, i: (bi, 0, i))]
    return in_specs, out_specs


def _linear_kernel(*refs, has_norm, has_res, has_moe, rope_tiles, tn):
    it = iter(refs)
    x_ref = next(it)
    if has_norm:
        g_ref, sh_ref, sc_ref = next(it), next(it), next(it)
    w_ref = next(it)
    if rope_tiles:
        cos_ref, sin_ref = next(it), next(it)
    if has_res:
        res_ref, gate_ref = next(it), next(it)
    if has_moe:
        moe_in = [next(it) for _ in range(4)]
    o_ref = next(it)

    if has_norm:
        h = _rms(x_ref[0]) * g_ref[...]
        hb = (h * (1.0 + sc_ref[0]) + sh_ref[0]).astype(BF16)
    else:
        hb = x_ref[0].astype(BF16)
    for j in range(w_ref.shape[1] // tn):
        cols = slice(j * tn, (j + 1) * tn)
        y = jnp.dot(hb, w_ref[:, cols], preferred_element_type=F32)
        if j < rope_tiles:
            cos, sin = cos_ref[...], sin_ref[...]
            for s in range(tn // LANES):
                seg = y[:, s * LANES:(s + 1) * LANES]
                rot = pltpu.roll(seg, LANES // 2, axis=1)
                lanes = slice(j * tn + s * LANES, j * tn + (s + 1) * LANES)
                o_ref[0, :, lanes] = (seg * cos + rot * sin).astype(o_ref.dtype)
        elif has_res:
            o_ref[0, :, cols] = (res_ref[0, :, cols] + gate_ref[0, :, cols] * y).astype(o_ref.dtype)
        else:
            o_ref[0, :, cols] = y.astype(o_ref.dtype)
    if has_moe:
        _moe_prologue(o_ref[0], *moe_in, next(it), next(it))


def linear(x, w, *, norm=None, res=None, rope=None, rope_tiles=0, moe=None, out_dtype=BF16, tm=1024, tn=1024):
    b, n, k = x.shape
    m = w.shape[1]
    tm = min(tm, n)
    tn = min(tn, m)
    args, specs = [x], [pl.BlockSpec((1, tm, k), lambda bi, i: (bi, i, 0))]
    if norm is not None:
        g, sh, sc = norm
        args += [g, sh, sc]
        specs += [pl.BlockSpec((1, k), lambda bi, i: (0, 0)),
                  pl.BlockSpec((1, 1, k), lambda bi, i: (bi, 0, 0)),
                  pl.BlockSpec((1, 1, k), lambda bi, i: (bi, 0, 0))]
    args.append(w)
    specs.append(pl.BlockSpec((k, m), lambda bi, i: (0, 0)))
    if rope is not None:
        args += list(rope)
        specs += [pl.BlockSpec((tm, LANES), lambda bi, i: (i, 0))] * 2
    if res is not None:
        r, gate = res
        args += [r, gate]
        specs += [pl.BlockSpec((1, tm, m), lambda bi, i: (bi, i, 0)),
                  pl.BlockSpec((1, 1, m), lambda bi, i: (bi, 0, 0))]
    out_specs = [pl.BlockSpec((1, tm, m), lambda bi, i: (bi, i, 0))]
    out_shape = [jax.ShapeDtypeStruct((b, n, m), out_dtype)]
    if moe is not None:
        moe_in, moe_out = _moe_prologue_specs(m, tm, N_EXPERTS)
        args += list(moe)
        specs += moe_in
        out_specs += moe_out
        out_shape += [jax.ShapeDtypeStruct((b, n, m), BF16), jax.ShapeDtypeStruct((b, N_EXPERTS, n), F32)]
    kern = functools.partial(_linear_kernel, has_norm=norm is not None, has_res=res is not None,
                             has_moe=moe is not None, rope_tiles=rope_tiles if rope is not None else 0, tn=tn)
    out = pl.pallas_call(
        kern,
        grid=(b, n // tm),
        in_specs=specs,
        out_specs=out_specs,
        out_shape=out_shape,
        compiler_params=_cparams(("parallel", "parallel"), VMEM_LIMIT),
        name="linear",
    )(*args)
    return out if moe is not None else out[0]


DA_VT_ROWS = DA_V_DIM + 16
ATT_TQ = 256
ATT_KC = 768


def _interleave(stage_a, stage_b):
    for i in range(max(len(stage_a), len(stage_b))):
        if i < len(stage_a):
            stage_a[i]()
        if i < len(stage_b):
            stage_b[i]()


def _diff_attn_kernel(*refs, n_src, lam_init):
    q_ref = refs[0]
    k_refs = refs[1:1 + n_src]
    v_refs = refs[1 + n_src:1 + 2 * n_src]
    lq1_ref, lk1_ref, lq2_ref, lk2_ref, g_ref, o_ref, k_all, vt_all, s_scr, e_scr = refs[1 + 2 * n_src:]
    n_heads = q_ref.shape[2] // LANES
    lanes = [slice(hh * LANES, (hh + 1) * LANES) for hh in range(n_heads)]

    for hh in range(n_heads):
        row = 0
        for k_ref, v_ref in zip(k_refs, v_refs):
            n = k_ref.shape[1]
            k_all[hh, row:row + n, :] = k_ref[0, :, lanes[hh]]
            vt_all[hh, :DA_V_DIM, row:row + n] = jnp.transpose(v_ref[0, :, lanes[hh]].astype(F32)).astype(BF16)
            row += n
        vt_all[hh, DA_V_DIM:, :] = jnp.ones((DA_VT_ROWS - DA_V_DIM, vt_all.shape[2]), BF16)

    lam = (jnp.exp(jnp.sum(lq1_ref[...] * lk1_ref[...], keepdims=True))
           - jnp.exp(jnp.sum(lq2_ref[...] * lk2_ref[...], keepdims=True)) + lam_init)
    lane = lax.broadcasted_iota(jnp.int32, (1, LANES), 1)
    first = (lane % DA_HEAD_DIM) < (DA_HEAD_DIM // 2)
    dims = (((1,), (1,)), ((), ()))
    c = DA_HEAD_DIM ** -0.5 * math.log2(math.e)
    nk = k_all.shape[1]
    tq = min(ATT_TQ, q_ref.shape[1])
    items = [(hh, t) for hh in range(n_heads) for t in range(q_ref.shape[1] // tq)]
    state = [dict(m=None) for _ in items]
    chunks = [slice(r, min(r + ATT_KC, nk)) for r in range(0, nk, ATT_KC)]

    def scores(w):
        hh, t = items[w]
        q = q_ref[0, t * tq:(t + 1) * tq, lanes[hh]]
        zero = jnp.zeros_like(q)
        q2 = jnp.concatenate([jnp.where(first, q, zero), jnp.where(first, zero, q)], axis=0)

        def chunk(rows):
            s = lax.dot_general(k_all[hh, rows, :], q2, dims, preferred_element_type=F32) * c
            s_scr[w % 2, rows, :] = s
            m = jnp.max(s, axis=0, keepdims=True)
            state[w]["m"] = m if state[w]["m"] is None else jnp.maximum(state[w]["m"], m)

        return [functools.partial(chunk, rows) for rows in chunks]

    def exps(w):
        def chunk(rows):
            e_scr[w % 2, rows, :] = jnp.exp2(s_scr[w % 2, rows, :] - state[w]["m"]).astype(BF16)

        return [functools.partial(chunk, rows) for rows in chunks]

    def values(w):
        hh, t = items[w]
        ot = jnp.dot(vt_all[hh], e_scr[w % 2], preferred_element_type=F32)
        ot = ot[:DA_V_DIM] / ot[DA_V_DIM:DA_V_DIM + 1]
        ot = ot[:, :tq] - lam * ot[:, tq:]
        ot = ot * lax.rsqrt(jnp.mean(ot * ot, axis=0, keepdims=True) + EPS) * g_ref[...] * (1.0 - lam_init)
        o_ref[0, t * tq:(t + 1) * tq, lanes[hh]] = jnp.transpose(ot).astype(o_ref.dtype)

    for w in range(len(items) + 2):
        if 0 <= w - 2 < len(items):
            values(w - 2)
        _interleave(scores(w) if w < len(items) else [],
                    (exps(w - 1) if 0 <= w - 1 < len(items) else []))


def diff_attention(q_arr, srcs, lams, subln_g, lam_init, heads_per_step=1):
    b, nq, _ = q_arr.shape
    nk = sum(a.shape[1] for a, _, _ in srcs)
    tq = min(ATT_TQ, nq)
    hps = heads_per_step
    w = hps * LANES
    assert all(kc % hps == 0 and vc % hps == 0 for _, kc, vc in srcs)
    vec = pl.BlockSpec((1, DA_HEAD_DIM), lambda bi, h: (0, 0))
    k_specs = [pl.BlockSpec((1, a.shape[1], w), lambda bi, h, c0=kc // hps: (bi, 0, c0 + h)) for a, kc, _ in srcs]
    v_specs = [pl.BlockSpec((1, a.shape[1], w), lambda bi, h, c0=vc // hps: (bi, 0, c0 + h)) for a, _, vc in srcs]
    arrs = [a for a, _, _ in srcs]
    return pl.pallas_call(
        functools.partial(_diff_attn_kernel, n_src=len(srcs), lam_init=lam_init),
        grid=(b, DA_HEADS // hps),
        in_specs=[pl.BlockSpec((1, nq, w), lambda bi, h: (bi, 0, h))] + k_specs + v_specs
                 + [vec, vec, vec, vec, pl.BlockSpec((DA_V_DIM, 1), lambda bi, h: (0, 0))],
        out_specs=pl.BlockSpec((1, nq, w), lambda bi, h: (bi, 0, h)),
        out_shape=jax.ShapeDtypeStruct((b, nq, DA_WIDTH), BF16),
        scratch_shapes=[pltpu.VMEM((hps, nk, LANES), BF16), pltpu.VMEM((hps, DA_VT_ROWS, nk), BF16),
                        pltpu.VMEM((2, nk, 2 * tq), F32), pltpu.VMEM((2, nk, 2 * tq), BF16)],
        compiler_params=_cparams(("parallel", "parallel"), VMEM_LIMIT),
        name="diff_attention",
    )(q_arr, *arrs, *arrs, *lams, subln_g)


def _da_perm():
    perm = np.zeros(LANES, np.int32)
    for l in range(LANES):
        part, within = divmod(l, 64)
        j, rem = divmod(within, 32)
        seg, i = divmod(rem, 16)
        perm[l] = j * 64 + seg * 32 + part * 16 + i
    return perm


def _rope_tables(n_tok):
    t = jnp.arange(n_tok, dtype=jnp.int32)
    n_freq = DA_HEAD_DIM // 4
    inv = ROPE_BASE ** (-jnp.arange(n_freq, dtype=F32) / n_freq)
    ang_r = (t // GRID_W).astype(F32)[:, None] * inv
    ang_c = (t % GRID_W).astype(F32)[:, None] * inv
    cos32 = jnp.concatenate([jnp.cos(ang_r), jnp.cos(ang_c)], axis=1)
    sin32 = jnp.concatenate([jnp.sin(ang_r), jnp.sin(ang_c)], axis=1)
    cos = jnp.tile(cos32, (1, 4))
    sin = jnp.concatenate([-sin32, -sin32, sin32, sin32], axis=1)
    return cos, sin


def _gmlp_kernel(x_ref, g_ref, sh_ref, sc_ref, win_ref, vg_ref, ws_ref, bs_ref, wout_ref, gate_ref,
                 mg_ref, sh2_ref, sc2_ref, wr_ref, o_ref, hx_ref, lg_ref, *, tm):
    x = x_ref[0]
    h = _rms(x) * g_ref[...]
    h = (h * (1.0 + sc_ref[0]) + sh_ref[0]).astype(BF16)
    uv = jax.nn.gelu(jnp.dot(h, win_ref[...], preferred_element_type=F32), approximate=True)
    u = uv[:, :GM_WIDTH]
    v = (_rms(uv[:, GM_WIDTH:]) * vg_ref[...]).astype(BF16)
    rows = []
    for c in range(tm // GM_CHUNK):
        cols = []
        for gi in range(GM_GROUPS):
            vv = v[c * GM_CHUNK:(c + 1) * GM_CHUNK, gi * GM_GROUP_DIM:(gi + 1) * GM_GROUP_DIM]
            sv = jnp.dot(ws_ref[gi], vv, preferred_element_type=F32) + bs_ref[:, gi:gi + 1]
            cols.append(sv)
        rows.append(jnp.concatenate(cols, axis=1))
    sv = jnp.concatenate(rows, axis=0) if len(rows) > 1 else rows[0]
    y = jnp.dot((u * sv).astype(BF16), wout_ref[...], preferred_element_type=F32)
    xn = x + gate_ref[0] * y
    o_ref[0] = xn
    _moe_prologue(xn, mg_ref, sh2_ref, sc2_ref, wr_ref, hx_ref, lg_ref)


def gmlp_layer(x, norm, w_in, v_g, w_s, b_s_t, w_out, gate, moe, tm=512):
    b, n, d = x.shape
    tm = min(tm, n)
    g, sh, sc = norm
    moe_in, moe_out = _moe_prologue_specs(d, tm, N_EXPERTS)
    full = lambda shape: pl.BlockSpec(shape, lambda bi, i: (0,) * len(shape))
    per_b = pl.BlockSpec((1, 1, d), lambda bi, i: (bi, 0, 0))
    return pl.pallas_call(
        functools.partial(_gmlp_kernel, tm=tm),
        grid=(b, n // tm),
        in_specs=[pl.BlockSpec((1, tm, d), lambda bi, i: (bi, i, 0)),
                  full((1, d)), per_b, per_b,
                  full(w_in.shape), full((1, GM_WIDTH)), full(w_s.shape), full(b_s_t.shape),
                  full(w_out.shape), per_b] + moe_in,
        out_specs=[pl.BlockSpec((1, tm, d), lambda bi, i: (bi, i, 0))] + moe_out,
        out_shape=[jax.ShapeDtypeStruct((b, n, d), F32), jax.ShapeDtypeStruct((b, n, d), BF16),
                   jax.ShapeDtypeStruct((b, N_EXPERTS, n), F32)],
        compiler_params=_cparams(("parallel", "arbitrary"), VMEM_LIMIT),
        name="gmlp_layer",
    )(x, g, sh, sc, w_in, v_g, w_s, b_s_t, w_out, gate, *moe)


NA_VT_ROWS = LANES + 16


def _natten_kernel(q_ref, k_ref, v_ref, kc_ref, vc_ref, bias_ref, o_ref, vt_all, vct, s_scr, e_scr, *, n_blocks):
    tq = NA_ROW_BLOCK * GRID_W
    slab_blocks = NA_SLAB_ROWS // NA_ROW_BLOCK
    slab = slab_blocks * tq
    lc = kc_ref.shape[1]
    n_pairs = q_ref.shape[2] // LANES
    lanes = [slice(p * LANES, (p + 1) * LANES) for p in range(n_pairs)]
    for p in range(n_pairs):
        for blk in range(n_blocks):
            vt_all[p, :LANES, blk * tq:(blk + 1) * tq] = jnp.transpose(
                v_ref[0, blk * tq:(blk + 1) * tq, lanes[p]].astype(F32)).astype(BF16)
        vt_all[p, LANES:, :] = jnp.ones((NA_VT_ROWS - LANES, vt_all.shape[2]), BF16)
        vct[p, :LANES, :] = jnp.transpose(vc_ref[0, :, lanes[p]].astype(F32)).astype(BF16)
        vct[p, LANES:, :] = jnp.ones((NA_VT_ROWS - LANES, lc), BF16)

    lane = lax.broadcasted_iota(jnp.int32, (1, LANES), 1)
    left = lane < NA_HEAD_DIM
    dims = (((1,), (1,)), ((), ()))
    c = NA_HEAD_DIM ** -0.5 * math.log2(math.e)
    items = [(p, t) for p in range(n_pairs) for t in range(n_blocks)]
    state = [dict(m=None) for _ in items]
    key0 = [min(max(t - 1, 0), n_blocks - slab_blocks) * tq for t in range(n_blocks)]
    kind = [0 if t == 0 else 2 if t == n_blocks - 1 else 1 for t in range(n_blocks)]

    def scores(w):
        p, t = items[w]
        q = q_ref[0, t * tq:(t + 1) * tq, lanes[p]]
        zero = jnp.zeros_like(q)
        q2 = jnp.concatenate([jnp.where(left, q, zero), jnp.where(left, zero, q)], axis=0)

        def local():
            k = k_ref[0, key0[t]:key0[t] + slab, lanes[p]]
            s = lax.dot_general(k, q2, dims, preferred_element_type=F32) * c + bias_ref[p, kind[t]]
            s_scr[w % 2, :slab, :] = s
            state[w]["m"] = jnp.max(s, axis=0, keepdims=True)

        def context():
            s = lax.dot_general(kc_ref[0, :, lanes[p]], q2, dims, preferred_element_type=F32) * c
            s_scr[w % 2, slab:, :] = s
            state[w]["m"] = jnp.maximum(state[w]["m"], jnp.max(s, axis=0, keepdims=True))

        return [local, context]

    def exps(w):
        def chunk(rows):
            e_scr[w % 2, rows, :] = jnp.exp2(s_scr[w % 2, rows, :] - state[w]["m"]).astype(BF16)

        return [functools.partial(chunk, slice(0, slab)), functools.partial(chunk, slice(slab, slab + lc))]

    def values(w):
        p, t = items[w]
        vt = jnp.concatenate([vt_all[p, :, key0[t]:key0[t] + slab], vct[p]], axis=1)
        ot = jnp.dot(vt, e_scr[w % 2], preferred_element_type=F32)
        ot = ot[:LANES] / ot[LANES:LANES + 1]
        pair = jnp.concatenate([ot[:NA_HEAD_DIM, :tq], ot[NA_HEAD_DIM:, tq:]], axis=0)
        o_ref[0, t * tq:(t + 1) * tq, lanes[p]] = jnp.transpose(pair).astype(o_ref.dtype)

    for w in range(len(items) + 2):
        if 0 <= w - 2 < len(items):
            values(w - 2)
        _interleave(scores(w) if w < len(items) else [],
                    (exps(w - 1) if 0 <= w - 1 < len(items) else []))


def _natten_bias(rpb, rows):
    n_heads = rpb.shape[0]
    n_blocks = rows // NA_ROW_BLOCK
    n_dr, n_dc = 2 * NA_WIN_R - 1, 2 * NA_WIN_C - 1
    rpb = rpb * math.log2(math.e)
    span = 2 * GRID_W - 1
    lo = GRID_W - NA_WIN_C - 1
    u = jnp.pad(rpb, ((0, 0), (0, 0), (lo, span - n_dc - lo)))
    skew = jnp.tile(u, (1, 1, GRID_W))[:, :, :GRID_W * (span - 1)].reshape(n_heads, n_dr, GRID_W, span - 1)
    toep = skew[:, :, :, GRID_W - 2:2 * GRID_W - 2]
    pad = NA_ROW_BLOCK
    toep = jnp.pad(toep, ((0, 0), (pad, pad), (0, 0), (0, 0)))
    i = np.arange(NA_ROW_BLOCK)[:, None, None, None]
    c = np.arange(GRID_W)[None, :, None, None]
    m = np.arange(NA_SLAB_ROWS)[None, None, :, None]
    kc = np.arange(GRID_W)[None, None, None, :]
    win_c0 = np.clip(c - NA_WIN_C // 2, 0, GRID_W - NA_WIN_C)
    col_ok = (kc >= win_c0) & (kc < win_c0 + NA_WIN_C)
    tiles, ok = [], []
    for rb in (0, 1, n_blocks - 1):
        slab0 = int(np.clip(rb - 1, 0, n_blocks - 3)) * NA_ROW_BLOCK
        r = rb * NA_ROW_BLOCK + i
        r0 = np.clip(r - NA_WIN_R // 2, 0, rows - NA_WIN_R)
        kr = slab0 + m
        ok.append(np.broadcast_to((kr >= r0) & (kr < r0 + NA_WIN_R) & col_ok,
                                  (NA_ROW_BLOCK, GRID_W, NA_SLAB_ROWS, GRID_W)))
        for ii in range(NA_ROW_BLOCK):
            off = slab0 - (rb * NA_ROW_BLOCK + ii) + NA_WIN_R - 1 + pad
            tiles.append(toep[:, off:off + NA_SLAB_ROWS])
    tq, slab = NA_ROW_BLOCK * GRID_W, NA_SLAB_ROWS * GRID_W
    bias = jnp.stack(tiles, axis=1).reshape(n_heads // 2, 2, 3, NA_ROW_BLOCK, NA_SLAB_ROWS, GRID_W, GRID_W)
    bias = jnp.transpose(bias, (0, 2, 4, 6, 1, 3, 5)).reshape(n_heads // 2, 3, slab, 2 * tq)
    ok = np.stack(ok).transpose(0, 3, 4, 1, 2).reshape(3, slab, tq)
    ok = np.concatenate([ok, ok], axis=2)
    return jnp.where(ok[None], bias, NEG_INF)


def natten(qkv, kv_ctx, bias, pairs_per_step=2):
    b, n, _ = qkv.shape
    lc = kv_ctx.shape[1]
    tq = NA_ROW_BLOCK * GRID_W
    n_blocks = n // tq
    pps = pairs_per_step
    steps = NA_HEADS // 2 // pps
    w = pps * LANES
    return pl.pallas_call(
        functools.partial(_natten_kernel, n_blocks=n_blocks),
        grid=(steps, b),
        in_specs=[pl.BlockSpec((1, n, w), lambda p, bi: (bi, 0, p)),
                  pl.BlockSpec((1, n, w), lambda p, bi: (bi, 0, steps + p)),
                  pl.BlockSpec((1, n, w), lambda p, bi: (bi, 0, 2 * steps + p)),
                  pl.BlockSpec((1, lc, w), lambda p, bi: (bi, 0, p)),
                  pl.BlockSpec((1, lc, w), lambda p, bi: (bi, 0, steps + p)),
                  pl.BlockSpec((pps, 3, NA_SLAB_ROWS * GRID_W, 2 * tq), lambda p, bi: (p, 0, 0, 0))],
        out_specs=pl.BlockSpec((1, n, w), lambda p, bi: (bi, 0, p)),
        out_shape=jax.ShapeDtypeStruct((b, n, NA_WIDTH), BF16),
        scratch_shapes=[pltpu.VMEM((pps, NA_VT_ROWS, n), BF16), pltpu.VMEM((pps, NA_VT_ROWS, lc), BF16),
                        pltpu.VMEM((2, NA_SLAB_ROWS * GRID_W + lc, 2 * tq), F32),
                        pltpu.VMEM((2, NA_SLAB_ROWS * GRID_W + lc, 2 * tq), BF16)],
        compiler_params=_cparams(("parallel", "parallel"), VMEM_LIMIT),
        name="natten",
    )(qkv, qkv, qkv, kv_ctx, kv_ctx, bias)


def _dft_tables(n, sign=1.0):
    k = jnp.arange(n, dtype=jnp.int32)
    w = 2.0 * math.pi / n

    def cs(m):
        ang = ((k[:, None] * m[None, :]) % n).astype(F32) * w
        return jnp.cos(ang), jnp.sin(ang)

    if n <= DFT_SPLIT:
        c, s = cs(k)
        return c, sign * s
    ca, sa = cs(jnp.arange(n // DFT_SPLIT, dtype=jnp.int32) * DFT_SPLIT)
    cb, sb = cs(jnp.arange(DFT_SPLIT, dtype=jnp.int32))
    c = ca[:, :, None] * cb[:, None, :] - sa[:, :, None] * sb[:, None, :]
    s = sa[:, :, None] * cb[:, None, :] + ca[:, :, None] * sb[:, None, :]
    return c.reshape(n, n), (sign * s).reshape(n, n)


def _fourier_chan_kernel(x_ref, g_ref, sh_ref, sc_ref, wc_ref, o_ref):
    h = _rms(x_ref[0]) * g_ref[...]
    h = (h * (1.0 + sc_ref[0]) + sh_ref[0]).astype(BF16)
    gd = D_MODEL // FN_GROUPS
    for gi in range(FN_GROUPS):
        z = jnp.dot(h[:, gi * gd:(gi + 1) * gd], wc_ref[...], preferred_element_type=F32).astype(BF16)
        o_ref[0, 0, :, gi * gd:(gi + 1) * gd] = z[:, :gd]
        o_ref[0, 1, :, gi * gd:(gi + 1) * gd] = z[:, gd:]


def _fourier_pos_kernel(wp_ref, z_ref, wout_ref, x_ref, gate_ref, mg_ref, sh2_ref, sc2_ref, wr_ref,
                        o_ref, hx_ref, lg_ref, *, scale):
    f = jnp.dot(wp_ref[...], z_ref[0], preferred_element_type=F32) * scale
    y = jnp.dot(f.astype(BF16), wout_ref[...], preferred_element_type=F32)
    xn = x_ref[0] + gate_ref[0] * y
    o_ref[0] = xn
    _moe_prologue(xn, mg_ref, sh2_ref, sc2_ref, wr_ref, hx_ref, lg_ref)


def fourier_layer(x, norm, w_out, gate, moe, tm=512):
    b, n, d = x.shape
    moe_in, moe_out = _moe_prologue_specs(d, tm, N_EXPERTS)
    gd = d // FN_GROUPS
    g, sh, sc = norm
    cc, sc_tab = _dft_tables(gd)
    wc = jnp.concatenate([cc, sc_tab], axis=1).astype(BF16)
    cn, sn = _dft_tables(n, -1.0)
    wp = jnp.concatenate([cn, sn], axis=1).astype(BF16)
    per_b = pl.BlockSpec((1, 1, d), lambda bi, i: (bi, 0, 0))
    tmc = min(2 * tm, n)
    z = pl.pallas_call(
        _fourier_chan_kernel,
        grid=(b, n // tmc),
        in_specs=[pl.BlockSpec((1, tmc, d), lambda bi, i: (bi, i, 0)),
                  pl.BlockSpec((1, d), lambda bi, i: (0, 0)), per_b, per_b,
                  pl.BlockSpec((gd, 2 * gd), lambda bi, i: (0, 0))],
        out_specs=pl.BlockSpec((1, 2, tmc, d), lambda bi, i: (bi, 0, i, 0)),
        out_shape=jax.ShapeDtypeStruct((b, 2, n, d), BF16),
        compiler_params=_cparams(("parallel", "parallel")),
        name="fourier_chan",
    )(x, g, sh, sc, wc)
    z = z.reshape(b, 2 * n, d)
    return pl.pallas_call(
        functools.partial(_fourier_pos_kernel, scale=1.0 / math.sqrt(n * gd)),
        grid=(b, n // tm),
        in_specs=[pl.BlockSpec((tm, 2 * n), lambda bi, i: (i, 0)),
                  pl.BlockSpec((1, 2 * n, d), lambda bi, i: (bi, 0, 0)),
                  pl.BlockSpec((d, d), lambda bi, i: (0, 0)),
                  pl.BlockSpec((1, tm, d), lambda bi, i: (bi, i, 0)), per_b] + moe_in,
        out_specs=[pl.BlockSpec((1, tm, d), lambda bi, i: (bi, i, 0))] + moe_out,
        out_shape=[jax.ShapeDtypeStruct((b, n, d), F32), jax.ShapeDtypeStruct((b, n, d), BF16),
                   jax.ShapeDtypeStruct((b, N_EXPERTS, n), F32)],
        compiler_params=_cparams(("parallel", "arbitrary"), VMEM_LIMIT),
        name="fourier_pos",
    )(wp, z, w_out, x, gate, *moe)


def _excl_cumsum_lanes(mask, tri):
    e, n = mask.shape
    mf = jnp.where(mask, 1.0, 0.0)
    offset = jnp.zeros((e, 1), F32)
    parts = []
    for blk in range(n // LANES):
        part = mf[:, blk * LANES:(blk + 1) * LANES]
        parts.append(jnp.dot(part.astype(BF16), tri, preferred_element_type=F32) + offset)
        offset = offset + jnp.sum(part, axis=1, keepdims=True)
    return jnp.concatenate(parts, axis=1)


def _moe_route_kernel(lg_ref, tri_ref, aff_ref, slot_ref, *, cap):
    lg = lg_ref[...]
    e = jnp.exp(lg - jnp.max(lg, axis=1, keepdims=True))
    aff3 = e / jnp.sum(e, axis=1, keepdims=True)
    aff_ref[...] = aff3
    b, n_exp, n = lg.shape
    aff = aff3.reshape(b * n_exp, n)

    def as_float(bits):
        return lax.bitcast_convert_type(bits, F32)

    def step(it, thr):
        cand = thr | jnp.left_shift(jnp.int32(1), 30 - it)
        cnt = jnp.sum((aff >= as_float(cand)).astype(jnp.int32), axis=1, keepdims=True)
        return jnp.where(cnt >= cap, cand, thr)

    thr = lax.fori_loop(0, 31, step, jnp.zeros((b * n_exp, 1), jnp.int32))
    gt = aff >= as_float(thr + 1)
    eq = (aff >= as_float(thr)) & jnp.logical_not(gt)
    need = (cap - jnp.sum(gt.astype(jnp.int32), axis=1, keepdims=True)).astype(F32)
    tri = tri_ref[...]
    sel = gt | (eq & (_excl_cumsum_lanes(eq, tri) < need))
    pos = _excl_cumsum_lanes(sel, tri).astype(jnp.int32)
    slot_ref[...] = jnp.where(sel, pos, -1).reshape(b, n_exp, n)


def _moe_gather_kernel(slot_ref, aff_ref, hx_ref, xs_ref, ta_ref, *, cap, ge):
    e0 = pl.multiple_of(pl.program_id(1) * ge, ge)
    n = slot_ref.shape[2]
    rows = lax.broadcasted_iota(jnp.int32, (cap, n), 0)
    hits, tas = [], []
    for k in range(ge):
        hit = rows == slot_ref[0, pl.ds(e0 + k, 1), :]
        hits.append(jnp.where(hit, 1.0, 0.0).astype(BF16))
        tas.append(jnp.sum(jnp.where(hit, aff_ref[0, pl.ds(e0 + k, 1), :], 0.0), axis=1, keepdims=True))
    onehot = jnp.concatenate(hits, axis=0) if ge > 1 else hits[0]
    xs = jnp.dot(onehot, hx_ref[0], preferred_element_type=F32).astype(BF16)
    xs_ref[0] = xs.reshape(ge, cap, xs.shape[-1])
    ta = jnp.concatenate(tas, axis=0) if ge > 1 else tas[0]
    ta_ref[0] = jnp.broadcast_to(ta, (ge * cap, LANES)).reshape(ge, cap, LANES)


def _moe_ffn_kernel(xs_ref, ta_ref, wg_ref, wu_ref, wd_ref, ys_ref, *cast_refs):
    if cast_refs:
        wgb, wub, wdb = cast_refs

        @pl.when(pl.program_id(1) == 0)
        def _():
            wgb[0] = wg_ref[0, 0].astype(BF16)
            wub[0] = wu_ref[0, 0].astype(BF16)
            wdb[0] = wd_ref[0, 0].astype(BF16)
    else:
        wgb, wub, wdb = wg_ref, wu_ref, wd_ref

    bb, _, cap, d = xs_ref.shape
    xs = xs_ref[...].reshape(bb * cap, d)
    ta = ta_ref[...].reshape(bb * cap, LANES)[:, :1]
    gate = jnp.dot(xs, wgb[0], preferred_element_type=F32)
    up = jnp.dot(xs, wub[0], preferred_element_type=F32)
    hid = (gate * jax.nn.sigmoid(gate) * up).astype(BF16)
    y = jnp.dot(hid, wdb[0], preferred_element_type=F32) * ta
    ys_ref[...] = y.astype(BF16).reshape(bb, 1, cap, d)


def _moe_scatter_kernel(slot_ref, ys_ref, x_ref, gate_ref, *rest, cap, final):
    if final:
        fg_ref, o_ref = rest
    else:
        (o_ref,) = rest
    slot = slot_ref[0]
    n_exp = slot.shape[1]
    if cap % LANES == 0:
        want = lax.broadcasted_iota(jnp.int32, (1, cap), 1)
        hit = jnp.concatenate([jnp.where(slot[:, e:e + 1] == want, 1.0, 0.0).astype(BF16) for e in range(n_exp)],
                              axis=1)
    else:
        col = lax.broadcasted_iota(jnp.int32, (n_exp, n_exp * cap), 1)
        row = lax.broadcasted_iota(jnp.int32, (n_exp, n_exp * cap), 0)
        spread = jnp.where(col // cap == row, 1.0, 0.0).astype(BF16)
        want = (lax.broadcasted_iota(jnp.int32, (1, n_exp * cap), 1) % cap).astype(F32)
        ids = jnp.dot(slot.astype(F32).astype(BF16), spread, preferred_element_type=F32)
        hit = jnp.where(ids == want, 1.0, 0.0).astype(BF16)
    ys = ys_ref[0].reshape(n_exp * cap, ys_ref.shape[-1])
    out = x_ref[0] + gate_ref[0] * jnp.dot(hit, ys, preferred_element_type=F32)
    if final:
        out = _rms(out) * fg_ref[...]
    o_ref[0] = out


def moe_layer(x, hx, logits, layer, w_gate, w_up, w_down, gate, final_g=None, w_bf16=None):
    b, n, d = x.shape
    n_exp = logits.shape[1]
    cap = EC_CAPACITY_FACTOR * n // n_exp
    per_b2 = pl.BlockSpec((1, 1, d), lambda bi, i: (bi, 0, 0))
    tri = (np.arange(LANES)[:, None] < np.arange(LANES)[None, :]).astype(np.float32)
    aff, slot = pl.pallas_call(
        functools.partial(_moe_route_kernel, cap=cap),
        grid=(1,),
        in_specs=[pl.BlockSpec((b, n_exp, n), lambda i: (0, 0, 0)),
                  pl.BlockSpec((LANES, LANES), lambda i: (0, 0))],
        out_specs=[pl.BlockSpec((b, n_exp, n), lambda i: (0, 0, 0))] * 2,
        out_shape=[jax.ShapeDtypeStruct((b, n_exp, n), F32), jax.ShapeDtypeStruct((b, n_exp, n), jnp.int32)],
        compiler_params=_cparams(("arbitrary",)),
        name="moe_route",
    )(logits, jnp.asarray(tri, BF16))

    rows_per_step = 1024
    ge = max(1, min(n_exp, 2 * rows_per_step // cap))
    xs, ta = pl.pallas_call(
        functools.partial(_moe_gather_kernel, cap=cap, ge=ge),
        grid=(b, n_exp // ge),
        in_specs=[pl.BlockSpec((1, n_exp, n), lambda bi, e: (bi, 0, 0)),
                  pl.BlockSpec((1, n_exp, n), lambda bi, e: (bi, 0, 0)),
                  pl.BlockSpec((1, n, d), lambda bi, e: (bi, 0, 0))],
        out_specs=[pl.BlockSpec((1, ge, cap, d), lambda bi, e: (bi, e, 0, 0)),
                   pl.BlockSpec((1, ge, cap, LANES), lambda bi, e: (bi, e, 0, 0))],
        out_shape=[jax.ShapeDtypeStruct((b, n_exp, cap, d), BF16),
                   jax.ShapeDtypeStruct((b, n_exp, cap, LANES), F32)],
        compiler_params=_cparams(("parallel", "arbitrary"), VMEM_LIMIT),
        name="moe_gather",
    )(slot, aff, hx)

    bb = max(1, min(b, rows_per_step // cap))
    f = w_gate.shape[-1]
    row_specs = [pl.BlockSpec((bb, 1, cap, d), lambda e, bi: (bi, e, 0, 0)),
                 pl.BlockSpec((bb, 1, cap, LANES), lambda e, bi: (bi, e, 0, 0))]
    ys_spec = pl.BlockSpec((bb, 1, cap, d), lambda e, bi: (bi, e, 0, 0))
    ys_shape = jax.ShapeDtypeStruct((b, n_exp, cap, d), BF16)
    if w_bf16 is None:
        cast_specs = [pl.BlockSpec((1, d, f), lambda e, bi: (e, 0, 0)), pl.BlockSpec((1, d, f), lambda e, bi: (e, 0, 0)),
                      pl.BlockSpec((1, f, d), lambda e, bi: (e, 0, 0))]
        cast_shapes = [jax.ShapeDtypeStruct((n_exp, d, f), BF16), jax.ShapeDtypeStruct((n_exp, d, f), BF16),
                       jax.ShapeDtypeStruct((n_exp, f, d), BF16)]
        ys, *w_bf16 = pl.pallas_call(
            _moe_ffn_kernel,
            grid=(n_exp, b // bb),
            in_specs=row_specs + [pl.BlockSpec((1, 1, d, f), lambda e, bi: (layer, e, 0, 0)),
                                  pl.BlockSpec((1, 1, d, f), lambda e, bi: (layer, e, 0, 0)),
                                  pl.BlockSpec((1, 1, f, d), lambda e, bi: (layer, e, 0, 0))],
            out_specs=[ys_spec] + cast_specs,
            out_shape=[ys_shape] + cast_shapes,
            compiler_params=_cparams(("parallel", "arbitrary"), VMEM_LIMIT),
            name="moe_ffn",
        )(xs, ta, w_gate, w_up, w_down)
    else:
        ys = pl.pallas_call(
            _moe_ffn_kernel,
            grid=(n_exp, b // bb),
            in_specs=row_specs + [pl.BlockSpec((1, d, f), lambda e, bi: (e, 0, 0)),
                                  pl.BlockSpec((1, d, f), lambda e, bi: (e, 0, 0)),
                                  pl.BlockSpec((1, f, d), lambda e, bi: (e, 0, 0))],
            out_specs=ys_spec,
            out_shape=ys_shape,
            compiler_params=_cparams(("parallel", "arbitrary"), VMEM_LIMIT),
            name="moe_ffn",
        )(xs, ta, *w_bf16)

    slot_t = jnp.swapaxes(slot, 1, 2)
    tn = min(1024, n)
    args = [slot_t, ys, x, gate]
    specs = [pl.BlockSpec((1, tn, n_exp), lambda bi, i: (bi, i, 0)),
             pl.BlockSpec((1, n_exp, cap, d), lambda bi, i: (bi, 0, 0, 0)),
             pl.BlockSpec((1, tn, d), lambda bi, i: (bi, i, 0)), per_b2]
    if final_g is not None:
        args.append(final_g)
        specs.append(pl.BlockSpec((1, d), lambda bi, i: (0, 0)))
    out = pl.pallas_call(
        functools.partial(_moe_scatter_kernel, cap=cap, final=final_g is not None),
        grid=(b, n // tn),
        in_specs=specs,
        out_specs=pl.BlockSpec((1, tn, d), lambda bi, i: (bi, i, 0)),
        out_shape=jax.ShapeDtypeStruct((b, n, d), F32),
        compiler_params=_cparams(("parallel", "arbitrary"), VMEM_LIMIT),
        name="moe_scatter",
    )(*args)
    return out, tuple(w_bf16)


def kernel(x, c, ctx, c_ctx, ada_w, ada_b, mixer_norm_g, moe_norm_g, router_w, moe_w_gate, moe_w_up, moe_w_down, da_w_in, da_lambda_q1, da_lambda_k1, da_lambda_q2, da_lambda_k2, da_subln_g, da_w_out, gm_w_in, gm_v_g, gm_w_s, gm_b_s, gm_w_out, na_w_in, na_rpb, na_w_out, fn_w_out, final_norm_g):
    b, n, d = x.shape
    rows = n // GRID_W
    readers = [i for i in range(DEPTH) if i % N_MIXERS in CTX_READERS]
    last_reader = max(readers) if readers else -1

    r_pad = -(-(b + 1) // 8) * 8
    cond = jnp.zeros((r_pad, d), F32).at[:b].set(c).at[b].set(c_ctx)
    mods = ada_params(cond, ada_w, ada_b)

    def lat_mod(i, k):
        return mods[i, :b, k * d:(k + 1) * d].reshape(b, 1, d)

    def ctx_mod(i, k):
        return jnp.broadcast_to(mods[i, b, k * d:(k + 1) * d].reshape(1, 1, d), (b, 1, d))

    rope = _rope_tables(n)
    perm = _da_perm()

    for i in range(DEPTH):
        kind, j = i % N_MIXERS, i // N_MIXERS
        need_ctx = i <= last_reader
        update_ctx = i < last_reader
        mg = mixer_norm_g[i].reshape(1, d)
        xnorm = (mg, lat_mod(i, 0), lat_mod(i, 1))
        cnorm = (mg, ctx_mod(i, 0), ctx_mod(i, 1)) if need_ctx else None
        g1 = lat_mod(i, 2)
        wr = _router_split(router_w[i])
        moe_g = moe_norm_g[i].reshape(1, d)
        xmoe = (moe_g, lat_mod(i, 3), lat_mod(i, 4), wr)
        cmoe = (moe_g, ctx_mod(i, 3), ctx_mod(i, 4), wr) if update_ctx else None

        if kind == MIX_DIFF:
            lam_init = 0.8 - 0.6 * math.exp(-0.3 * i)
            cols = np.arange(2 * DA_WIDTH).reshape(2 * DA_HEADS, LANES)[:, perm].reshape(-1)
            w_in = jnp.concatenate([da_w_in[j][:, cols], da_w_in[j][:, 2 * DA_WIDTH:]], axis=1).astype(BF16)
            w_out = da_w_out[j].astype(BF16)
            lams = [t[j].reshape(1, DA_HEAD_DIM) for t in (da_lambda_q1, da_lambda_k1, da_lambda_q2, da_lambda_k2)]
            sg = da_subln_g[j].reshape(DA_V_DIM, 1)
            nb = DA_WIDTH // LANES
            qkv = linear(x, w_in, norm=xnorm, rope=rope, rope_tiles=2)
            if update_ctx:
                qkv_c = linear(ctx, w_in, norm=cnorm)
                ctx_src = (qkv_c, nb, 2 * nb)
            else:
                ctx_src = (linear(ctx, w_in[:, DA_WIDTH:], norm=cnorm), 0, nb)
            o = diff_attention(qkv, [ctx_src, (qkv, nb, 2 * nb)], lams, sg, lam_init, heads_per_step=2)
            x, hx, logits = linear(o, w_out, res=(x, g1), moe=xmoe, out_dtype=F32)
            if update_ctx:
                oc = diff_attention(qkv_c, [ctx_src], lams, sg, lam_init, heads_per_step=DA_HEADS)
                ctx, hc, logits_c = linear(oc, w_out, res=(ctx, ctx_mod(i, 2)), moe=cmoe, out_dtype=F32)
        elif kind == MIX_GMLP:
            w_in = gm_w_in[j].astype(BF16)
            w_out = gm_w_out[j].astype(BF16)
            vg = gm_v_g[j].reshape(1, GM_WIDTH)
            w_s = gm_w_s[j].astype(BF16)
            b_s_t = jnp.transpose(gm_b_s[j])
            x, hx, logits = gmlp_layer(x, xnorm, w_in, vg, w_s, b_s_t, w_out, g1, xmoe)
            if update_ctx:
                ctx, hc, logits_c = gmlp_layer(ctx, cnorm, w_in, vg, w_s, b_s_t, w_out, ctx_mod(i, 2), cmoe)
        elif kind == MIX_NATTEN:
            w_in = na_w_in[j].astype(BF16)
            w_out = na_w_out[j].astype(BF16)
            qkv = linear(x, w_in, norm=xnorm)
            kv_c = linear(ctx, w_in[:, NA_WIDTH:], norm=cnorm)
            o = natten(qkv, kv_c, _natten_bias(na_rpb[j], rows))
            x, hx, logits = linear(o, w_out, res=(x, g1), moe=xmoe, out_dtype=F32)
            if update_ctx:
                raise NotImplementedError("context update after a neighbourhood layer")
        else:
            x, hx, logits = fourier_layer(x, xnorm, fn_w_out[j].astype(BF16), g1, xmoe)
            if update_ctx:
                ctx, hc, logits_c = fourier_layer(ctx, cnorm, fn_w_out[j].astype(BF16), ctx_mod(i, 2), cmoe)

        last = i == DEPTH - 1
        x, w_bf16 = moe_layer(x, hx, logits, i, moe_w_gate, moe_w_up, moe_w_down, lat_mod(i, 5),
                              final_g=final_norm_g.reshape(1, d) if last else None)
        if update_ctx:
            ctx, _ = moe_layer(ctx, hc, logits_c, i, moe_w_gate, moe_w_up, moe_w_down, ctx_mod(i, 5), w_bf16=w_bf16)
    return x
```

```python
import functools
import math

import numpy as np
import jax
import jax.numpy as jnp
from jax import lax
from jax.experimental import pallas as pl
from jax.experimental.pallas import tpu as pltpu

F32 = jnp.float32
BF16 = jnp.bfloat16

D_MODEL = 1024
DEPTH = 4
GRID_W = 64
N_MIXERS = 4
MIX_DIFF, MIX_GMLP, MIX_NATTEN, MIX_FOURIER = 0, 1, 2, 3
CTX_READERS = (MIX_DIFF, MIX_NATTEN)
EPS = 1e-6
NEG_INF = -1e30
ROPE_BASE = 10000.0

DA_HEADS = 8
DA_HEAD_DIM = 64
DA_V_DIM = 2 * DA_HEAD_DIM
DA_WIDTH = DA_HEADS * DA_V_DIM

GM_CHUNK = 128
GM_GROUPS = 8
GM_WIDTH = 2 * D_MODEL
GM_GROUP_DIM = GM_WIDTH // GM_GROUPS

NA_HEADS = 16
NA_HEAD_DIM = D_MODEL // NA_HEADS
NA_WIDTH = NA_HEADS * NA_HEAD_DIM
NA_WIN_R = 8
NA_WIN_C = 16
NA_ROW_BLOCK = 4
NA_SLAB_ROWS = NA_ROW_BLOCK + NA_WIN_R

FN_GROUPS = 4
DFT_SPLIT = 256

N_EXPERTS = 16
EC_CAPACITY_FACTOR = 2

LANES = 128
VMEM_LIMIT = 56 * 1024 * 1024


def _cparams(sem, vmem=None):
    return pltpu.CompilerParams(dimension_semantics=sem, vmem_limit_bytes=vmem)


def _rms(x):
    return x * lax.rsqrt(jnp.mean(x * x, axis=-1, keepdims=True) + EPS)


def _ada_kernel(c_ref, w_ref, b_ref, o_ref):
    c = c_ref[...]
    h = (c * jax.nn.sigmoid(c)).astype(BF16)
    o_ref[0] = jnp.dot(h, w_ref[0].astype(BF16), preferred_element_type=F32) + b_ref[0]


def ada_params(cond, ada_w, ada_b):
    r = cond.shape[0]
    depth, d, n_out = ada_w.shape
    tn = 1024
    return pl.pallas_call(
        _ada_kernel,
        grid=(depth, n_out // tn),
        in_specs=[pl.BlockSpec((r, d), lambda l, j: (0, 0)),
                  pl.BlockSpec((1, d, tn), lambda l, j: (l, 0, j)),
                  pl.BlockSpec((1, 1, tn), lambda l, j: (l, 0, j))],
        out_specs=pl.BlockSpec((1, r, tn), lambda l, j: (l, 0, j)),
        out_shape=jax.ShapeDtypeStruct((depth, r, n_out), F32),
        compiler_params=_cparams(("parallel", "parallel")),
        name="ada_params",
    )(cond, ada_w, ada_b.reshape(depth, 1, n_out))


def _moe_prologue(xn, mg_ref, sh_ref, sc_ref, wr_ref, hx_ref, lg_ref):
    h = _rms(xn) * mg_ref[...]
    h = h * (1.0 + sc_ref[0]) + sh_ref[0]
    h_hi = h.astype(BF16)
    h_lo = (h - h_hi.astype(F32)).astype(BF16)
    hx_ref[0] = h_hi
    a = jnp.dot(h_hi, wr_ref[...], preferred_element_type=F32)
    b = jnp.dot(h_lo, wr_ref[:, :LANES], preferred_element_type=F32)
    lt = a[:, :LANES] + a[:, LANES:] + b
    lg_ref[0] = jnp.transpose(lt)[:lg_ref.shape[1]]


def _router_split(w_router):
    w = jnp.pad(w_router, ((0, 0), (0, LANES - w_router.shape[1])))
    hi = w.astype(BF16)
    return jnp.concatenate([hi, (w - hi.astype(F32)).astype(BF16)], axis=1)


def _moe_prologue_specs(d, tm, n_exp):
    in_specs = [pl.BlockSpec((1, d), lambda bi, i: (0, 0)),
                pl.BlockSpec((1, 1, d), lambda bi, i: (bi, 0, 0)),
                pl.BlockSpec((1, 1, d), lambda bi, i: (bi, 0, 0)),
                pl.BlockSpec((d, 2 * LANES), lambda bi, i: (0, 0))]
    out_specs = [pl.BlockSpec((1, tm, d), lambda bi, i: (bi, i, 0)),
                 pl.BlockSpec((1, n_exp, tm), lambda bi, i: (bi, 0, i))]
    return in_specs, out_specs


def _linear_kernel(*refs, has_norm, has_res, has_moe, rope_tiles, tn):
    it = iter(refs)
    x_ref = next(it)
    if has_norm:
        g_ref, sh_ref, sc_ref = next(it), next(it), next(it)
    w_ref = next(it)
    if rope_tiles:
        cos_ref, sin_ref = next(it), next(it)
    if has_res:
        res_ref, gate_ref = next(it), next(it)
    if has_moe:
        moe_in = [next(it) for _ in range(4)]
    o_ref = next(it)

    if has_norm:
        h = _rms(x_ref[0]) * g_ref[...]
        hb = (h * (1.0 + sc_ref[0]) + sh_ref[0]).astype(BF16)
    else:
        hb = x_ref[0].astype(BF16)
    for j in range(w_ref.shape[1] // tn):
        cols = slice(j * tn, (j + 1) * tn)
        y = jnp.dot(hb, w_ref[:, cols], preferred_element_type=F32)
        if j < rope_tiles:
            cos, sin = cos_ref[...], sin_ref[...]
            for s in range(tn // LANES):
                seg = y[:, s * LANES:(s + 1) * LANES]
                rot = pltpu.roll(seg, LANES // 2, axis=1)
                lanes = slice(j * tn + s * LANES, j * tn + (s + 1) * LANES)
                o_ref[0, :, lanes] = (seg * cos + rot * sin).astype(o_ref.dtype)
        elif has_res:
            o_ref[0, :, cols] = (res_ref[0, :, cols] + gate_ref[0, :, cols] * y).astype(o_ref.dtype)
        else:
            o_ref[0, :, cols] = y.astype(o_ref.dtype)
    if has_moe:
        _moe_prologue(o_ref[0], *moe_in, next(it), next(it))


def linear(x, w, *, norm=None, res=None, rope=None, rope_tiles=0, moe=None, out_dtype=BF16, tm=1024, tn=1024):
    b, n, k = x.shape
    m = w.shape[1]
    tm = min(tm, n)
    tn = min(tn, m)
    args, specs = [x], [pl.BlockSpec((1, tm, k), lambda bi, i: (bi, i, 0))]
    if norm is not None:
        g, sh, sc = norm
        args += [g, sh, sc]
        specs += [pl.BlockSpec((1, k), lambda bi, i: (0, 0)),
                  pl.BlockSpec((1, 1, k), lambda bi, i: (bi, 0, 0)),
                  pl.BlockSpec((1, 1, k), lambda bi, i: (bi, 0, 0))]
    args.append(w)
    specs.append(pl.BlockSpec((k, m), lambda bi, i: (0, 0)))
    if rope is not None:
        args += list(rope)
        specs += [pl.BlockSpec((tm, LANES), lambda bi, i: (i, 0))] * 2
    if res is not None:
        r, gate = res
        args += [r, gate]
        specs += [pl.BlockSpec((1, tm, m), lambda bi, i: (bi, i, 0)),
                  pl.BlockSpec((1, 1, m), lambda bi, i: (bi, 0, 0))]
    out_specs = [pl.BlockSpec((1, tm, m), lambda bi, i: (bi, i, 0))]
    out_shape = [jax.ShapeDtypeStruct((b, n, m), out_dtype)]
    if moe is not None:
        moe_in, moe_out = _moe_prologue_specs(m, tm, N_EXPERTS)
        args += list(moe)
        specs += moe_in
        out_specs += moe_out
        out_shape += [jax.ShapeDtypeStruct((b, n, m), BF16), jax.ShapeDtypeStruct((b, N_EXPERTS, n), F32)]
    kern = functools.partial(_linear_kernel, has_norm=norm is not None, has_res=res is not None,
                             has_moe=moe is not None, rope_tiles=rope_tiles if rope is not None else 0, tn=tn)
    out = pl.pallas_call(
        kern,
        grid=(b, n // tm),
        in_specs=specs,
        out_specs=out_specs,
        out_shape=out_shape,
        compiler_params=_cparams(("parallel", "parallel"), VMEM_LIMIT),
        name="linear",
    )(*args)
    return out if moe is not None else out[0]


DA_VT_ROWS = DA_V_DIM + 16
ATT_TQ = 256
ATT_KC = 768


def _interleave(stage_a, stage_b):
    for i in range(max(len(stage_a), len(stage_b))):
        if i < len(stage_a):
            stage_a[i]()
        if i < len(stage_b):
            stage_b[i]()


def _diff_attn_kernel(*refs, n_src, lam_init):
    q_ref = refs[0]
    k_refs = refs[1:1 + n_src]
    v_refs = refs[1 + n_src:1 + 2 * n_src]
    lq1_ref, lk1_ref, lq2_ref, lk2_ref, g_ref, o_ref, k_all, vt_all, s_scr, e_scr = refs[1 + 2 * n_src:]
    n_heads = q_ref.shape[2] // LANES
    lanes = [slice(hh * LANES, (hh + 1) * LANES) for hh in range(n_heads)]

    for hh in range(n_heads):
        row = 0
        for k_ref, v_ref in zip(k_refs, v_refs):
            n = k_ref.shape[1]
            k_all[hh, row:row + n, :] = k_ref[0, :, lanes[hh]]
            vt_all[hh, :DA_V_DIM, row:row + n] = jnp.transpose(v_ref[0, :, lanes[hh]].astype(F32)).astype(BF16)
            row += n
        vt_all[hh, DA_V_DIM:, :] = jnp.ones((DA_VT_ROWS - DA_V_DIM, vt_all.shape[2]), BF16)

    lam = (jnp.exp(jnp.sum(lq1_ref[...] * lk1_ref[...], keepdims=True))
           - jnp.exp(jnp.sum(lq2_ref[...] * lk2_ref[...], keepdims=True)) + lam_init)
    lane = lax.broadcasted_iota(jnp.int32, (1, LANES), 1)
    first = (lane % DA_HEAD_DIM) < (DA_HEAD_DIM // 2)
    dims = (((1,), (1,)), ((), ()))
    c = DA_HEAD_DIM ** -0.5 * math.log2(math.e)
    nk = k_all.shape[1]
    tq = min(ATT_TQ, q_ref.shape[1])
    items = [(hh, t) for hh in range(n_heads) for t in range(q_ref.shape[1] // tq)]
    state = [dict(m=None) for _ in items]
    chunks = [slice(r, min(r + ATT_KC, nk)) for r in range(0, nk, ATT_KC)]

    def scores(w):
        hh, t = items[w]
        q = q_ref[0, t * tq:(t + 1) * tq, lanes[hh]]
        zero = jnp.zeros_like(q)
        q2 = jnp.concatenate([jnp.where(first, q, zero), jnp.where(first, zero, q)], axis=0)

        def chunk(rows):
            s = lax.dot_general(k_all[hh, rows, :], q2, dims, preferred_element_type=F32) * c
            s_scr[w % 2, rows, :] = s
            m = jnp.max(s, axis=0, keepdims=True)
            state[w]["m"] = m if state[w]["m"] is None else jnp.maximum(state[w]["m"], m)

        return [functools.partial(chunk, rows) for rows in chunks]

    def exps(w):
        def chunk(rows):
            e_scr[w % 2, rows, :] = jnp.exp2(s_scr[w % 2, rows, :] - state[w]["m"]).astype(BF16)

        return [functools.partial(chunk, rows) for rows in chunks]

    def values(w):
        hh, t = items[w]
        ot = jnp.dot(vt_all[hh], e_scr[w % 2], preferred_element_type=F32)
        ot = ot[:DA_V_DIM] / ot[DA_V_DIM:DA_V_DIM + 1]
        ot = ot[:, :tq] - lam * ot[:, tq:]
        ot = ot * lax.rsqrt(jnp.mean(ot * ot, axis=0, keepdims=True) + EPS) * g_ref[...] * (1.0 - lam_init)
        o_ref[0, t * tq:(t + 1) * tq, lanes[hh]] = jnp.transpose(ot).astype(o_ref.dtype)

    for w in range(len(items) + 2):
        if 0 <= w - 2 < len(items):
            values(w - 2)
        _interleave(scores(w) if w < len(items) else [],
                    (exps(w - 1) if 0 <= w - 1 < len(items) else []))


def diff_attention(q_arr, srcs, lams, subln_g, lam_init, heads_per_step=1):
    b, nq, _ = q_arr.shape
    nk = sum(a.shape[1] for a, _, _ in srcs)
    tq = min(ATT_TQ, nq)
    hps = heads_per_step
    w = hps * LANES
    assert all(kc % hps == 0 and vc % hps == 0 for _, kc, vc in srcs)
    vec = pl.BlockSpec((1, DA_HEAD_DIM), lambda bi, h: (0, 0))
    k_specs = [pl.BlockSpec((1, a.shape[1], w), lambda bi, h, c0=kc // hps: (bi, 0, c0 + h)) for a, kc, _ in srcs]
    v_specs = [pl.BlockSpec((1, a.shape[1], w), lambda bi, h, c0=vc // hps: (bi, 0, c0 + h)) for a, _, vc in srcs]
    arrs = [a for a, _, _ in srcs]
    return pl.pallas_call(
        functools.partial(_diff_attn_kernel, n_src=len(srcs), lam_init=lam_init),
        grid=(b, DA_HEADS // hps),
        in_specs=[pl.BlockSpec((1, nq, w), lambda bi, h: (bi, 0, h))] + k_specs + v_specs
                 + [vec, vec, vec, vec, pl.BlockSpec((DA_V_DIM, 1), lambda bi, h: (0, 0))],
        out_specs=pl.BlockSpec((1, nq, w), lambda bi, h: (bi, 0, h)),
        out_shape=jax.ShapeDtypeStruct((b, nq, DA_WIDTH), BF16),
        scratch_shapes=[pltpu.VMEM((hps, nk, LANES), BF16), pltpu.VMEM((hps, DA_VT_ROWS, nk), BF16),
                        pltpu.VMEM((2, nk, 2 * tq), F32), pltpu.VMEM((2, nk, 2 * tq), BF16)],
        compiler_params=_cparams(("parallel", "parallel"), VMEM_LIMIT),
        name="diff_attention",
    )(q_arr, *arrs, *arrs, *lams, subln_g)


def _da_perm():
    perm = np.zeros(LANES, np.int32)
    for l in range(LANES):
        part, within = divmod(l, 64)
        j, rem = divmod(within, 32)
        seg, i = divmod(rem, 16)
        perm[l] = j * 64 + seg * 32 + part * 16 + i
    return perm


def _rope_tables(n_tok):
    t = jnp.arange(n_tok, dtype=jnp.int32)
    n_freq = DA_HEAD_DIM // 4
    inv = ROPE_BASE ** (-jnp.arange(n_freq, dtype=F32) / n_freq)
    ang_r = (t // GRID_W).astype(F32)[:, None] * inv
    ang_c = (t % GRID_W).astype(F32)[:, None] * inv
    cos32 = jnp.concatenate([jnp.cos(ang_r), jnp.cos(ang_c)], axis=1)
    sin32 = jnp.concatenate([jnp.sin(ang_r), jnp.sin(ang_c)], axis=1)
    cos = jnp.tile(cos32, (1, 4))
    sin = jnp.concatenate([-sin32, -sin32, sin32, sin32], axis=1)
    return cos, sin


def _gmlp_kernel(x_ref, g_ref, sh_ref, sc_ref, win_ref, vg_ref, ws_ref, bs_ref, wout_ref, gate_ref,
                 mg_ref, sh2_ref, sc2_ref, wr_ref, o_ref, hx_ref, lg_ref, *, tm):
    x = x_ref[0]
    h = _rms(x) * g_ref[...]
    h = (h * (1.0 + sc_ref[0]) + sh_ref[0]).astype(BF16)
    uv = jax.nn.gelu(jnp.dot(h, win_ref[...], preferred_element_type=F32), approximate=True)
    u = uv[:, :GM_WIDTH]
    v = (_rms(uv[:, GM_WIDTH:]) * vg_ref[...]).astype(BF16)
    rows = []
    for c in range(tm // GM_CHUNK):
        cols = []
        for gi in range(GM_GROUPS):
            vv = v[c * GM_CHUNK:(c + 1) * GM_CHUNK, gi * GM_GROUP_DIM:(gi + 1) * GM_GROUP_DIM]
            sv = jnp.dot(ws_ref[gi], vv, preferred_element_type=F32) + bs_ref[:, gi:gi + 1]
            cols.append(sv)
        rows.append(jnp.concatenate(cols, axis=1))
    sv = jnp.concatenate(rows, axis=0) if len(rows) > 1 else rows[0]
    y = jnp.dot((u * sv).astype(BF16), wout_ref[...], preferred_element_type=F32)
    xn = x + gate_ref[0] * y
    o_ref[0] = xn
    _moe_prologue(xn, mg_ref, sh2_ref, sc2_ref, wr_ref, hx_ref, lg_ref)


def gmlp_layer(x, norm, w_in, v_g, w_s, b_s_t, w_out, gate, moe, tm=512):
    b, n, d = x.shape
    tm = min(tm, n)
    g, sh, sc = norm
    moe_in, moe_out = _moe_prologue_specs(d, tm, N_EXPERTS)
    full = lambda shape: pl.BlockSpec(shape, lambda bi, i: (0,) * len(shape))
    per_b = pl.BlockSpec((1, 1, d), lambda bi, i: (bi, 0, 0))
    return pl.pallas_call(
        functools.partial(_gmlp_kernel, tm=tm),
        grid=(b, n // tm),
        in_specs=[pl.BlockSpec((1, tm, d), lambda bi, i: (bi, i, 0)),
                  full((1, d)), per_b, per_b,
                  full(w_in.shape), full((1, GM_WIDTH)), full(w_s.shape), full(b_s_t.shape),
                  full(w_out.shape), per_b] + moe_in,
        out_specs=[pl.BlockSpec((1, tm, d), lambda bi, i: (bi, i, 0))] + moe_out,
        out_shape=[jax.ShapeDtypeStruct((b, n, d), F32), jax.ShapeDtypeStruct((b, n, d), BF16),
                   jax.ShapeDtypeStruct((b, N_EXPERTS, n), F32)],
        compiler_params=_cparams(("parallel", "arbitrary"), VMEM_LIMIT),
        name="gmlp_layer",
    )(x, g, sh, sc, w_in, v_g, w_s, b_s_t, w_out, gate, *moe)


NA_VT_ROWS = LANES + 16


def _natten_kernel(q_ref, k_ref, v_ref, kc_ref, vc_ref, bias_ref, o_ref, vt_all, vct, s_scr, e_scr, *, n_blocks):
    tq = NA_ROW_BLOCK * GRID_W
    slab_blocks = NA_SLAB_ROWS // NA_ROW_BLOCK
    slab = slab_blocks * tq
    lc = kc_ref.shape[1]
    n_pairs = q_ref.shape[2] // LANES
    lanes = [slice(p * LANES, (p + 1) * LANES) for p in range(n_pairs)]
    for p in range(n_pairs):
        for blk in range(n_blocks):
            vt_all[p, :LANES, blk * tq:(blk + 1) * tq] = jnp.transpose(
                v_ref[0, blk * tq:(blk + 1) * tq, lanes[p]].astype(F32)).astype(BF16)
        vt_all[p, LANES:, :] = jnp.ones((NA_VT_ROWS - LANES, vt_all.shape[2]), BF16)
        vct[p, :LANES, :] = jnp.transpose(vc_ref[0, :, lanes[p]].astype(F32)).astype(BF16)
        vct[p, LANES:, :] = jnp.ones((NA_VT_ROWS - LANES, lc), BF16)

    lane = lax.broadcasted_iota(jnp.int32, (1, LANES), 1)
    left = lane < NA_HEAD_DIM
    dims = (((1,), (1,)), ((), ()))
    c = NA_HEAD_DIM ** -0.5 * math.log2(math.e)
    items = [(p, t) for p in range(n_pairs) for t in range(n_blocks)]
    state = [dict(m=None) for _ in items]
    key0 = [min(max(t - 1, 0), n_blocks - slab_blocks) * tq for t in range(n_blocks)]
    kind = [0 if t == 0 else 2 if t == n_blocks - 1 else 1 for t in range(n_blocks)]

    def scores(w):
        p, t = items[w]
        q = q_ref[0, t * tq:(t + 1) * tq, lanes[p]]
        zero = jnp.zeros_like(q)
        q2 = jnp.concatenate([jnp.where(left, q, zero), jnp.where(left, zero, q)], axis=0)

        def local():
            k = k_ref[0, key0[t]:key0[t] + slab, lanes[p]]
            s = lax.dot_general(k, q2, dims, preferred_element_type=F32) * c + bias_ref[p, kind[t]]
            s_scr[w % 2, :slab, :] = s
            state[w]["m"] = jnp.max(s, axis=0, keepdims=True)

        def context():
            s = lax.dot_general(kc_ref[0, :, lanes[p]], q2, dims, preferred_element_type=F32) * c
            s_scr[w % 2, slab:, :] = s
            state[w]["m"] = jnp.maximum(state[w]["m"], jnp.max(s, axis=0, keepdims=True))

        return [local, context]

    def exps(w):
        def chunk(rows):
            e_scr[w % 2, rows, :] = jnp.exp2(s_scr[w % 2, rows, :] - state[w]["m"]).astype(BF16)

        return [functools.partial(chunk, slice(0, slab)), functools.partial(chunk, slice(slab, slab + lc))]

    def values(w):
        p, t = items[w]
        vt = jnp.concatenate([vt_all[p, :, key0[t]:key0[t] + slab], vct[p]], axis=1)
        ot = jnp.dot(vt, e_scr[w % 2], preferred_element_type=F32)
        ot = ot[:LANES] / ot[LANES:LANES + 1]
        pair = jnp.concatenate([ot[:NA_HEAD_DIM, :tq], ot[NA_HEAD_DIM:, tq:]], axis=0)
        o_ref[0, t * tq:(t + 1) * tq, lanes[p]] = jnp.transpose(pair).astype(o_ref.dtype)

    for w in range(len(items) + 2):
        if 0 <= w - 2 < len(items):
            values(w - 2)
        _interleave(scores(w) if w < len(items) else [],
                    (exps(w - 1) if 0 <= w - 1 < len(items) else []))


def _natten_bias(rpb, rows):
    n_heads = rpb.shape[0]
    n_blocks = rows // NA_ROW_BLOCK
    n_dr, n_dc = 2 * NA_WIN_R - 1, 2 * NA_WIN_C - 1
    f = rpb[:, :, ::-1] * math.log2(math.e)
    span = 2 * GRID_W - 1
    lo = GRID_W - NA_WIN_C - 1
    u = jnp.pad(f, ((0, 0), (0, 0), (lo, span - n_dc - lo)))
    skew = jnp.tile(u, (1, 1, GRID_W))[:, :, :GRID_W * (span - 1)].reshape(n_heads, n_dr, GRID_W, span - 1)
    toep = skew[:, :, :, GRID_W - 2:2 * GRID_W - 2]
    pad = NA_ROW_BLOCK
    toep = jnp.pad(toep, ((0, 0), (pad, pad), (0, 0), (0, 0)))
    i = np.arange(NA_ROW_BLOCK)[:, None, None, None]
    c = np.arange(GRID_W)[None, :, None, None]
    m = np.arange(NA_SLAB_ROWS)[None, None, :, None]
    kc = np.arange(GRID_W)[None, None, None, :]
    win_c0 = np.clip(c - NA_WIN_C // 2, 0, GRID_W - NA_WIN_C)
    col_ok = (kc >= win_c0) & (kc < win_c0 + NA_WIN_C)
    tiles, ok = [], []
    for rb in (0, 1, n_blocks - 1):
        slab0 = int(np.clip(rb - 1, 0, n_blocks - 3)) * NA_ROW_BLOCK
        r = rb * NA_ROW_BLOCK + i
        r0 = np.clip(r - NA_WIN_R // 2, 0, rows - NA_WIN_R)
        kr = slab0 + m
        ok.append(np.broadcast_to((kr >= r0) & (kr < r0 + NA_WIN_R) & col_ok,
                                  (NA_ROW_BLOCK, GRID_W, NA_SLAB_ROWS, GRID_W)))
        for ii in range(NA_ROW_BLOCK):
            off = slab0 - (rb * NA_ROW_BLOCK + ii) + NA_WIN_R - 1 + pad
            tiles.append(toep[:, off:off + NA_SLAB_ROWS])
    tq, slab = NA_ROW_BLOCK * GRID_W, NA_SLAB_ROWS * GRID_W
    bias = jnp.stack(tiles, axis=1).reshape(n_heads // 2, 2, 3, NA_ROW_BLOCK, NA_SLAB_ROWS, GRID_W, GRID_W)
    bias = jnp.transpose(bias, (0, 2, 4, 5, 1, 3, 6)).reshape(n_heads // 2, 3, slab, 2 * tq)
    ok = np.stack(ok).transpose(0, 3, 4, 1, 2).reshape(3, slab, tq)
    ok = np.concatenate([ok, ok], axis=2)
    return jnp.where(ok[None], bias, NEG_INF)


def natten(qkv, kv_ctx, bias, pairs_per_step=2):
    b, n, _ = qkv.shape
    lc = kv_ctx.shape[1]
    tq = NA_ROW_BLOCK * GRID_W
    n_blocks = n // tq
    pps = pairs_per_step
    steps = NA_HEADS // 2 // pps
    w = pps * LANES
    return pl.pallas_call(
        functools.partial(_natten_kernel, n_blocks=n_blocks),
        grid=(steps, b),
        in_specs=[pl.BlockSpec((1, n, w), lambda p, bi: (bi, 0, p)),
                  pl.BlockSpec((1, n, w), lambda p, bi: (bi, 0, steps + p)),
                  pl.BlockSpec((1, n, w), lambda p, bi: (bi, 0, 2 * steps + p)),
                  pl.BlockSpec((1, lc, w), lambda p, bi: (bi, 0, p)),
                  pl.BlockSpec((1, lc, w), lambda p, bi: (bi, 0, steps + p)),
                  pl.BlockSpec((pps, 3, NA_SLAB_ROWS * GRID_W, 2 * tq), lambda p, bi: (p, 0, 0, 0))],
        out_specs=pl.BlockSpec((1, n, w), lambda p, bi: (bi, 0, p)),
        out_shape=jax.ShapeDtypeStruct((b, n, NA_WIDTH), BF16),
        scratch_shapes=[pltpu.VMEM((pps, NA_VT_ROWS, n), BF16), pltpu.VMEM((pps, NA_VT_ROWS, lc), BF16),
                        pltpu.VMEM((2, NA_SLAB_ROWS * GRID_W + lc, 2 * tq), F32),
                        pltpu.VMEM((2, NA_SLAB_ROWS * GRID_W + lc, 2 * tq), BF16)],
        compiler_params=_cparams(("parallel", "parallel"), VMEM_LIMIT),
        name="natten",
    )(qkv, qkv, qkv, kv_ctx, kv_ctx, bias)


def _dft_tables(n, sign=1.0):
    k = jnp.arange(n, dtype=jnp.int32)
    w = 2.0 * math.pi / n

    def cs(m):
        ang = ((k[:, None] * m[None, :]) % n).astype(F32) * w
        return jnp.cos(ang), jnp.sin(ang)

    if n <= DFT_SPLIT:
        c, s = cs(k)
        return c, sign * s
    ca, sa = cs(jnp.arange(n // DFT_SPLIT, dtype=jnp.int32) * DFT_SPLIT)
    cb, sb = cs(jnp.arange(DFT_SPLIT, dtype=jnp.int32))
    c = ca[:, :, None] * cb[:, None, :] - sa[:, :, None] * sb[:, None, :]
    s = sa[:, :, None] * cb[:, None, :] + ca[:, :, None] * sb[:, None, :]
    return c.reshape(n, n), (sign * s).reshape(n, n)


def _fourier_chan_kernel(x_ref, g_ref, sh_ref, sc_ref, wc_ref, o_ref):
    h = _rms(x_ref[0]) * g_ref[...]
    h = (h * (1.0 + sc_ref[0]) + sh_ref[0]).astype(BF16)
    gd = D_MODEL // FN_GROUPS
    for gi in range(FN_GROUPS):
        z = jnp.dot(h[:, gi * gd:(gi + 1) * gd], wc_ref[...], preferred_element_type=F32).astype(BF16)
        o_ref[0, 0, :, gi * gd:(gi + 1) * gd] = z[:, :gd]
        o_ref[0, 1, :, gi * gd:(gi + 1) * gd] = z[:, gd:]


def _fourier_pos_kernel(wp_ref, z_ref, wout_ref, x_ref, gate_ref, mg_ref, sh2_ref, sc2_ref, wr_ref,
                        o_ref, hx_ref, lg_ref, *, scale):
    f = jnp.dot(wp_ref[...], z_ref[0], preferred_element_type=F32) * scale
    y = jnp.dot(f.astype(BF16), wout_ref[...], preferred_element_type=F32)
    xn = x_ref[0] + gate_ref[0] * y
    o_ref[0] = xn
    _moe_prologue(xn, mg_ref, sh2_ref, sc2_ref, wr_ref, hx_ref, lg_ref)


def fourier_layer(x, norm, w_out, gate, moe, tm=512):
    b, n, d = x.shape
    moe_in, moe_out = _moe_prologue_specs(d, tm, N_EXPERTS)
    gd = d // FN_GROUPS
    g, sh, sc = norm
    cc, sc_tab = _dft_tables(gd)
    wc = jnp.concatenate([cc, sc_tab], axis=1).astype(BF16)
    cn, sn = _dft_tables(n, -1.0)
    wp = jnp.concatenate([cn, sn], axis=1).astype(BF16)
    per_b = pl.BlockSpec((1, 1, d), lambda bi, i: (bi, 0, 0))
    tmc = min(2 * tm, n)
    z = pl.pallas_call(
        _fourier_chan_kernel,
        grid=(b, n // tmc),
        in_specs=[pl.BlockSpec((1, tmc, d), lambda bi, i: (bi, i, 0)),
                  pl.BlockSpec((1, d), lambda bi, i: (0, 0)), per_b, per_b,
                  pl.BlockSpec((gd, 2 * gd), lambda bi, i: (0, 0))],
        out_specs=pl.BlockSpec((1, 2, tmc, d), lambda bi, i: (bi, 0, i, 0)),
        out_shape=jax.ShapeDtypeStruct((b, 2, n, d), BF16),
        compiler_params=_cparams(("parallel", "parallel")),
        name="fourier_chan",
    )(x, g, sh, sc, wc)
    z = z.reshape(b, 2 * n, d)
    return pl.pallas_call(
        functools.partial(_fourier_pos_kernel, scale=1.0 / math.sqrt(n * gd)),
        grid=(b, n // tm),
        in_specs=[pl.BlockSpec((tm, 2 * n), lambda bi, i: (i, 0)),
                  pl.BlockSpec((1, 2 * n, d), lambda bi, i: (bi, 0, 0)),
                  pl.BlockSpec((d, d), lambda bi, i: (0, 0)),
                  pl.BlockSpec((1, tm, d), lambda bi, i: (bi, i, 0)), per_b] + moe_in,
        out_specs=[pl.BlockSpec((1, tm, d), lambda bi, i: (bi, i, 0))] + moe_out,
        out_shape=[jax.ShapeDtypeStruct((b, n, d), F32), jax.ShapeDtypeStruct((b, n, d), BF16),
                   jax.ShapeDtypeStruct((b, N_EXPERTS, n), F32)],
        compiler_params=_cparams(("parallel", "arbitrary"), VMEM_LIMIT),
        name="fourier_pos",
    )(wp, z, w_out, x, gate, *moe)


def _excl_cumsum_lanes(mask, tri):
    e, n = mask.shape
    mf = jnp.where(mask, 1.0, 0.0)
    offset = jnp.zeros((e, 1), F32)
    parts = []
    for blk in range(n // LANES):
        part = mf[:, blk * LANES:(blk + 1) * LANES]
        parts.append(jnp.dot(part.astype(BF16), tri, preferred_element_type=F32) + offset)
        offset = offset + jnp.sum(part, axis=1, keepdims=True)
    return jnp.concatenate(parts, axis=1)


def _moe_route_kernel(lg_ref, tri_ref, aff_ref, slot_ref, *, cap):
    lg = lg_ref[...]
    e = jnp.exp(lg - jnp.max(lg, axis=1, keepdims=True))
    aff3 = e / jnp.sum(e, axis=1, keepdims=True)
    aff_ref[...] = aff3
    b, n_exp, n = lg.shape
    aff = aff3.reshape(b * n_exp, n)

    def as_float(bits):
        return lax.bitcast_convert_type(bits, F32)

    def step(it, thr):
        cand = thr | jnp.left_shift(jnp.int32(1), 30 - it)
        cnt = jnp.sum((aff >= as_float(cand)).astype(jnp.int32), axis=1, keepdims=True)
        return jnp.where(cnt >= cap, cand, thr)

    thr = lax.fori_loop(0, 31, step, jnp.zeros((b * n_exp, 1), jnp.int32))
    gt = aff >= as_float(thr + 1)
    eq = (aff >= as_float(thr)) & jnp.logical_not(gt)
    need = (cap - jnp.sum(gt.astype(jnp.int32), axis=1, keepdims=True)).astype(F32)
    tri = tri_ref[...]
    sel = gt | (eq & (_excl_cumsum_lanes(eq, tri) < need))
    pos = _excl_cumsum_lanes(sel, tri).astype(jnp.int32)
    slot_ref[...] = jnp.where(sel, pos, -1).reshape(b, n_exp, n)


def _moe_gather_kernel(slot_ref, aff_ref, hx_ref, xs_ref, ta_ref, *, cap, ge):
    e0 = pl.multiple_of(pl.program_id(1) * ge, ge)
    n = slot_ref.shape[2]
    rows = lax.broadcasted_iota(jnp.int32, (cap, n), 0)
    hits, tas = [], []
    for k in range(ge):
        hit = rows == slot_ref[0, pl.ds(e0 + k, 1), :]
        hits.append(jnp.where(hit, 1.0, 0.0).astype(BF16))
        tas.append(jnp.sum(jnp.where(hit, aff_ref[0, pl.ds(e0 + k, 1), :], 0.0), axis=1, keepdims=True))
    onehot = jnp.concatenate(hits, axis=0) if ge > 1 else hits[0]
    xs = jnp.dot(onehot, hx_ref[0], preferred_element_type=F32).astype(BF16)
    xs_ref[0] = xs.reshape(ge, cap, xs.shape[-1])
    ta = jnp.concatenate(tas, axis=0) if ge > 1 else tas[0]
    ta_ref[0] = jnp.broadcast_to(ta, (ge * cap, LANES)).reshape(ge, cap, LANES)


def _moe_ffn_kernel(xs_ref, ta_ref, wg_ref, wu_ref, wd_ref, ys_ref, *cast_refs):
    if cast_refs:
        wgb, wub, wdb = cast_refs

        @pl.when(pl.program_id(1) == 0)
        def _():
            wgb[0] = wg_ref[0, 0].astype(BF16)
            wub[0] = wu_ref[0, 0].astype(BF16)
            wdb[0] = wd_ref[0, 0].astype(BF16)
    else:
        wgb, wub, wdb = wg_ref, wu_ref, wd_ref

    bb, _, cap, d = xs_ref.shape
    xs = xs_ref[...].reshape(bb * cap, d)
    ta = ta_ref[...].reshape(bb * cap, LANES)[:, :1]
    gate = jnp.dot(xs, wgb[0], preferred_element_type=F32)
    up = jnp.dot(xs, wub[0], preferred_element_type=F32)
    hid = (gate * jax.nn.sigmoid(gate) * up).astype(BF16)
    y = jnp.dot(hid, wdb[0], preferred_element_type=F32) * ta
    ys_ref[...] = y.astype(BF16).reshape(bb, 1, cap, d)


def _moe_scatter_kernel(slot_ref, ys_ref, x_ref, gate_ref, *rest, cap, final):
    if final:
        fg_ref, o_ref = rest
    else:
        (o_ref,) = rest
    slot = slot_ref[0]
    n_exp = slot.shape[1]
    if cap % LANES == 0:
        want = lax.broadcasted_iota(jnp.int32, (1, cap), 1)
        hit = jnp.concatenate([jnp.where(slot[:, e:e + 1] == want, 1.0, 0.0).astype(BF16) for e in range(n_exp)],
                              axis=1)
    else:
        col = lax.broadcasted_iota(jnp.int32, (n_exp, n_exp * cap), 1)
        row = lax.broadcasted_iota(jnp.int32, (n_exp, n_exp * cap), 0)
        spread = jnp.where(col // cap == row, 1.0, 0.0).astype(BF16)
        want = (lax.broadcasted_iota(jnp.int32, (1, n_exp * cap), 1) % cap).astype(F32)
        ids = jnp.dot(slot.astype(F32).astype(BF16), spread, preferred_element_type=F32)
        hit = jnp.where(ids == want, 1.0, 0.0).astype(BF16)
    ys = ys_ref[0].reshape(n_exp * cap, ys_ref.shape[-1])
    out = x_ref[0] + gate_ref[0] * jnp.dot(hit, ys, preferred_element_type=F32)
    if final:
        out = _rms(out) * fg_ref[...]
    o_ref[0] = out


def moe_layer(x, hx, logits, layer, w_gate, w_up, w_down, gate, final_g=None, w_bf16=None):
    b, n, d = x.shape
    n_exp = logits.shape[1]
    cap = EC_CAPACITY_FACTOR * n // n_exp
    per_b2 = pl.BlockSpec((1, 1, d), lambda bi, i: (bi, 0, 0))
    tri = (np.arange(LANES)[:, None] < np.arange(LANES)[None, :]).astype(np.float32)
    aff, slot = pl.pallas_call(
        functools.partial(_moe_route_kernel, cap=cap),
        grid=(1,),
        in_specs=[pl.BlockSpec((b, n_exp, n), lambda i: (0, 0, 0)),
                  pl.BlockSpec((LANES, LANES), lambda i: (0, 0))],
        out_specs=[pl.BlockSpec((b, n_exp, n), lambda i: (0, 0, 0))] * 2,
        out_shape=[jax.ShapeDtypeStruct((b, n_exp, n), F32), jax.ShapeDtypeStruct((b, n_exp, n), jnp.int32)],
        compiler_params=_cparams(("arbitrary",)),
        name="moe_route",
    )(logits, jnp.asarray(tri, BF16))

    rows_per_step = 1024
    ge = max(1, min(n_exp, 2 * rows_per_step // cap))
    xs, ta = pl.pallas_call(
        functools.partial(_moe_gather_kernel, cap=cap, ge=ge),
        grid=(b, n_exp // ge),
        in_specs=[pl.BlockSpec((1, n_exp, n), lambda bi, e: (bi, 0, 0)),
                  pl.BlockSpec((1, n_exp, n), lambda bi, e: (bi, 0, 0)),
                  pl.BlockSpec((1, n, d), lambda bi, e: (bi, 0, 0))],
        out_specs=[pl.BlockSpec((1, ge, cap, d), lambda bi, e: (bi, e, 0, 0)),
                   pl.BlockSpec((1, ge, cap, LANES), lambda bi, e: (bi, e, 0, 0))],
        out_shape=[jax.ShapeDtypeStruct((b, n_exp, cap, d), BF16),
                   jax.ShapeDtypeStruct((b, n_exp, cap, LANES), F32)],
        compiler_params=_cparams(("parallel", "arbitrary"), VMEM_LIMIT),
        name="moe_gather",
    )(slot, aff, hx)

    bb = max(1, min(b, rows_per_step // cap))
    f = w_gate.shape[-1]
    row_specs = [pl.BlockSpec((bb, 1, cap, d), lambda e, bi: (bi, e, 0, 0)),
                 pl.BlockSpec((bb, 1, cap, LANES), lambda e, bi: (bi, e, 0, 0))]
    ys_spec = pl.BlockSpec((bb, 1, cap, d), lambda e, bi: (bi, e, 0, 0))
    ys_shape = jax.ShapeDtypeStruct((b, n_exp, cap, d), BF16)
    if w_bf16 is None:
        cast_specs = [pl.BlockSpec((1, d, f), lambda e, bi: (e, 0, 0)), pl.BlockSpec((1, d, f), lambda e, bi: (e, 0, 0)),
                      pl.BlockSpec((1, f, d), lambda e, bi: (e, 0, 0))]
        cast_shapes = [jax.ShapeDtypeStruct((n_exp, d, f), BF16), jax.ShapeDtypeStruct((n_exp, d, f), BF16),
                       jax.ShapeDtypeStruct((n_exp, f, d), BF16)]
        ys, *w_bf16 = pl.pallas_call(
            _moe_ffn_kernel,
            grid=(n_exp, b // bb),
            in_specs=row_specs + [pl.BlockSpec((1, 1, d, f), lambda e, bi: (layer, e, 0, 0)),
                                  pl.BlockSpec((1, 1, d, f), lambda e, bi: (layer, e, 0, 0)),
                                  pl.BlockSpec((1, 1, f, d), lambda e, bi: (layer, e, 0, 0))],
            out_specs=[ys_spec] + cast_specs,
            out_shape=[ys_shape] + cast_shapes,
            compiler_params=_cparams(("parallel", "arbitrary"), VMEM_LIMIT),
            name="moe_ffn",
        )(xs, ta, w_gate, w_up, w_down)
    else:
        ys = pl.pallas_call(
            _moe_ffn_kernel,
            grid=(n_exp, b // bb),
            in_specs=row_specs + [pl.BlockSpec((1, d, f), lambda e, bi: (e, 0, 0)),
                                  pl.BlockSpec((1, d, f), lambda e, bi: (e, 0, 0)),
                                  pl.BlockSpec((1, f, d), lambda e, bi: (e, 0, 0))],
            out_specs=ys_spec,
            out_shape=ys_shape,
            compiler_params=_cparams(("parallel", "arbitrary"), VMEM_LIMIT),
            name="moe_ffn",
        )(xs, ta, *w_bf16)

    slot_t = jnp.swapaxes(slot, 1, 2)
    tn = min(1024, n)
    args = [slot_t, ys, x, gate]
    specs = [pl.BlockSpec((1, tn, n_exp), lambda bi, i: (bi, i, 0)),
             pl.BlockSpec((1, n_exp, cap, d), lambda bi, i: (bi, 0, 0, 0)),
             pl.BlockSpec((1, tn, d), lambda bi, i: (bi, i, 0)), per_b2]
    if final_g is not None:
        args.append(final_g)
        specs.append(pl.BlockSpec((1, d), lambda bi, i: (0, 0)))
    out = pl.pallas_call(
        functools.partial(_moe_scatter_kernel, cap=cap, final=final_g is not None),
        grid=(b, n // tn),
        in_specs=specs,
        out_specs=pl.BlockSpec((1, tn, d), lambda bi, i: (bi, i, 0)),
        out_shape=jax.ShapeDtypeStruct((b, n, d), F32),
        compiler_params=_cparams(("parallel", "arbitrary"), VMEM_LIMIT),
        name="moe_scatter",
    )(*args)
    return out, tuple(w_bf16)


def kernel(x, c, ctx, c_ctx, ada_w, ada_b, mixer_norm_g, moe_norm_g, router_w, moe_w_gate, moe_w_up, moe_w_down, da_w_in, da_lambda_q1, da_lambda_k1, da_lambda_q2, da_lambda_k2, da_subln_g, da_w_out, gm_w_in, gm_v_g, gm_w_s, gm_b_s, gm_w_out, na_w_in, na_rpb, na_w_out, fn_w_out, final_norm_g):
    b, n, d = x.shape
    rows = n // GRID_W
    readers = [i for i in range(DEPTH) if i % N_MIXERS in CTX_READERS]
    last_reader = max(readers) if readers else -1

    r_pad = -(-(b + 1) // 8) * 8
    cond = jnp.zeros((r_pad, d), F32).at[:b].set(c).at[b].set(c_ctx)
    mods = ada_params(cond, ada_w, ada_b)

    def lat_mod(i, k):
        return mods[i, :b, k * d:(k + 1) * d].reshape(b, 1, d)

    def ctx_mod(i, k):
        return jnp.broadcast_to(mods[i, b, k * d:(k + 1) * d].reshape(1, 1, d), (b, 1, d))

    rope = _rope_tables(n)
    perm = _da_perm()

    for i in range(DEPTH):
        kind, j = i % N_MIXERS, i // N_MIXERS
        need_ctx = i <= last_reader
        update_ctx = i < last_reader
        mg = mixer_norm_g[i].reshape(1, d)
        xnorm = (mg, lat_mod(i, 0), lat_mod(i, 1))
        cnorm = (mg, ctx_mod(i, 0), ctx_mod(i, 1)) if need_ctx else None
        g1 = lat_mod(i, 2)
        wr = _router_split(router_w[i])
        moe_g = moe_norm_g[i].reshape(1, d)
        xmoe = (moe_g, lat_mod(i, 3), lat_mod(i, 4), wr)
        cmoe = (moe_g, ctx_mod(i, 3), ctx_mod(i, 4), wr) if update_ctx else None

        if kind == MIX_DIFF:
            lam_init = 0.8 - 0.6 * math.exp(-0.3 * i)
            cols = np.arange(2 * DA_WIDTH).reshape(2 * DA_HEADS, LANES)[:, perm].reshape(-1)
            w_in = jnp.concatenate([da_w_in[j][:, cols], da_w_in[j][:, 2 * DA_WIDTH:]], axis=1).astype(BF16)
            w_out = da_w_out[j].astype(BF16)
            lams = [t[j].reshape(1, DA_HEAD_DIM) for t in (da_lambda_q1, da_lambda_k1, da_lambda_q2, da_lambda_k2)]
            sg = da_subln_g[j].reshape(DA_V_DIM, 1)
            nb = DA_WIDTH // LANES
            qkv = linear(x, w_in, norm=xnorm, rope=rope, rope_tiles=2)
            if update_ctx:
                qkv_c = linear(ctx, w_in, norm=cnorm)
                ctx_src = (qkv_c, nb, 2 * nb)
            else:
                ctx_src = (linear(ctx, w_in[:, DA_WIDTH:], norm=cnorm), 0, nb)
            o = diff_attention(qkv, [ctx_src, (qkv, nb, 2 * nb)], lams, sg, lam_init, heads_per_step=2)
            x, hx, logits = linear(o, w_out, res=(x, g1), moe=xmoe, out_dtype=F32)
            if update_ctx:
                oc = diff_attention(qkv_c, [ctx_src], lams, sg, lam_init, heads_per_step=DA_HEADS)
                ctx, hc, logits_c = linear(oc, w_out, res=(ctx, ctx_mod(i, 2)), moe=cmoe, out_dtype=F32)
        elif kind == MIX_GMLP:
            w_in = gm_w_in[j].astype(BF16)
            w_out = gm_w_out[j].astype(BF16)
            vg = gm_v_g[j].reshape(1, GM_WIDTH)
            w_s = gm_w_s[j].astype(BF16)
            b_s_t = jnp.transpose(gm_b_s[j])
            x, hx, logits = gmlp_layer(x, xnorm, w_in, vg, w_s, b_s_t, w_out, g1, xmoe)
            if update_ctx:
                ctx, hc, logits_c = gmlp_layer(ctx, cnorm, w_in, vg, w_s, b_s_t, w_out, ctx_mod(i, 2), cmoe)
        elif kind == MIX_NATTEN:
            w_in = na_w_in[j].astype(BF16)
            w_out = na_w_out[j].astype(BF16)
            qkv = linear(x, w_in, norm=xnorm)
            kv_c = linear(ctx, w_in[:, NA_WIDTH:], norm=cnorm)
            o = natten(qkv, kv_c, _natten_bias(na_rpb[j], rows))
            x, hx, logits = linear(o, w_out, res=(x, g1), moe=xmoe, out_dtype=F32)
            if update_ctx:
                raise NotImplementedError("context update after a neighbourhood layer")
        else:
            x, hx, logits = fourier_layer(x, xnorm, fn_w_out[j].astype(BF16), g1, xmoe)
            if update_ctx:
                ctx, hc, logits_c = fourier_layer(ctx, cnorm, fn_w_out[j].astype(BF16), ctx_mod(i, 2), cmoe)

        last = i == DEPTH - 1
        x, w_bf16 = moe_layer(x, hx, logits, i, moe_w_gate, moe_w_up, moe_w_down, lat_mod(i, 5),
                              final_g=final_norm_g.reshape(1, d) if last else None)
        if update_ctx:
            ctx, _ = moe_layer(ctx, hc, logits_c, i, moe_w_gate, moe_w_up, moe_w_down, ctx_mod(i, 5), w_bf16=w_bf16)
    return x
```
